```python
import math
import jax, jax.numpy as jnp
from jax import lax
import numpy as np

D_MODEL = 2048
BATCH = 1
SEQ = 8192
DEPTH = 1
DEC_BATCH = 1
DEC_SEQ = 16384
PAST_LEN = 128

D_MIX = D_MODEL
D_MLSTM = D_MIX // 2
D_S5 = D_MIX - D_MLSTM
MLSTM_HEADS = 8
MLSTM_HEAD_DIM = D_MLSTM // MLSTM_HEADS
MLSTM_CHUNK = 128
CONV_WIDTH = 3
S5_GROUP_CH = 16
S5_GROUPS = D_S5 // S5_GROUP_CH
S5_STATE = 64
D_FF = 5632
N_GATES = 4 * MLSTM_HEADS
D_IN = 4 * D_MLSTM + N_GATES + D_S5
EPS = 1e-6
M_INIT = -1e30

kernel_name = "hybrid_mlstm_s5_macaron_encoder"


def rmsnorm(x, w):
    xf = x.astype(jnp.float32)
    y = xf * lax.rsqrt(jnp.mean(xf * xf, axis=-1, keepdims=True) + EPS)
    return (y * w.astype(jnp.float32)).astype(x.dtype)


def swiglu_ffn(x, w_gate, w_up, w_down):
    return (jax.nn.silu(x @ w_gate) * (x @ w_up)) @ w_down


def centred_dwconv(x, w, b):
    pad = (CONV_WIDTH - 1) // 2
    L = x.shape[1]
    xp = jnp.pad(x, ((0, 0), (pad, pad), (0, 0)))
    out = xp[:, 0:L] * w[0]
    for j in range(1, CONV_WIDTH):
        out = out + xp[:, j:j + L] * w[j]
    return out + b


def mlstm_chunkwise(q, k, v, ig, lf):
    Bb, H, L, dh = q.shape
    T = MLSTM_CHUNK
    nc = L // T
    qc = q.reshape(Bb, H, nc, T, dh).transpose(2, 0, 1, 3, 4)
    kc = k.reshape(Bb, H, nc, T, dh).transpose(2, 0, 1, 3, 4)
    vc = v.reshape(Bb, H, nc, T, dh).transpose(2, 0, 1, 3, 4)
    ic = ig.reshape(Bb, H, nc, T).transpose(2, 0, 1, 3)
    fc = lf.reshape(Bb, H, nc, T).transpose(2, 0, 1, 3)
    tril = jnp.tril(jnp.ones((T, T), dtype=bool))

    def step(carry, inp):
        C, n, m = carry
        qb, kb, vb, ib, fb = inp
        b = jnp.cumsum(fb, axis=-1)
        dmat = b[..., :, None] - b[..., None, :] + ib[..., None, :]
        dmat = jnp.where(tril, dmat, -jnp.inf)
        inter = b + m[..., None]
        m_t = jnp.maximum(jnp.max(dmat, axis=-1), inter)
        w = jnp.exp(dmat - m_t[..., None])
        s_inter = jnp.exp(inter - m_t)
        s = jnp.einsum('bhtd,bhsd->bhts', qb, kb) * w
        num = jnp.einsum('bhts,bhsd->bhtd', s, vb) + s_inter[..., None] * jnp.einsum('bhed,bhtd->bhte', C, qb)
        den = jnp.sum(s, axis=-1) + s_inter * jnp.einsum('bhd,bhtd->bht', n, qb)
        h = num / jnp.maximum(jnp.abs(den), jnp.exp(-m_t))[..., None]
        bT = b[..., -1]
        wk = bT[..., None] - b + ib
        m_new = jnp.maximum(bT + m, jnp.max(wk, axis=-1))
        sc = jnp.exp(bT + m - m_new)
        wkk = jnp.exp(wk - m_new[..., None])
        C_new = sc[..., None, None] * C + jnp.einsum('bhs,bhse,bhsd->bhed', wkk, vb, kb)
        n_new = sc[..., None] * n + jnp.einsum('bhs,bhsd->bhd', wkk, kb)
        return (C_new, n_new, m_new), h

    init = (jnp.zeros((Bb, H, dh, dh), jnp.float32),
            jnp.zeros((Bb, H, dh), jnp.float32),
            jnp.full((Bb, H), M_INIT, jnp.float32))
    _, h = lax.scan(step, init, (qc, kc, vc, ic, fc))
    return h.transpose(1, 2, 0, 3, 4).reshape(Bb, H, L, dh)


def mlstm_group(q_in, k_in, v_in, o_in, gates, conv_w, conv_b, b_igate, b_fgate, norm_w):
    Bb, L, _ = q_in.shape
    H, dh = MLSTM_HEADS, MLSTM_HEAD_DIM
    qk = jax.nn.silu(centred_dwconv(jnp.concatenate([q_in, k_in], axis=-1), conv_w, conv_b)).astype(jnp.float32)

    def heads(a):
        return a.reshape(Bb, L, H, dh).transpose(0, 2, 1, 3)

    q = heads(qk[..., :D_MLSTM]) * (dh ** -0.5)
    k = heads(qk[..., D_MLSTM:])
    v = heads(v_in.astype(jnp.float32))
    g = gates.astype(jnp.float32).reshape(Bb, L, 4, H)
    ig = (g[:, :, 0:2] + b_igate.astype(jnp.float32)).transpose(0, 2, 3, 1)
    lf = jax.nn.log_sigmoid(g[:, :, 2:4] + b_fgate.astype(jnp.float32)).transpose(0, 2, 3, 1)
    h_fwd = mlstm_chunkwise(q, k, v, ig[:, 0], lf[:, 0])
    flip = lambda a: jnp.flip(a, axis=2)
    h_bwd = flip(mlstm_chunkwise(flip(q), flip(k), flip(v), flip(ig[:, 1]), flip(lf[:, 1])))
    h = h_fwd + h_bwd
    mu = jnp.mean(h, axis=-1, keepdims=True)
    var = jnp.mean(jnp.square(h - mu), axis=-1, keepdims=True)
    hn = ((h - mu) * lax.rsqrt(var + EPS)).transpose(0, 2, 1, 3).reshape(Bb, L, D_MLSTM)
    return hn * norm_w.astype(jnp.float32) * jax.nn.sigmoid(o_in.astype(jnp.float32))


def _ssm_combine(e1, e2):
    a1, b1 = e1
    a2, b2 = e2
    return a2 * a1, a2 * b1 + b2


def s5_direction(ug, a_re, a_im, log_dt, b_re, b_im, c_re, c_im, reverse):
    f32 = jnp.float32
    lam = lax.complex(a_re.astype(f32), a_im.astype(f32))
    dt = jnp.exp(log_dt.astype(f32))[:, None]
    lam_bar = jnp.exp(lam * dt)
    b_mat = lax.complex(b_re.astype(f32), b_im.astype(f32))
    b_bar = ((lam_bar - 1.0) / lam)[..., None] * b_mat
    bu = jnp.einsum('gpc,blgc->blgp', b_bar, ug.astype(jnp.complex64))
    a = jnp.broadcast_to(lam_bar, bu.shape)
    _, xs = lax.associative_scan(_ssm_combine, (a, bu), axis=1, reverse=reverse)
    c_mat = lax.complex(c_re.astype(f32), c_im.astype(f32))
    return jnp.real(jnp.einsum('gcp,blgp->blgc', c_mat, xs))


def s5_group(u, a_re, a_im, log_dt, b_re, b_im, c_re, c_im, d_skip, w_glu):
    Bb, L, _ = u.shape
    uf = u.astype(jnp.float32)
    ug = uf.reshape(Bb, L, S5_GROUPS, S5_GROUP_CH)
    y_f = s5_direction(ug, a_re[0], a_im[0], log_dt[0], b_re[0], b_im[0], c_re[0], c_im[0], False)
    y_b = s5_direction(ug, a_re[1], a_im[1], log_dt[1], b_re[1], b_im[1], c_re[1], c_im[1], True)
    y = (y_f + y_b).reshape(Bb, L, D_S5) + d_skip.astype(jnp.float32) * uf
    ab = jax.nn.gelu(y) @ w_glu.astype(jnp.float32)
    return ab[..., :D_S5] * jax.nn.sigmoid(ab[..., D_S5:])


def setup_inputs(seed: int = 0) -> dict:
    key = jax.random.key(seed)
    ks = iter(jax.random.split(key, 40))
    f32 = jnp.float32
    nrm = lambda shape, scale: jax.random.normal(next(ks), shape, f32) * scale
    gain = lambda shape: 1.0 + nrm(shape, 0.01)
    H, G, P, GC = MLSTM_HEADS, S5_GROUPS, S5_STATE, S5_GROUP_CH
    fbias = jnp.broadcast_to(jnp.linspace(3.0, 6.0, H, dtype=f32), (DEPTH, 2, H)) + nrm((DEPTH, 2, H), 0.1)
    a_im0 = jnp.broadcast_to(math.pi * jnp.arange(P, dtype=f32), (DEPTH, 2, G, P))
    return {
        "x_prompt": nrm((BATCH, SEQ, D_MODEL), 1.0),
        "x_sample": nrm((DEC_BATCH, DEC_SEQ, D_MODEL), 1.0),
        "norm_ffn1": gain((DEPTH, D_MODEL)),
        "ffn1_w_gate": nrm((DEPTH, D_MODEL, D_FF), D_MODEL ** -0.5),
        "ffn1_w_up": nrm((DEPTH, D_MODEL, D_FF), D_MODEL ** -0.5),
        "ffn1_w_down": nrm((DEPTH, D_FF, D_MODEL), D_FF ** -0.5),
        "norm_mix": gain((DEPTH, D_MODEL)),
        "w_in": nrm((DEPTH, D_MODEL, D_IN), D_MODEL ** -0.5),
        "conv_w": nrm((DEPTH, CONV_WIDTH, 2 * D_MLSTM), CONV_WIDTH ** -0.5),
        "conv_b": nrm((DEPTH, 2 * D_MLSTM), 0.01),
        "b_igate": nrm((DEPTH, 2, H), 0.1),
        "b_fgate": fbias,
        "mlstm_norm_w": gain((DEPTH, D_MLSTM)),
        "s5_a_re": -0.5 + nrm((DEPTH, 2, G, P), 0.01),
        "s5_a_im": a_im0 + nrm((DEPTH, 2, G, P), 0.01),
        "s5_log_dt": jax.random.uniform(next(ks), (DEPTH, 2, G), f32, math.log(1e-3), math.log(1e-1)),
        "s5_b_re": nrm((DEPTH, 2, G, P, GC), (2 * GC) ** -0.5),
        "s5_b_im": nrm((DEPTH, 2, G, P, GC), (2 * GC) ** -0.5),
        "s5_c_re": nrm((DEPTH, 2, G, GC, P), (2 * P) ** -0.5),
        "s5_c_im": nrm((DEPTH, 2, G, GC, P), (2 * P) ** -0.5),
        "s5_d": nrm((DEPTH, D_S5), 1.0),
        "s5_w_glu": nrm((DEPTH, D_S5, 2 * D_S5), D_S5 ** -0.5),
        "w_out": nrm((DEPTH, D_MIX, D_MODEL), D_MIX ** -0.5),
        "norm_ffn2": gain((DEPTH, D_MODEL)),
        "ffn2_w_gate": nrm((DEPTH, D_MODEL, D_FF), D_MODEL ** -0.5),
        "ffn2_w_up": nrm((DEPTH, D_MODEL, D_FF), D_MODEL ** -0.5),
        "ffn2_w_down": nrm((DEPTH, D_FF, D_MODEL), D_FF ** -0.5),
        "norm_final": gain((D_MODEL,)),
    }


def reference(x_prompt, x_sample, norm_ffn1, ffn1_w_gate, ffn1_w_up, ffn1_w_down, norm_mix, w_in, conv_w, conv_b,
              b_igate, b_fgate, mlstm_norm_w, s5_a_re, s5_a_im, s5_log_dt, s5_b_re, s5_b_im, s5_c_re, s5_c_im,
              s5_d, s5_w_glu, w_out, norm_ffn2, ffn2_w_gate, ffn2_w_up, ffn2_w_down, norm_final):

    def encode(x):
        for l in range(DEPTH):
            x = x + 0.5 * swiglu_ffn(rmsnorm(x, norm_ffn1[l]), ffn1_w_gate[l], ffn1_w_up[l], ffn1_w_down[l])
            z = rmsnorm(x, norm_mix[l]) @ w_in[l]
            q_in = z[..., 0:D_MLSTM]
            k_in = z[..., D_MLSTM:2 * D_MLSTM]
            v_in = z[..., 2 * D_MLSTM:3 * D_MLSTM]
            o_in = z[..., 3 * D_MLSTM:4 * D_MLSTM]
            gates = z[..., 4 * D_MLSTM:4 * D_MLSTM + N_GATES]
            u = z[..., 4 * D_MLSTM + N_GATES:]
            h_m = mlstm_group(q_in, k_in, v_in, o_in, gates, conv_w[l], conv_b[l], b_igate[l], b_fgate[l],
                              mlstm_norm_w[l])
            h_s = s5_group(u, s5_a_re[l], s5_a_im[l], s5_log_dt[l], s5_b_re[l], s5_b_im[l], s5_c_re[l], s5_c_im[l],
                           s5_d[l], s5_w_glu[l])
            mix = jnp.concatenate([h_m, h_s], axis=-1).astype(x.dtype) @ w_out[l]
            x = x + mix
            x = x + 0.5 * swiglu_ffn(rmsnorm(x, norm_ffn2[l]), ffn2_w_gate[l], ffn2_w_up[l], ffn2_w_down[l])
        return rmsnorm(x, norm_final)

    y_prompt = encode(x_prompt)
    y_sample = encode(x_sample)
    return (y_prompt, y_sample)
```

```python
import functools
import math

import jax
import jax.numpy as jnp
from jax import lax
from jax.experimental import pallas as pl
from jax.experimental.pallas import tpu as pltpu

F32 = jnp.float32
BF16 = jnp.bfloat16

D_MODEL = 2048
D_MLSTM = 1024
D_S5 = 1024
N_HEADS = 8
HEAD_DIM = 128
CHUNK = 128
N_GATES = 32
GATE_PAD = 128
S5_GROUPS = 64
S5_GC = 16
S5_P = 64
S5_BLK = 16
S5_PAIRS = S5_GROUPS // 2
D_FF = 5632
EPS = 1e-6
M_INIT = -1e30

VMEM_LIMIT_BYTES = 56 * 1024 * 1024

FFN_TM = 512
FFN_TF = 512
PROJ_TM = 256
PROJ_HALO = 8
MIX_TM = 256


def _sigmoid(x):
    return 1.0 / (1.0 + jnp.exp(-x))


def _rmsnorm(x, w):
    return x * lax.rsqrt(jnp.mean(x * x, axis=-1, keepdims=True) + EPS) * w


def _ffn_body(x_ref, nw_ref, wg_ref, wu_ref, wd_ref, nf_ref, o_ref, xn_ref, *, final_norm):
    j = pl.program_id(1)

    @pl.when(j == 0)
    def _():
        x = x_ref[...]
        xn_ref[...] = _rmsnorm(x, nw_ref[...]).astype(BF16)
        o_ref[...] = x

    xn = xn_ref[...]
    g = jnp.dot(xn, wg_ref[...], preferred_element_type=F32)
    u = jnp.dot(xn, wu_ref[...], preferred_element_type=F32)
    h = (0.5 * g * _sigmoid(g)) * u
    o_ref[...] += jnp.dot(h.astype(BF16), wd_ref[...], preferred_element_type=F32)

    if final_norm:
        @pl.when(j == pl.num_programs(1) - 1)
        def _():
            o_ref[...] = _rmsnorm(o_ref[...], nf_ref[...])


def _ffn(x, norm_w, w_gate, w_up, w_down, norm_final, *, final_norm):
    L = x.shape[0]
    assert L % FFN_TM == 0 and D_FF % FFN_TF == 0
    return pl.pallas_call(
        functools.partial(_ffn_body, final_norm=final_norm),
        grid=(L // FFN_TM, D_FF // FFN_TF),
        in_specs=[
            pl.BlockSpec((FFN_TM, D_MODEL), lambda i, j: (i, 0)),
            pl.BlockSpec((1, D_MODEL), lambda i, j: (0, 0)),
            pl.BlockSpec((D_MODEL, FFN_TF), lambda i, j: (0, j)),
            pl.BlockSpec((D_MODEL, FFN_TF), lambda i, j: (0, j)),
            pl.BlockSpec((FFN_TF, D_MODEL), lambda i, j: (j, 0)),
            pl.BlockSpec((1, D_MODEL), lambda i, j: (0, 0)),
        ],
        out_specs=pl.BlockSpec((FFN_TM, D_MODEL), lambda i, j: (i, 0)),
        out_shape=jax.ShapeDtypeStruct((L, D_MODEL), F32),
        scratch_shapes=[pltpu.VMEM((FFN_TM, D_MODEL), BF16)],
        compiler_params=pltpu.CompilerParams(
            dimension_semantics=("parallel", "arbitrary"), vmem_limit_bytes=VMEM_LIMIT_BYTES),
        name="ffn_final" if final_norm else "ffn",
    )(x, norm_w, w_gate, w_up, w_down, norm_final)


QK_COLS = 2 * D_MLSTM
QK_CB = 512


def _in_proj_body(xp_ref, x_ref, xnx_ref, nw_ref, wqk_ref, wvou_ref, wgc_ref, wgr_ref,
                  cw_ref, cb_ref, qk_ref, v_ref, o_ref, u_ref, gc_ref, gr_ref):
    i = pl.program_id(0)
    tm = x_ref.shape[0]
    nw = nw_ref[...]
    xn = _rmsnorm(x_ref[...], nw)
    xn_prev = jnp.where(i == 0, 0.0, _rmsnorm(xp_ref[...], nw))
    xn_next = jnp.where(i == pl.num_programs(0) - 1, 0.0, _rmsnorm(xnx_ref[...], nw))
    xn_b = xn.astype(BF16)
    xe_b = jnp.concatenate([xn_prev, xn, xn_next], axis=0).astype(BF16)
    rows = tm + 2 * PROJ_HALO

    for c in range(QK_COLS // QK_CB):
        cs = slice(c * QK_CB, (c + 1) * QK_CB)
        z = jnp.dot(xe_b, wqk_ref[:, cs], preferred_element_type=F32)
        z_m1 = pltpu.roll(z, 1, 0)[PROJ_HALO:PROJ_HALO + tm]
        z_0 = z[PROJ_HALO:PROJ_HALO + tm]
        z_p1 = pltpu.roll(z, rows - 1, 0)[PROJ_HALO:PROJ_HALO + tm]
        y = z_m1 * cw_ref[0:1, cs] + z_0 * cw_ref[1:2, cs] + z_p1 * cw_ref[2:3, cs] + cb_ref[:, cs]
        y = y * _sigmoid(y)
        if c * QK_CB < D_MLSTM:
            y = y * (HEAD_DIM ** -0.5)
        qk_ref[:, cs] = y.astype(qk_ref.dtype)

    v_ref[...] = jnp.dot(xn_b, wvou_ref[:, 0:D_MLSTM], preferred_element_type=F32).astype(v_ref.dtype)
    o_ref[...] = jnp.dot(xn_b, wvou_ref[:, D_MLSTM:2 * D_MLSTM], preferred_element_type=F32)
    u_ref[...] = jnp.dot(xn_b, wvou_ref[:, 2 * D_MLSTM:], preferred_element_type=F32).astype(u_ref.dtype)
    gc_ref[...] = jnp.dot(xn_b, wgc_ref[...], preferred_element_type=F32)
    gr_ref[...] = lax.dot_general(wgr_ref[...], xn_b, (((1,), (1,)), ((), ())), preferred_element_type=F32)


def _in_proj(x, norm_w, w_qk, w_vou, w_gc, w_gr, conv_w, conv_b):
    L = x.shape[0]
    tm = PROJ_TM
    assert L % tm == 0
    hb = tm // PROJ_HALO
    nblk8 = L // PROJ_HALO
    whole = pl.BlockSpec(memory_space=pltpu.VMEM)
    return pl.pallas_call(
        _in_proj_body,
        grid=(L // tm,),
        in_specs=[
            pl.BlockSpec((PROJ_HALO, D_MODEL), lambda i: (jnp.maximum(i * hb - 1, 0), 0)),
            pl.BlockSpec((tm, D_MODEL), lambda i: (i, 0)),
            pl.BlockSpec((PROJ_HALO, D_MODEL), lambda i: (jnp.minimum((i + 1) * hb, nblk8 - 1), 0)),
            whole, whole, whole, whole, whole, whole, whole,
        ],
        out_specs=[
            pl.BlockSpec((tm, QK_COLS), lambda i: (i, 0)),
            pl.BlockSpec((tm, D_MLSTM), lambda i: (i, 0)),
            pl.BlockSpec((tm, D_MLSTM), lambda i: (i, 0)),
            pl.BlockSpec((tm, D_S5), lambda i: (i, 0)),
            pl.BlockSpec((tm, GATE_PAD), lambda i: (i, 0)),
            pl.BlockSpec((N_GATES, tm), lambda i: (0, i)),
        ],
        out_shape=[
            jax.ShapeDtypeStruct((L, QK_COLS), BF16),
            jax.ShapeDtypeStruct((L, D_MLSTM), BF16),
            jax.ShapeDtypeStruct((L, D_MLSTM), F32),
            jax.ShapeDtypeStruct((L, D_S5), BF16),
            jax.ShapeDtypeStruct((L, GATE_PAD), F32),
            jax.ShapeDtypeStruct((N_GATES, L), F32),
        ],
        compiler_params=pltpu.CompilerParams(
            dimension_semantics=("parallel",), vmem_limit_bytes=VMEM_LIMIT_BYTES),
        name="in_proj",
    )(x, x, x, norm_w, w_qk, w_vou, w_gc, w_gr, conv_w, conv_b)


def _bf16_split3(x):
    hi = x.astype(BF16)
    r1 = x - hi.astype(F32)
    mid = r1.astype(BF16)
    lo = (r1 - mid.astype(F32)).astype(BF16)
    return hi, mid, lo


def _mlstm_direction(qk_ref, v_ref, gc_ref, gr_ref, bias_c_ref, bias_r_ref, h_ref, ct_ref, m_ref, d):
    T = CHUNK
    rr = lax.broadcasted_iota(jnp.int32, (T, T), 0)
    cc = lax.broadcasted_iota(jnp.int32, (T, T), 1)
    mask = (cc <= rr) if d == 0 else (cc >= rr)
    tri_c = mask.astype(BF16)
    tri_r = ((rr <= cc) if d == 0 else (rr >= cc)).astype(BF16)

    gcol = gc_ref[...] + bias_c_ref[...]
    grow = gr_ref[...] + bias_r_ref[...]
    lf_col = jnp.minimum(gcol, 0.0) - jnp.log1p(jnp.exp(-jnp.abs(gcol)))
    lf_row = jnp.minimum(grow, 0.0) - jnp.log1p(jnp.exp(-jnp.abs(grow)))
    cum_col = sum(jnp.dot(tri_c, part, preferred_element_type=F32) for part in _bf16_split3(lf_col))
    cum_row = sum(jnp.dot(part, tri_r, preferred_element_type=F32) for part in _bf16_split3(lf_row))
    tot_idx = T - 1 if d == 0 else 0
    ones = jnp.ones((T, HEAD_DIM), BF16)

    for h in range(N_HEADS):
        ci = d * N_HEADS + h
        cf = 2 * N_HEADS + d * N_HEADS + h
        hs = slice(h * HEAD_DIM, (h + 1) * HEAD_DIM)
        q = qk_ref[:, hs]
        k = qk_ref[:, D_MLSTM + h * HEAD_DIM:D_MLSTM + (h + 1) * HEAD_DIM]
        v = v_ref[:, hs]
        vaug = jnp.concatenate([v, ones], axis=1)
        b_col = cum_col[:, cf:cf + 1]
        i_col = gcol[:, ci:ci + 1]
        b_row = cum_row[cf:cf + 1, :]
        i_row = grow[ci:ci + 1, :]
        b_tot = b_row[:, tot_idx:tot_idx + 1]
        m_prev = m_ref[ci]
        ct_prev = ct_ref[ci]

        dmat = jnp.where(mask, b_col + (i_row - b_row), -jnp.inf)
        inter = b_col + m_prev
        m_t = jnp.maximum(jnp.max(dmat, axis=-1, keepdims=True), inter)
        w = jnp.exp(dmat - m_t)
        s_inter = jnp.exp(inter - m_t)
        s = lax.dot_general(q, k, (((1,), (1,)), ((), ())), preferred_element_type=F32) * w
        sv = jnp.dot(s.astype(BF16), vaug, preferred_element_type=F32)
        qc = jnp.dot(q, ct_prev.astype(BF16), preferred_element_type=F32)
        comb = sv + s_inter * qc
        num = comb[:, :HEAD_DIM]
        den = comb[:, HEAD_DIM:HEAD_DIM + 1]
        h_ref[:, hs] = num / jnp.maximum(jnp.abs(den), jnp.exp(-m_t))

        wk_row = b_tot - b_row + i_row
        m_new = jnp.maximum(b_tot + m_prev, jnp.max(wk_row, axis=-1, keepdims=True))
        sc = jnp.exp(b_tot + m_prev - m_new)
        wkk_col = jnp.exp(b_tot - b_col + i_col - m_new)
        kw = (k.astype(F32) * wkk_col).astype(BF16)
        upd = lax.dot_general(kw, vaug, (((0,), (0,)), ((), ())), preferred_element_type=F32)
        ct_ref[ci] = sc * ct_prev + upd
        m_ref[ci] = m_new


def _mlstm_body(qkf_ref, vf_ref, gcf_ref, grf_ref, qkb_ref, vb_ref, gcb_ref, grb_ref, bias_c_ref, bias_r_ref,
                hf_ref, hb_ref, ct_ref, m_ref):
    @pl.when(pl.program_id(0) == 0)
    def _():
        ct_ref[...] = jnp.zeros_like(ct_ref)
        m_ref[...] = jnp.full_like(m_ref, M_INIT)

    _mlstm_direction(qkf_ref, vf_ref, gcf_ref, grf_ref, bias_c_ref, bias_r_ref, hf_ref, ct_ref, m_ref, 0)
    _mlstm_direction(qkb_ref, vb_ref, gcb_ref, grb_ref, bias_c_ref, bias_r_ref, hb_ref, ct_ref, m_ref, 1)


def _mlstm(qk, v, gcol, grow, bias_c, bias_r):
    L = qk.shape[0]
    T = CHUNK
    assert L % T == 0
    nc = L // T
    fwd = lambda c: (c, 0)
    bwd = lambda c: (nc - 1 - c, 0)
    fwd_r = lambda c: (0, c)
    bwd_r = lambda c: (0, nc - 1 - c)
    return pl.pallas_call(
        _mlstm_body,
        grid=(nc,),
        in_specs=[
            pl.BlockSpec((T, QK_COLS), fwd), pl.BlockSpec((T, D_MLSTM), fwd),
            pl.BlockSpec((T, GATE_PAD), fwd), pl.BlockSpec((N_GATES, T), fwd_r),
            pl.BlockSpec((T, QK_COLS), bwd), pl.BlockSpec((T, D_MLSTM), bwd),
            pl.BlockSpec((T, GATE_PAD), bwd), pl.BlockSpec((N_GATES, T), bwd_r),
            pl.BlockSpec((1, GATE_PAD), lambda c: (0, 0)),
            pl.BlockSpec((N_GATES, 1), lambda c: (0, 0)),
        ],
        out_specs=[pl.BlockSpec((T, D_MLSTM), fwd), pl.BlockSpec((T, D_MLSTM), bwd)],
        out_shape=[jax.ShapeDtypeStruct((L, D_MLSTM), F32), jax.ShapeDtypeStruct((L, D_MLSTM), F32)],
        scratch_shapes=[
            pltpu.VMEM((2 * N_HEADS, HEAD_DIM, 2 * HEAD_DIM), F32),
            pltpu.VMEM((2 * N_HEADS, 1, 1), F32),
        ],
        compiler_params=pltpu.CompilerParams(
            dimension_semantics=("arbitrary",), vmem_limit_bytes=VMEM_LIMIT_BYTES),
        name="mlstm",
    )(qk, v, gcol, grow, qk, v, gcol, grow, bias_c, bias_r)


S5_ROW = 2 * S5_BLK * S5_GC
S5_ST = 2 * S5_P
S5_TILE = 8


def _s5_body(u_ref, m_ref, win_ref, wout_ref, a_ref, y_ref, sin_ref, st_ref):
    nb = u_ref.shape[1]
    u = u_ref[0]
    y_ref[0] = jnp.dot(u, m_ref[0], preferred_element_type=F32)
    sin_ref[...] = jnp.dot(u, win_ref[0], preferred_element_type=F32)
    R = S5_TILE
    assert nb % R == 0
    ntile = nb // R
    a = a_ref[0]
    row = lax.broadcasted_iota(jnp.int32, (R, S5_ST), 0)
    zero = jnp.zeros((1, S5_ST), F32)

    def cmul(x, y):
        return x[0] * y[0] - x[1] * y[1], x[0] * y[1] + x[1] * y[0]

    def bcast(x):
        return tuple(jnp.broadcast_to(t, (R, S5_ST)) for t in x)

    def tables(ar, ai, backward):
        pw = {1: (ar, ai)}
        for e in range(2, R + 1):
            pw[e] = cmul(pw[e // 2], pw[e - e // 2])
        steps = []
        for sh in (1, 2, 4):
            keep = (row < R - sh) if backward else (row >= sh)
            steps.append(tuple(jnp.where(keep, t, 0.0) for t in bcast(pw[sh])))
        order = range(R, 0, -1) if backward else range(1, R + 1)
        carry_pw = tuple(jnp.concatenate([pw[e][j] for e in order], axis=0) for j in (0, 1))
        return steps, carry_pw

    steps_f, cpw_f = tables(a[0:1], a[1:2], False)
    steps_b, cpw_b = tables(a[2:3], a[3:4], True)

    def scan_tile(x, carry, steps, cpw, backward):
        for sh, am in zip((1, 2, 4), steps):
            rs = (R - sh) if backward else sh
            x = tuple(p + q for p, q in zip(x, cmul(am, (pltpu.roll(x[0], rs, 0), pltpu.roll(x[1], rs, 0)))))
        cb = bcast(carry)
        x = tuple(p + q for p, q in zip(x, cmul(cpw, cb)))
        edge, rs = (R - 1, R - 1) if backward else (0, 1)
        enter = tuple(jnp.where(row == edge, c, pltpu.roll(t, rs, 0)) for t, c in zip(x, cb))
        last = 0 if backward else R - 1
        return enter, (x[0][last:last + 1], x[1][last:last + 1])

    def step(i, carry):
        cf, cb = carry
        rf = pl.multiple_of(i * R, R)
        rb = pl.multiple_of((ntile - 1 - i) * R, R)
        xf = (sin_ref[pl.ds(rf, R), 0:S5_ST], sin_ref[pl.ds(rf, R), S5_ST:2 * S5_ST])
        xb = (sin_ref[pl.ds(rb, R), 2 * S5_ST:3 * S5_ST], sin_ref[pl.ds(rb, R), 3 * S5_ST:4 * S5_ST])
        ef, cf = scan_tile(xf, cf, steps_f, cpw_f, False)
        eb, cb = scan_tile(xb, cb, steps_b, cpw_b, True)
        st_ref[pl.ds(rf, R), 0:S5_ST] = ef[0]
        st_ref[pl.ds(rf, R), S5_ST:2 * S5_ST] = ef[1]
        st_ref[pl.ds(rb, R), 2 * S5_ST:3 * S5_ST] = eb[0]
        st_ref[pl.ds(rb, R), 3 * S5_ST:4 * S5_ST] = eb[1]
        return cf, cb

    lax.fori_loop(0, ntile, step, ((zero, zero), (zero, zero)))
    y_ref[0] += jnp.dot(st_ref[...].astype(BF16), wout_ref[0], preferred_element_type=F32)


def _s5(u2, m2, win2, wout2, a2):
    npairs, nb, _ = u2.shape
    wspec = pl.BlockSpec((1, S5_ROW, S5_ROW), lambda p: (p, 0, 0))
    return pl.pallas_call(
        _s5_body,
        grid=(npairs,),
        in_specs=[
            pl.BlockSpec((1, nb, S5_ROW), lambda p: (p, 0, 0)),
            wspec, wspec, wspec,
            pl.BlockSpec((1, 4, S5_ST), lambda p: (p, 0, 0)),
        ],
        out_specs=pl.BlockSpec((1, nb, S5_ROW), lambda p: (p, 0, 0)),
        out_shape=jax.ShapeDtypeStruct((npairs, nb, S5_ROW), F32),
        scratch_shapes=[pltpu.VMEM((nb, 4 * S5_ST), F32), pltpu.VMEM((nb, 4 * S5_ST), F32)],
        compiler_params=pltpu.CompilerParams(
            dimension_semantics=("parallel",), vmem_limit_bytes=VMEM_LIMIT_BYTES),
        name="s5",
    )(u2, m2, win2, wout2, a2)


def _s5_weights(a_re, a_im, log_dt, b_re, b_im, c_re, c_im, d_skip):
    nt = S5_BLK
    lam = lax.complex(a_re, a_im)
    dt = jnp.exp(log_dt)[..., None]
    lam_bar = jnp.exp(lam * dt)
    b_bar = ((lam_bar - 1.0) / lam)[..., None] * lax.complex(b_re, b_im)
    c_mat = lax.complex(c_re, c_im)
    taus = jnp.arange(nt + 1, dtype=F32)
    pw = jnp.exp((lam * dt)[..., None] * taus)
    kern = jnp.real(jnp.einsum('zgop,zgpt,zgpi->zgtoi', c_mat, pw[..., :nt], b_bar, precision='highest'))
    t_idx = jnp.arange(nt)
    diff = t_idx[None, :] - t_idx[:, None]
    kf = kern[0][:, jnp.clip(diff, 0, nt - 1)]
    kb = kern[1][:, jnp.clip(-diff, 0, nt - 1)]
    mf = jnp.where((diff >= 0)[None, :, :, None, None], kf, 0.0)
    mb = jnp.where((diff <= 0)[None, :, :, None, None], kb, 0.0)
    eye = (diff == 0)[None, :, :, None, None] * jnp.eye(S5_GC, dtype=F32)[None, None, None] \
        * d_skip.reshape(S5_GROUPS, 1, 1, 1, S5_GC)
    m = (mf + mb + eye).transpose(0, 1, 4, 2, 3).reshape(S5_GROUPS, nt * S5_GC, nt * S5_GC)

    pw_in_f = pw[0][..., ::-1][..., 1:]
    pw_in_b = pw[1][..., :nt]
    win_f = jnp.einsum('gps,gpi->gsip', pw_in_f, b_bar[0]).reshape(S5_GROUPS, nt * S5_GC, S5_P)
    win_b = jnp.einsum('gps,gpi->gsip', pw_in_b, b_bar[1]).reshape(S5_GROUPS, nt * S5_GC, S5_P)
    pw_out_f = pw[0][..., 1:]
    pw_out_b = pw[1][..., ::-1][..., :nt]
    wo_f = jnp.einsum('gop,gpt->gpto', c_mat[0], pw_out_f).reshape(S5_GROUPS, S5_P, nt * S5_GC)
    wo_b = jnp.einsum('gop,gpt->gpto', c_mat[1], pw_out_b).reshape(S5_GROUPS, S5_P, nt * S5_GC)
    a_blk = pw[..., nt]

    def pair_cols(x):
        return x.reshape(S5_PAIRS, 2, x.shape[1], x.shape[2])

    def pair_diag(x):
        xp = x.reshape(S5_PAIRS, 2, x.shape[1], x.shape[2])
        z = jnp.zeros_like(xp[:, 0])
        return jnp.concatenate([jnp.concatenate([xp[:, 0], z], axis=2),
                                jnp.concatenate([z, xp[:, 1]], axis=2)], axis=1)

    m2 = pair_diag(m)
    win2 = jnp.concatenate([pair_diag(jnp.real(win_f)), pair_diag(jnp.imag(win_f)),
                            pair_diag(jnp.real(win_b)), pair_diag(jnp.imag(win_b))], axis=2)
    wout2 = jnp.concatenate([pair_diag(jnp.real(wo_f)), pair_diag(-jnp.imag(wo_f)),
                             pair_diag(jnp.real(wo_b)), pair_diag(-jnp.imag(wo_b))], axis=1)
    a2 = jnp.stack([jnp.real(a_blk[0]), jnp.imag(a_blk[0]), jnp.real(a_blk[1]), jnp.imag(a_blk[1])],
                   axis=1).reshape(S5_PAIRS, 2, 4, S5_P).transpose(0, 2, 1, 3).reshape(S5_PAIRS, 4, S5_ST)
    return m2.astype(BF16), win2.astype(BF16), wout2.astype(BF16), a2


def _mix_body(x_ref, hf_ref, hb_ref, o_ref, y_ref, nw_ref, wglu_ref, wout_ref, out_ref):
    h = hf_ref[...] + hb_ref[...]
    parts = []
    for hd in range(N_HEADS):
        hh = h[:, hd * HEAD_DIM:(hd + 1) * HEAD_DIM]
        mu = jnp.mean(hh, axis=-1, keepdims=True)
        var = jnp.mean(jnp.square(hh - mu), axis=-1, keepdims=True)
        parts.append((hh - mu) * lax.rsqrt(var + EPS))
    hn = jnp.concatenate(parts, axis=1)
    h_m = hn * nw_ref[...] * _sigmoid(o_ref[...])
    y = y_ref[...]
    gelu = 0.5 * y * (1.0 + jnp.tanh(math.sqrt(2.0 / math.pi) * (y + 0.044715 * (y * y * y))))
    ab = jnp.dot(gelu.astype(BF16), wglu_ref[...], preferred_element_type=F32)
    h_s = ab[:, :D_S5] * _sigmoid(ab[:, D_S5:])
    mix = jnp.dot(h_m.astype(BF16), wout_ref[0:D_MLSTM, :], preferred_element_type=F32)
    mix += jnp.dot(h_s.astype(BF16), wout_ref[D_MLSTM:, :], preferred_element_type=F32)
    out_ref[...] = x_ref[...] + mix


def _mix(x, hf, hb, o_in, y, norm_w, w_glu, w_out):
    L = x.shape[0]
    tm = MIX_TM
    assert L % tm == 0
    whole = pl.BlockSpec(memory_space=pltpu.VMEM)
    row = lambda n: pl.BlockSpec((tm, n), lambda i: (i, 0))
    return pl.pallas_call(
        _mix_body,
        grid=(L // tm,),
        in_specs=[row(D_MODEL), row(D_MLSTM), row(D_MLSTM), row(D_MLSTM), row(D_S5), whole, whole, whole],
        out_specs=row(D_MODEL),
        out_shape=jax.ShapeDtypeStruct((L, D_MODEL), F32),
        compiler_params=pltpu.CompilerParams(
            dimension_semantics=("parallel",), vmem_limit_bytes=VMEM_LIMIT_BYTES),
        name="mix",
    )(x, hf, hb, o_in, y, norm_w, w_glu, w_out)


def _encode(x, p):
    L = x.shape[0]
    x = _ffn(x, p["norm_ffn1"], p["ffn1_w_gate"], p["ffn1_w_up"], p["ffn1_w_down"], p["norm_final"],
             final_norm=False)
    qk, v, o_in, u, gcol, grow = _in_proj(x, p["norm_mix"], p["w_qk"], p["w_vou"], p["w_gc"], p["w_gr"],
                                          p["conv_w"], p["conv_b"])
    hf, hb = _mlstm(qk, v, gcol, grow, p["gate_bias_c"], p["gate_bias_r"])
    nb = L // S5_BLK
    u2 = u.reshape(nb, S5_BLK, S5_PAIRS, 2, S5_GC).transpose(2, 0, 3, 1, 4).reshape(S5_PAIRS, nb, S5_ROW)
    y2 = _s5(u2, p["s5_m"], p["s5_win"], p["s5_wout"], p["s5_a"])
    y = y2.reshape(S5_PAIRS, nb, 2, S5_BLK, S5_GC).transpose(1, 3, 0, 2, 4).reshape(L, D_S5)
    x = _mix(x, hf, hb, o_in, y, p["mlstm_norm_w"], p["s5_w_glu"], p["w_out"])
    return _ffn(x, p["norm_ffn2"], p["ffn2_w_gate"], p["ffn2_w_up"], p["ffn2_w_down"], p["norm_final"],
                final_norm=True)


def _prepare(norm_ffn1, ffn1_w_gate, ffn1_w_up, ffn1_w_down, norm_mix, w_in, conv_w, conv_b, b_igate, b_fgate,
             mlstm_norm_w, s5_a_re, s5_a_im, s5_log_dt, s5_b_re, s5_b_im, s5_c_re, s5_c_im, s5_d, s5_w_glu,
             w_out, norm_ffn2, ffn2_w_gate, ffn2_w_up, ffn2_w_down, norm_final):
    l = 0
    w = w_in[l]
    g0 = 4 * D_MLSTM
    w_g = w[:, g0:g0 + N_GATES]
    gate_bias = jnp.concatenate([b_igate[l].reshape(-1), b_fgate[l].reshape(-1)])
    m2, win2, wout2, a2 = _s5_weights(s5_a_re[l], s5_a_im[l], s5_log_dt[l], s5_b_re[l], s5_b_im[l],
                                      s5_c_re[l], s5_c_im[l], s5_d[l])
    row = lambda a: a.reshape(1, -1).astype(F32)
    return {
        "norm_ffn1": row(norm_ffn1[l]), "norm_ffn2": row(norm_ffn2[l]), "norm_final": row(norm_final),
        "ffn1_w_gate": ffn1_w_gate[l].astype(BF16), "ffn1_w_up": ffn1_w_up[l].astype(BF16),
        "ffn1_w_down": ffn1_w_down[l].astype(BF16),
        "ffn2_w_gate": ffn2_w_gate[l].astype(BF16), "ffn2_w_up": ffn2_w_up[l].astype(BF16),
        "ffn2_w_down": ffn2_w_down[l].astype(BF16),
        "norm_mix": row(norm_mix[l]),
        "w_qk": w[:, :QK_COLS].astype(BF16),
        "w_vou": jnp.concatenate([w[:, QK_COLS:g0], w[:, g0 + N_GATES:]], axis=1).astype(BF16),
        "w_gc": jnp.pad(w_g, ((0, 0), (0, GATE_PAD - N_GATES))).astype(BF16),
        "w_gr": w_g.T.astype(BF16),
        "conv_w": conv_w[l].astype(F32), "conv_b": row(conv_b[l]),
        "gate_bias_c": jnp.pad(gate_bias, (0, GATE_PAD - N_GATES)).reshape(1, GATE_PAD),
        "gate_bias_r": gate_bias.reshape(N_GATES, 1),
        "mlstm_norm_w": row(mlstm_norm_w[l]),
        "s5_m": m2, "s5_win": win2, "s5_wout": wout2, "s5_a": a2,
        "s5_w_glu": s5_w_glu[l].astype(BF16), "w_out": w_out[l].astype(BF16),
    }


def kernel(x_prompt, x_sample, norm_ffn1, ffn1_w_gate, ffn1_w_up, ffn1_w_down, norm_mix, w_in, conv_w, conv_b, b_igate, b_fgate, mlstm_norm_w, s5_a_re, s5_a_im, s5_log_dt, s5_b_re, s5_b_im, s5_c_re, s5_c_im, s5_d, s5_w_glu, w_out, norm_ffn2, ffn2_w_gate, ffn2_w_up, ffn2_w_down, norm_final):
    assert norm_ffn1.shape[0] == 1 and x_prompt.shape[0] == 1 and x_sample.shape[0] == 1
    p = _prepare(norm_ffn1, ffn1_w_gate, ffn1_w_up, ffn1_w_down, norm_mix, w_in, conv_w, conv_b, b_igate,
                 b_fgate, mlstm_norm_w, s5_a_re, s5_a_im, s5_log_dt, s5_b_re, s5_b_im, s5_c_re, s5_c_im, s5_d,
                 s5_w_glu, w_out, norm_ffn2, ffn2_w_gate, ffn2_w_up, ffn2_w_down, norm_final)
    y_prompt = _encode(x_prompt[0], p)[None]
    y_sample = _encode(x_sample[0], p)[None]
    return (y_prompt, y_sample)
```

```python
import functools
import math

import jax
import jax.numpy as jnp
from jax import lax
from jax.experimental import pallas as pl
from jax.experimental.pallas import tpu as pltpu

F32 = jnp.float32
BF16 = jnp.bfloat16

D_MODEL = 2048
D_MLSTM = 1024
D_S5 = 1024
N_HEADS = 8
HEAD_DIM = 128
CHUNK = 128
N_GATES = 32
GATE_PAD = 128
S5_GROUPS = 64
S5_GC = 16
S5_P = 64
S5_BLK = 16
S5_PAIRS = S5_GROUPS // 2
D_FF = 5632
EPS = 1e-6
M_INIT = -1e30

VMEM_LIMIT_BYTES = 56 * 1024 * 1024

FFN_TM = 512
FFN_TF = 512
PROJ_TM = 256
PROJ_HALO = 8
MIX_TM = 256


def _sigmoid(x):
    return 1.0 / (1.0 + jnp.exp(-x))


def _rmsnorm(x, w):
    return x * lax.rsqrt(jnp.mean(x * x, axis=-1, keepdims=True) + EPS) * w


def _ffn_body(x_ref, nw_ref, wg_ref, wu_ref, wd_ref, nf_ref, o_ref, xn_ref, *, final_norm):
    j = pl.program_id(1)

    @pl.when(j == 0)
    def _():
        x = x_ref[...]
        xn_ref[...] = _rmsnorm(x, nw_ref[...]).astype(BF16)
        o_ref[...] = x

    xn = xn_ref[...]
    g = jnp.dot(xn, wg_ref[...], preferred_element_type=F32)
    u = jnp.dot(xn, wu_ref[...], preferred_element_type=F32)
    h = (0.5 * g * _sigmoid(g)) * u
    o_ref[...] += jnp.dot(h.astype(BF16), wd_ref[...], preferred_element_type=F32)

    if final_norm:
        @pl.when(j == pl.num_programs(1) - 1)
        def _():
            o_ref[...] = _rmsnorm(o_ref[...], nf_ref[...])


def _ffn(x, norm_w, w_gate, w_up, w_down, norm_final, *, final_norm):
    L = x.shape[0]
    assert L % FFN_TM == 0 and D_FF % FFN_TF == 0
    return pl.pallas_call(
        functools.partial(_ffn_body, final_norm=final_norm),
        grid=(L // FFN_TM, D_FF // FFN_TF),
        in_specs=[
            pl.BlockSpec((FFN_TM, D_MODEL), lambda i, j: (i, 0)),
            pl.BlockSpec((1, D_MODEL), lambda i, j: (0, 0)),
            pl.BlockSpec((D_MODEL, FFN_TF), lambda i, j: (0, j)),
            pl.BlockSpec((D_MODEL, FFN_TF), lambda i, j: (0, j)),
            pl.BlockSpec((FFN_TF, D_MODEL), lambda i, j: (j, 0)),
            pl.BlockSpec((1, D_MODEL), lambda i, j: (0, 0)),
        ],
        out_specs=pl.BlockSpec((FFN_TM, D_MODEL), lambda i, j: (i, 0)),
        out_shape=jax.ShapeDtypeStruct((L, D_MODEL), F32),
        scratch_shapes=[pltpu.VMEM((FFN_TM, D_MODEL), BF16)],
        compiler_params=pltpu.CompilerParams(
            dimension_semantics=("parallel", "arbitrary"), vmem_limit_bytes=VMEM_LIMIT_BYTES),
        name="ffn_final" if final_norm else "ffn",
    )(x, norm_w, w_gate, w_up, w_down, norm_final)


QK_COLS = 2 * D_MLSTM
QK_CB = 512


def _in_proj_body(xp_ref, x_ref, xnx_ref, nw_ref, wqk_ref, wvou_ref, wgc_ref, wgr_ref,
                  cw_ref, cb_ref, qk_ref, v_ref, o_ref, u_ref, gc_ref, gr_ref):
    i = pl.program_id(0)
    tm = x_ref.shape[0]
    nw = nw_ref[...]
    xn = _rmsnorm(x_ref[...], nw)
    xn_prev = jnp.where(i == 0, 0.0, _rmsnorm(xp_ref[...], nw))
    xn_next = jnp.where(i == pl.num_programs(0) - 1, 0.0, _rmsnorm(xnx_ref[...], nw))
    xn_b = xn.astype(BF16)
    xe_b = jnp.concatenate([xn_prev, xn, xn_next], axis=0).astype(BF16)
    rows = tm + 2 * PROJ_HALO

    for c in range(QK_COLS // QK_CB):
        cs = slice(c * QK_CB, (c + 1) * QK_CB)
        z = jnp.dot(xe_b, wqk_ref[:, cs], preferred_element_type=F32)
        z_m1 = pltpu.roll(z, 1, 0)[PROJ_HALO:PROJ_HALO + tm]
        z_0 = z[PROJ_HALO:PROJ_HALO + tm]
        z_p1 = pltpu.roll(z, rows - 1, 0)[PROJ_HALO:PROJ_HALO + tm]
        y = z_m1 * cw_ref[0:1, cs] + z_0 * cw_ref[1:2, cs] + z_p1 * cw_ref[2:3, cs] + cb_ref[:, cs]
        y = y * _sigmoid(y)
        if c * QK_CB < D_MLSTM:
            y = y * (HEAD_DIM ** -0.5)
        qk_ref[:, cs] = y.astype(qk_ref.dtype)

    v_ref[...] = jnp.dot(xn_b, wvou_ref[:, 0:D_MLSTM], preferred_element_type=F32).astype(v_ref.dtype)
    o_ref[...] = jnp.dot(xn_b, wvou_ref[:, D_MLSTM:2 * D_MLSTM], preferred_element_type=F32)
    u_ref[...] = jnp.dot(xn_b, wvou_ref[:, 2 * D_MLSTM:], preferred_element_type=F32).astype(u_ref.dtype)
    gc_ref[...] = jnp.dot(xn_b, wgc_ref[...], preferred_element_type=F32)
    gr_ref[...] = lax.dot_general(wgr_ref[...], xn_b, (((1,), (1,)), ((), ())), preferred_element_type=F32)


def _in_proj(x, norm_w, w_qk, w_vou, w_gc, w_gr, conv_w, conv_b):
    L = x.shape[0]
    tm = PROJ_TM
    assert L % tm == 0
    hb = tm // PROJ_HALO
    nblk8 = L // PROJ_HALO
    whole = pl.BlockSpec(memory_space=pltpu.VMEM)
    return pl.pallas_call(
        _in_proj_body,
        grid=(L // tm,),
        in_specs=[
            pl.BlockSpec((PROJ_HALO, D_MODEL), lambda i: (jnp.maximum(i * hb - 1, 0), 0)),
            pl.BlockSpec((tm, D_MODEL), lambda i: (i, 0)),
            pl.BlockSpec((PROJ_HALO, D_MODEL), lambda i: (jnp.minimum((i + 1) * hb, nblk8 - 1), 0)),
            whole, whole, whole, whole, whole, whole, whole,
        ],
        out_specs=[
            pl.BlockSpec((tm, QK_COLS), lambda i: (i, 0)),
            pl.BlockSpec((tm, D_MLSTM), lambda i: (i, 0)),
            pl.BlockSpec((tm, D_MLSTM), lambda i: (i, 0)),
            pl.BlockSpec((tm, D_S5), lambda i: (i, 0)),
            pl.BlockSpec((tm, GATE_PAD), lambda i: (i, 0)),
            pl.BlockSpec((N_GATES, tm), lambda i: (0, i)),
        ],
        out_shape=[
            jax.ShapeDtypeStruct((L, QK_COLS), BF16),
            jax.ShapeDtypeStruct((L, D_MLSTM), BF16),
            jax.ShapeDtypeStruct((L, D_MLSTM), F32),
            jax.ShapeDtypeStruct((L, D_S5), BF16),
            jax.ShapeDtypeStruct((L, GATE_PAD), F32),
            jax.ShapeDtypeStruct((N_GATES, L), F32),
        ],
        compiler_params=pltpu.CompilerParams(
            dimension_semantics=("parallel",), vmem_limit_bytes=VMEM_LIMIT_BYTES),
        name="in_proj",
    )(x, x, x, norm_w, w_qk, w_vou, w_gc, w_gr, conv_w, conv_b)


def _bf16_split3(x):
    hi = x.astype(BF16)
    r1 = x - hi.astype(F32)
    mid = r1.astype(BF16)
    lo = (r1 - mid.astype(F32)).astype(BF16)
    return hi, mid, lo


def _mlstm_direction(qk_ref, v_ref, gc_ref, gr_ref, bias_c_ref, bias_r_ref, h_ref, ct_ref, m_ref, d):
    T = CHUNK
    rr = lax.broadcasted_iota(jnp.int32, (T, T), 0)
    cc = lax.broadcasted_iota(jnp.int32, (T, T), 1)
    mask = (cc <= rr) if d == 0 else (cc >= rr)
    tri_c = mask.astype(BF16)
    tri_r = ((rr <= cc) if d == 0 else (rr >= cc)).astype(BF16)

    gcol = gc_ref[...] + bias_c_ref[...]
    grow = gr_ref[...] + bias_r_ref[...]
    lf_col = jnp.minimum(gcol, 0.0) - jnp.log1p(jnp.exp(-jnp.abs(gcol)))
    lf_row = jnp.minimum(grow, 0.0) - jnp.log1p(jnp.exp(-jnp.abs(grow)))
    cum_col = sum(jnp.dot(tri_c, part, preferred_element_type=F32) for part in _bf16_split3(lf_col))
    cum_row = sum(jnp.dot(part, tri_r, preferred_element_type=F32) for part in _bf16_split3(lf_row))
    tot_idx = T - 1 if d == 0 else 0
    ones = jnp.ones((T, HEAD_DIM), BF16)

    for h in range(N_HEADS):
        ci = d * N_HEADS + h
        cf = 2 * N_HEADS + d * N_HEADS + h
        hs = slice(h * HEAD_DIM, (h + 1) * HEAD_DIM)
        q = qk_ref[:, hs]
        k = qk_ref[:, D_MLSTM + h * HEAD_DIM:D_MLSTM + (h + 1) * HEAD_DIM]
        v = v_ref[:, hs]
        vaug = jnp.concatenate([v, ones], axis=1)
        b_col = cum_col[:, cf:cf + 1]
        i_col = gcol[:, ci:ci + 1]
        b_row = cum_row[cf:cf + 1, :]
        i_row = grow[ci:ci + 1, :]
        b_tot = b_row[:, tot_idx:tot_idx + 1]
        m_prev = m_ref[ci]
        ct_prev = ct_ref[ci]

        dmat = jnp.where(mask, b_col + (i_row - b_row), -jnp.inf)
        inter = b_col + m_prev
        m_t = jnp.maximum(jnp.max(dmat, axis=-1, keepdims=True), inter)
        w = jnp.exp(dmat - m_t)
        s_inter = jnp.exp(inter - m_t)
        s = lax.dot_general(q, k, (((1,), (1,)), ((), ())), preferred_element_type=F32) * w
        sv = jnp.dot(s.astype(BF16), vaug, preferred_element_type=F32)
        qc = jnp.dot(q, ct_prev.astype(BF16), preferred_element_type=F32)
        comb = sv + s_inter * qc
        num = comb[:, :HEAD_DIM]
        den = comb[:, HEAD_DIM:HEAD_DIM + 1]
        h_ref[:, hs] = num / jnp.maximum(jnp.abs(den), jnp.exp(-m_t))

        wk_row = b_tot - b_row + i_row
        m_new = jnp.maximum(b_tot + m_prev, jnp.max(wk_row, axis=-1, keepdims=True))
        sc = jnp.exp(b_tot + m_prev - m_new)
        wkk_col = jnp.exp(b_tot - b_col + i_col - m_new)
        kw = (k.astype(F32) * wkk_col).astype(BF16)
        upd = lax.dot_general(kw, vaug, (((0,), (0,)), ((), ())), preferred_element_type=F32)
        ct_ref[ci] = sc * ct_prev + upd
        m_ref[ci] = m_new


def _mlstm_body(qkf_ref, vf_ref, gcf_ref, grf_ref, qkb_ref, vb_ref, gcb_ref, grb_ref, bias_c_ref, bias_r_ref,
                hf_ref, hb_ref, ct_ref, m_ref):
    @pl.when(pl.program_id(0) == 0)
    def _():
        ct_ref[...] = jnp.zeros_like(ct_ref)
        m_ref[...] = jnp.full_like(m_ref, M_INIT)

    _mlstm_direction(qkf_ref, vf_ref, gcf_ref, grf_ref, bias_c_ref, bias_r_ref, hf_ref, ct_ref, m_ref, 0)
    _mlstm_direction(qkb_ref, vb_ref, gcb_ref, grb_ref, bias_c_ref, bias_r_ref, hb_ref, ct_ref, m_ref, 1)


def _mlstm(qk, v, gcol, grow, bias_c, bias_r):
    L = qk.shape[0]
    T = CHUNK
    assert L % T == 0
    nc = L // T
    fwd = lambda c: (c, 0)
    bwd = lambda c: (nc - 1 - c, 0)
    fwd_r = lambda c: (0, c)
    bwd_r = lambda c: (0, nc - 1 - c)
    return pl.pallas_call(
        _mlstm_body,
        grid=(nc,),
        in_specs=[
            pl.BlockSpec((T, QK_COLS), fwd), pl.BlockSpec((T, D_MLSTM), fwd),
            pl.BlockSpec((T, GATE_PAD), fwd), pl.BlockSpec((N_GATES, T), fwd_r),
            pl.BlockSpec((T, QK_COLS), bwd), pl.BlockSpec((T, D_MLSTM), bwd),
            pl.BlockSpec((T, GATE_PAD), bwd), pl.BlockSpec((N_GATES, T), bwd_r),
            pl.BlockSpec((1, GATE_PAD), lambda c: (0, 0)),
            pl.BlockSpec((N_GATES, 1), lambda c: (0, 0)),
        ],
        out_specs=[pl.BlockSpec((T, D_MLSTM), fwd), pl.BlockSpec((T, D_MLSTM), bwd)],
        out_shape=[jax.ShapeDtypeStruct((L, D_MLSTM), F32), jax.ShapeDtypeStruct((L, D_MLSTM), F32)],
        scratch_shapes=[
            pltpu.VMEM((2 * N_HEADS, HEAD_DIM, 2 * HEAD_DIM), F32),
            pltpu.VMEM((2 * N_HEADS, 1, 1), F32),
        ],
        compiler_params=pltpu.CompilerParams(
            dimension_semantics=("arbitrary",), vmem_limit_bytes=VMEM_LIMIT_BYTES),
        name="mlstm",
    )(qk, v, gcol, grow, qk, v, gcol, grow, bias_c, bias_r)


S5_PAIR_CH = 2 * S5_GC
S5_ROW = S5_BLK * S5_PAIR_CH
S5_ST = 2 * S5_P
S5_TILE = 8
S5_LANE_PAIRS = 4
S5_TT = 2048
S5_NBT = S5_TT // S5_BLK


def _s5_scan(a, sin_ref, nb):
    R = S5_TILE
    ntile = nb // R
    row = lax.broadcasted_iota(jnp.int32, (R, S5_ST), 0)
    zero = jnp.zeros((1, S5_ST), F32)

    def cmul(x, y):
        return x[0] * y[0] - x[1] * y[1], x[0] * y[1] + x[1] * y[0]

    def bcast(x):
        return tuple(jnp.broadcast_to(t, (R, S5_ST)) for t in x)

    def tables(ar, ai, backward):
        pw = {1: (ar, ai)}
        for e in range(2, R + 1):
            pw[e] = cmul(pw[e // 2], pw[e - e // 2])
        steps = []
        for sh in (1, 2, 4):
            keep = (row < R - sh) if backward else (row >= sh)
            steps.append(tuple(jnp.where(keep, t, 0.0) for t in bcast(pw[sh])))
        order = range(R, 0, -1) if backward else range(1, R + 1)
        carry_pw = tuple(jnp.concatenate([pw[e][j] for e in order], axis=0) for j in (0, 1))
        return steps, carry_pw

    def scan_tile(x, carry, steps, cpw, backward):
        for sh, am in zip((1, 2, 4), steps):
            rs = (R - sh) if backward else sh
            x = tuple(p + q for p, q in zip(x, cmul(am, (pltpu.roll(x[0], rs, 0), pltpu.roll(x[1], rs, 0)))))
        cb = bcast(carry)
        x = tuple(p + q for p, q in zip(x, cmul(cpw, cb)))
        edge, rs = (R - 1, R - 1) if backward else (0, 1)
        enter = tuple(jnp.where(row == edge, c, pltpu.roll(t, rs, 0)) for t, c in zip(x, cb))
        last = 0 if backward else R - 1
        return enter, (x[0][last:last + 1], x[1][last:last + 1])

    tabs = []
    for p in range(S5_LANE_PAIRS):
        ap = a[p]
        tabs.append((tables(ap[0:1], ap[1:2], False), tables(ap[2:3], ap[3:4], True)))

    def step(i, carry):
        rf = pl.multiple_of(i * R, R)
        rb = pl.multiple_of((ntile - 1 - i) * R, R)
        out = []
        for p in range(S5_LANE_PAIRS):
            (steps_f, cpw_f), (steps_b, cpw_b) = tabs[p]
            cf, cb = carry[p]
            xf = (sin_ref[p, pl.ds(rf, R), 0:S5_ST], sin_ref[p, pl.ds(rf, R), S5_ST:2 * S5_ST])
            xb = (sin_ref[p, pl.ds(rb, R), 2 * S5_ST:3 * S5_ST], sin_ref[p, pl.ds(rb, R), 3 * S5_ST:4 * S5_ST])
            ef, cf = scan_tile(xf, cf, steps_f, cpw_f, False)
            eb, cb = scan_tile(xb, cb, steps_b, cpw_b, True)
            sin_ref[p, pl.ds(rf, R), 0:S5_ST] = ef[0]
            sin_ref[p, pl.ds(rf, R), S5_ST:2 * S5_ST] = ef[1]
            sin_ref[p, pl.ds(rb, R), 2 * S5_ST:3 * S5_ST] = eb[0]
            sin_ref[p, pl.ds(rb, R), 3 * S5_ST:4 * S5_ST] = eb[1]
            out.append((cf, cb))
        return tuple(out)

    init = tuple(((zero, zero), (zero, zero)) for _ in range(S5_LANE_PAIRS))
    lax.fori_loop(0, ntile, step, init)


def _s5_body(u_ref, m_ref, win_ref, wout_ref, a_ref, y_ref, tok_ref, u2_ref, sin_ref, y2_ref, stage_ref):
    phase = pl.program_id(1)
    t = pl.program_id(2)
    nb = u2_ref.shape[1]
    r0 = pl.multiple_of(t * S5_NBT, S5_NBT)
    rows = pl.ds(r0, S5_NBT)

    @pl.when(phase == 0)
    def _():
        tok_ref[...] = u_ref[...].astype(F32)
        for s in range(S5_BLK):
            tok_s = tok_ref[pl.ds(s, S5_NBT, stride=S5_BLK), :]
            for p in range(S5_LANE_PAIRS):
                u2_ref[p, rows, s * S5_PAIR_CH:(s + 1) * S5_PAIR_CH] = (
                    tok_s[:, p * S5_PAIR_CH:(p + 1) * S5_PAIR_CH].astype(BF16))
        for p in range(S5_LANE_PAIRS):
            sin_ref[p, rows, :] = jnp.dot(u2_ref[p, rows, :], win_ref[p], preferred_element_type=F32)

    @pl.when((phase == 0) & (t == pl.num_programs(2) - 1))
    def _():
        _s5_scan(a_ref[...], sin_ref, nb)

    @pl.when(phase == 1)
    def _():
        for p in range(S5_LANE_PAIRS):
            y2_ref[p] = (jnp.dot(u2_ref[p, rows, :], m_ref[p], preferred_element_type=F32)
                         + jnp.dot(sin_ref[p, rows, :].astype(BF16), wout_ref[p], preferred_element_type=F32))
        for s in range(S5_BLK):
            for p in range(S5_LANE_PAIRS):
                stage_ref[:, p * S5_PAIR_CH:(p + 1) * S5_PAIR_CH] = y2_ref[p, :, s * S5_PAIR_CH:(s + 1) * S5_PAIR_CH]
            y_ref[pl.ds(s, S5_NBT, stride=S5_BLK), :] = stage_ref[...]


def _s5(u, m2, win2, wout2, a2):
    L = u.shape[0]
    assert L % S5_TT == 0 and D_S5 == S5_PAIRS * S5_PAIR_CH
    nb = L // S5_BLK
    nq = S5_PAIRS // S5_LANE_PAIRS
    lane_tile = S5_LANE_PAIRS * S5_PAIR_CH
    wspec = pl.BlockSpec((S5_LANE_PAIRS, S5_ROW, S5_ROW), lambda q, ph, t: (q, 0, 0))
    return pl.pallas_call(
        _s5_body,
        grid=(nq, 2, L // S5_TT),
        in_specs=[
            pl.BlockSpec((S5_TT, lane_tile), lambda q, ph, t: (t, q)),
            wspec, wspec, wspec,
            pl.BlockSpec((S5_LANE_PAIRS, 4, S5_ST), lambda q, ph, t: (q, 0, 0)),
        ],
        out_specs=pl.BlockSpec((S5_TT, lane_tile), lambda q, ph, t: (t * ph, q)),
        out_shape=jax.ShapeDtypeStruct((L, D_S5), F32),
        scratch_shapes=[
            pltpu.VMEM((S5_TT, lane_tile), F32),
            pltpu.VMEM((S5_LANE_PAIRS, nb, S5_ROW), BF16),
            pltpu.VMEM((S5_LANE_PAIRS, nb, 4 * S5_ST), F32),
            pltpu.VMEM((S5_LANE_PAIRS, S5_NBT, S5_ROW), F32),
            pltpu.VMEM((S5_NBT, lane_tile), F32),
        ],
        compiler_params=pltpu.CompilerParams(
            dimension_semantics=("parallel", "arbitrary", "arbitrary"), vmem_limit_bytes=VMEM_LIMIT_BYTES),
        name="s5",
    )(u, m2, win2, wout2, a2)


def _s5_weights(a_re, a_im, log_dt, b_re, b_im, c_re, c_im, d_skip):
    nt = S5_BLK
    lam = lax.complex(a_re, a_im)
    dt = jnp.exp(log_dt)[..., None]
    lam_bar = jnp.exp(lam * dt)
    b_bar = ((lam_bar - 1.0) / lam)[..., None] * lax.complex(b_re, b_im)
    c_mat = lax.complex(c_re, c_im)
    taus = jnp.arange(nt + 1, dtype=F32)
    pw = jnp.exp((lam * dt)[..., None] * taus)
    kern = jnp.real(jnp.einsum('zgop,zgpt,zgpi->zgtoi', c_mat, pw[..., :nt], b_bar, precision='highest'))
    t_idx = jnp.arange(nt)
    diff = t_idx[None, :] - t_idx[:, None]
    kf = kern[0][:, jnp.clip(diff, 0, nt - 1)]
    kb = kern[1][:, jnp.clip(-diff, 0, nt - 1)]
    mf = jnp.where((diff >= 0)[None, :, :, None, None], kf, 0.0)
    mb = jnp.where((diff <= 0)[None, :, :, None, None], kb, 0.0)
    eye = (diff == 0)[None, :, :, None, None] * jnp.eye(S5_GC, dtype=F32)[None, None, None] \
        * d_skip.reshape(S5_GROUPS, 1, 1, 1, S5_GC)
    m = (mf + mb + eye).transpose(0, 1, 4, 2, 3).reshape(S5_GROUPS, nt * S5_GC, nt * S5_GC)

    pw_in_f = pw[0][..., ::-1][..., 1:]
    pw_in_b = pw[1][..., :nt]
    win_f = jnp.einsum('gps,gpi->gsip', pw_in_f, b_bar[0]).reshape(S5_GROUPS, nt * S5_GC, S5_P)
    win_b = jnp.einsum('gps,gpi->gsip', pw_in_b, b_bar[1]).reshape(S5_GROUPS, nt * S5_GC, S5_P)
    pw_out_f = pw[0][..., 1:]
    pw_out_b = pw[1][..., ::-1][..., :nt]
    wo_f = jnp.einsum('gop,gpt->gpto', c_mat[0], pw_out_f).reshape(S5_GROUPS, S5_P, nt * S5_GC)
    wo_b = jnp.einsum('gop,gpt->gpto', c_mat[1], pw_out_b).reshape(S5_GROUPS, S5_P, nt * S5_GC)
    a_blk = pw[..., nt]

    eye2 = jnp.eye(2, dtype=F32)
    nc = S5_GC

    def pair_in_out(x):
        xp = x.reshape(S5_PAIRS, 2, nt, nc, nt, nc)
        return jnp.einsum('pjscto,jk->psjctko', xp, eye2).reshape(S5_PAIRS, S5_ROW, S5_ROW)

    def pair_in(x):
        xp = x.reshape(S5_PAIRS, 2, nt, nc, S5_P)
        return jnp.einsum('pjscn,jk->psjckn', xp, eye2).reshape(S5_PAIRS, S5_ROW, S5_ST)

    def pair_out(x):
        xp = x.reshape(S5_PAIRS, 2, S5_P, nt, nc)
        return jnp.einsum('pjnto,jk->pjntko', xp, eye2).reshape(S5_PAIRS, S5_ST, S5_ROW)

    m2 = pair_in_out(m)
    win2 = jnp.concatenate([pair_in(jnp.real(win_f)), pair_in(jnp.imag(win_f)),
                            pair_in(jnp.real(win_b)), pair_in(jnp.imag(win_b))], axis=2)
    wout2 = jnp.concatenate([pair_out(jnp.real(wo_f)), pair_out(-jnp.imag(wo_f)),
                             pair_out(jnp.real(wo_b)), pair_out(-jnp.imag(wo_b))], axis=1)
    a2 = jnp.stack([jnp.real(a_blk[0]), jnp.imag(a_blk[0]), jnp.real(a_blk[1]), jnp.imag(a_blk[1])],
                   axis=1).reshape(S5_PAIRS, 2, 4, S5_P).transpose(0, 2, 1, 3).reshape(S5_PAIRS, 4, S5_ST)
    return m2.astype(BF16), win2.astype(BF16), wout2.astype(BF16), a2


def _mix_body(x_ref, hf_ref, hb_ref, o_ref, y_ref, nw_ref, wglu_ref, wout_ref, out_ref):
    h = hf_ref[...] + hb_ref[...]
    parts = []
    for hd in range(N_HEADS):
        hh = h[:, hd * HEAD_DIM:(hd + 1) * HEAD_DIM]
        mu = jnp.mean(hh, axis=-1, keepdims=True)
        var = jnp.mean(jnp.square(hh - mu), axis=-1, keepdims=True)
        parts.append((hh - mu) * lax.rsqrt(var + EPS))
    hn = jnp.concatenate(parts, axis=1)
    h_m = hn * nw_ref[...] * _sigmoid(o_ref[...])
    y = y_ref[...]
    gelu = 0.5 * y * (1.0 + jnp.tanh(math.sqrt(2.0 / math.pi) * (y + 0.044715 * (y * y * y))))
    ab = jnp.dot(gelu.astype(BF16), wglu_ref[...], preferred_element_type=F32)
    h_s = ab[:, :D_S5] * _sigmoid(ab[:, D_S5:])
    mix = jnp.dot(h_m.astype(BF16), wout_ref[0:D_MLSTM, :], preferred_element_type=F32)
    mix += jnp.dot(h_s.astype(BF16), wout_ref[D_MLSTM:, :], preferred_element_type=F32)
    out_ref[...] = x_ref[...] + mix


def _mix(x, hf, hb, o_in, y, norm_w, w_glu, w_out):
    L = x.shape[0]
    tm = MIX_TM
    assert L % tm == 0
    whole = pl.BlockSpec(memory_space=pltpu.VMEM)
    row = lambda n: pl.BlockSpec((tm, n), lambda i: (i, 0))
    return pl.pallas_call(
        _mix_body,
        grid=(L // tm,),
        in_specs=[row(D_MODEL), row(D_MLSTM), row(D_MLSTM), row(D_MLSTM), row(D_S5), whole, whole, whole],
        out_specs=row(D_MODEL),
        out_shape=jax.ShapeDtypeStruct((L, D_MODEL), F32),
        compiler_params=pltpu.CompilerParams(
            dimension_semantics=("parallel",), vmem_limit_bytes=VMEM_LIMIT_BYTES),
        name="mix",
    )(x, hf, hb, o_in, y, norm_w, w_glu, w_out)


def _encode(x, p):
    x = _ffn(x, p["norm_ffn1"], p["ffn1_w_gate"], p["ffn1_w_up"], p["ffn1_w_down"], p["norm_final"],
             final_norm=False)
    qk, v, o_in, u, gcol, grow = _in_proj(x, p["norm_mix"], p["w_qk"], p["w_vou"], p["w_gc"], p["w_gr"],
                                          p["conv_w"], p["conv_b"])
    hf, hb = _mlstm(qk, v, gcol, grow, p["gate_bias_c"], p["gate_bias_r"])
    y = _s5(u, p["s5_m"], p["s5_win"], p["s5_wout"], p["s5_a"])
    x = _mix(x, hf, hb, o_in, y, p["mlstm_norm_w"], p["s5_w_glu"], p["w_out"])
    return _ffn(x, p["norm_ffn2"], p["ffn2_w_gate"], p["ffn2_w_up"], p["ffn2_w_down"], p["norm_final"],
                final_norm=True)


def _prepare(norm_ffn1, ffn1_w_gate, ffn1_w_up, ffn1_w_down, norm_mix, w_in, conv_w, conv_b, b_igate, b_fgate,
             mlstm_norm_w, s5_a_re, s5_a_im, s5_log_dt, s5_b_re, s5_b_im, s5_c_re, s5_c_im, s5_d, s5_w_glu,
             w_out, norm_ffn2, ffn2_w_gate, ffn2_w_up, ffn2_w_down, norm_final):
    l = 0
    w = w_in[l]
    g0 = 4 * D_MLSTM
    w_g = w[:, g0:g0 + N_GATES]
    gate_bias = jnp.concatenate([b_igate[l].reshape(-1), b_fgate[l].reshape(-1)])
    m2, win2, wout2, a2 = _s5_weights(s5_a_re[l], s5_a_im[l], s5_log_dt[l], s5_b_re[l], s5_b_im[l],
                                      s5_c_re[l], s5_c_im[l], s5_d[l])
    row = lambda a: a.reshape(1, -1).astype(F32)
    return {
        "norm_ffn1": row(norm_ffn1[l]), "norm_ffn2": row(norm_ffn2[l]), "norm_final": row(norm_final),
        "ffn1_w_gate": ffn1_w_gate[l].astype(BF16), "ffn1_w_up": ffn1_w_up[l].astype(BF16),
        "ffn1_w_down": ffn1_w_down[l].astype(BF16),
        "ffn2_w_gate": ffn2_w_gate[l].astype(BF16), "ffn2_w_up": ffn2_w_up[l].astype(BF16),
        "ffn2_w_down": ffn2_w_down[l].astype(BF16),
        "norm_mix": row(norm_mix[l]),
        "w_qk": w[:, :QK_COLS].astype(BF16),
        "w_vou": jnp.concatenate([w[:, QK_COLS:g0], w[:, g0 + N_GATES:]], axis=1).astype(BF16),
        "w_gc": jnp.pad(w_g, ((0, 0), (0, GATE_PAD - N_GATES))).astype(BF16),
        "w_gr": w_g.T.astype(BF16),
        "conv_w": conv_w[l].astype(F32), "conv_b": row(conv_b[l]),
        "gate_bias_c": jnp.pad(gate_bias, (0, GATE_PAD - N_GATES)).reshape(1, GATE_PAD),
        "gate_bias_r": gate_bias.reshape(N_GATES, 1),
        "mlstm_norm_w": row(mlstm_norm_w[l]),
        "s5_m": m2, "s5_win": win2, "s5_wout": wout2, "s5_a": a2,
        "s5_w_glu": s5_w_glu[l].astype(BF16), "w_out": w_out[l].astype(BF16),
    }


def kernel(x_prompt, x_sample, norm_ffn1, ffn1_w_gate, ffn1_w_up, ffn1_w_down, norm_mix, w_in, conv_w, conv_b, b_igate, b_fgate, mlstm_norm_w, s5_a_re, s5_a_im, s5_log_dt, s5_b_re, s5_b_im, s5_c_re, s5_c_im, s5_d, s5_w_glu, w_out, norm_ffn2, ffn2_w_gate, ffn2_w_up, ffn2_w_down, norm_final):
    assert norm_ffn1.shape[0] == 1 and x_prompt.shape[0] == 1 and x_sample.shape[0] == 1
    p = _prepare(norm_ffn1, ffn1_w_gate, ffn1_w_up, ffn1_w_down, norm_mix, w_in, conv_w, conv_b, b_igate,
                 b_fgate, mlstm_norm_w, s5_a_re, s5_a_im, s5_log_dt, s5_b_re, s5_b_im, s5_c_re, s5_c_im, s5_d,
                 s5_w_glu, w_out, norm_ffn2, ffn2_w_gate, ffn2_w_up, ffn2_w_down, norm_final)
    y_prompt = _encode(x_prompt[0], p)[None]
    y_sample = _encode(x_sample[0], p)[None]
    return (y_prompt, y_sample)
```

```python
import functools
import math

import jax
import jax.numpy as jnp
from jax import lax
from jax.experimental import pallas as pl
from jax.experimental.pallas import tpu as pltpu

F32 = jnp.float32
BF16 = jnp.bfloat16

D_MODEL = 2048
D_MLSTM = 1024
D_S5 = 1024
N_HEADS = 8
HEAD_DIM = 128
CHUNK = 128
N_GATES = 32
S5_GROUPS = 64
S5_GC = 16
S5_P = 64
S5_BLK = 16
S5_PAIRS = S5_GROUPS // 2
D_FF = 5632
EPS = 1e-6
M_INIT = -1e30

VMEM_LIMIT_BYTES = 56 * 1024 * 1024

FFN_TM = 512
FFN_TF = 512
PROJ_TM = 256
PROJ_HALO = 8
MIX_TM = 256


def _sigmoid(x):
    return 1.0 / (1.0 + jnp.exp(-x))


def _rmsnorm(x, w):
    return x * lax.rsqrt(jnp.mean(x * x, axis=-1, keepdims=True) + EPS) * w


def _ffn_body(x_ref, nw_ref, wg_ref, wu_ref, wd_ref, nf_ref, o_ref, xn_ref, *, final_norm):
    j = pl.program_id(1)

    @pl.when(j == 0)
    def _():
        x = x_ref[...]
        xn_ref[...] = _rmsnorm(x, nw_ref[...]).astype(BF16)
        o_ref[...] = x

    xn = xn_ref[...]
    g = jnp.dot(xn, wg_ref[...], preferred_element_type=F32)
    u = jnp.dot(xn, wu_ref[...], preferred_element_type=F32)
    h = (0.5 * g * _sigmoid(g)) * u
    o_ref[...] += jnp.dot(h.astype(BF16), wd_ref[...], preferred_element_type=F32)

    if final_norm:
        @pl.when(j == pl.num_programs(1) - 1)
        def _():
            o_ref[...] = _rmsnorm(o_ref[...], nf_ref[...])


def _ffn(x, norm_w, w_gate, w_up, w_down, norm_final, *, final_norm):
    L = x.shape[0]
    assert L % FFN_TM == 0 and D_FF % FFN_TF == 0
    return pl.pallas_call(
        functools.partial(_ffn_body, final_norm=final_norm),
        grid=(L // FFN_TM, D_FF // FFN_TF),
        in_specs=[
            pl.BlockSpec((FFN_TM, D_MODEL), lambda i, j: (i, 0)),
            pl.BlockSpec((1, D_MODEL), lambda i, j: (0, 0)),
            pl.BlockSpec((D_MODEL, FFN_TF), lambda i, j: (0, j)),
            pl.BlockSpec((D_MODEL, FFN_TF), lambda i, j: (0, j)),
            pl.BlockSpec((FFN_TF, D_MODEL), lambda i, j: (j, 0)),
            pl.BlockSpec((1, D_MODEL), lambda i, j: (0, 0)),
        ],
        out_specs=pl.BlockSpec((FFN_TM, D_MODEL), lambda i, j: (i, 0)),
        out_shape=jax.ShapeDtypeStruct((L, D_MODEL), F32),
        scratch_shapes=[pltpu.VMEM((FFN_TM, D_MODEL), BF16)],
        compiler_params=pltpu.CompilerParams(
            dimension_semantics=("parallel", "arbitrary"), vmem_limit_bytes=VMEM_LIMIT_BYTES),
        name="ffn_final" if final_norm else "ffn",
    )(x, norm_w, w_gate, w_up, w_down, norm_final)


QK_COLS = 2 * D_MLSTM
QK_CB = 512


def _in_proj_body(xp_ref, x_ref, xnx_ref, nw_ref, wqk_ref, wvou_ref, wgr_ref,
                  cw_ref, cb_ref, q_ref, kt_ref, v_ref, o_ref, u_ref, gr_ref):
    i = pl.program_id(0)
    tm = x_ref.shape[0]
    nw = nw_ref[...]
    xn = _rmsnorm(x_ref[...], nw)
    xn_prev = jnp.where(i == 0, 0.0, _rmsnorm(xp_ref[...], nw))
    xn_next = jnp.where(i == pl.num_programs(0) - 1, 0.0, _rmsnorm(xnx_ref[...], nw))
    xn_b = xn.astype(BF16)
    xe_b = jnp.concatenate([xn_prev, xn, xn_next], axis=0).astype(BF16)
    rows = tm + 2 * PROJ_HALO

    for c in range(QK_COLS // QK_CB):
        cs = slice(c * QK_CB, (c + 1) * QK_CB)
        z = jnp.dot(xe_b, wqk_ref[:, cs], preferred_element_type=F32)
        z_m1 = pltpu.roll(z, 1, 0)[PROJ_HALO:PROJ_HALO + tm]
        z_0 = z[PROJ_HALO:PROJ_HALO + tm]
        z_p1 = pltpu.roll(z, rows - 1, 0)[PROJ_HALO:PROJ_HALO + tm]
        y = z_m1 * cw_ref[0:1, cs] + z_0 * cw_ref[1:2, cs] + z_p1 * cw_ref[2:3, cs] + cb_ref[:, cs]
        y = y * _sigmoid(y)
        if c * QK_CB < D_MLSTM:
            q_ref[:, cs] = (y * (HEAD_DIM ** -0.5)).astype(q_ref.dtype)
        else:
            kt_ref[c * QK_CB - D_MLSTM:(c + 1) * QK_CB - D_MLSTM, :] = y.T.astype(kt_ref.dtype)

    v_ref[...] = jnp.dot(xn_b, wvou_ref[:, 0:D_MLSTM], preferred_element_type=F32).astype(v_ref.dtype)
    o_ref[...] = jnp.dot(xn_b, wvou_ref[:, D_MLSTM:2 * D_MLSTM], preferred_element_type=F32)
    u_ref[...] = jnp.dot(xn_b, wvou_ref[:, 2 * D_MLSTM:], preferred_element_type=F32).astype(u_ref.dtype)
    gr_ref[...] = lax.dot_general(wgr_ref[...], xn_b, (((1,), (1,)), ((), ())), preferred_element_type=F32)


def _in_proj(x, norm_w, w_qk, w_vou, w_gr, conv_w, conv_b):
    L = x.shape[0]
    tm = PROJ_TM
    assert L % tm == 0
    hb = tm // PROJ_HALO
    nblk8 = L // PROJ_HALO
    whole = pl.BlockSpec(memory_space=pltpu.VMEM)
    return pl.pallas_call(
        _in_proj_body,
        grid=(L // tm,),
        in_specs=[
            pl.BlockSpec((PROJ_HALO, D_MODEL), lambda i: (jnp.maximum(i * hb - 1, 0), 0)),
            pl.BlockSpec((tm, D_MODEL), lambda i: (i, 0)),
            pl.BlockSpec((PROJ_HALO, D_MODEL), lambda i: (jnp.minimum((i + 1) * hb, nblk8 - 1), 0)),
            whole, whole, whole, whole, whole, whole,
        ],
        out_specs=[
            pl.BlockSpec((tm, D_MLSTM), lambda i: (i, 0)),
            pl.BlockSpec((D_MLSTM, tm), lambda i: (0, i)),
            pl.BlockSpec((tm, D_MLSTM), lambda i: (i, 0)),
            pl.BlockSpec((tm, D_MLSTM), lambda i: (i, 0)),
            pl.BlockSpec((tm, D_S5), lambda i: (i, 0)),
            pl.BlockSpec((N_GATES, tm), lambda i: (0, i)),
        ],
        out_shape=[
            jax.ShapeDtypeStruct((L, D_MLSTM), BF16),
            jax.ShapeDtypeStruct((D_MLSTM, L), BF16),
            jax.ShapeDtypeStruct((L, D_MLSTM), BF16),
            jax.ShapeDtypeStruct((L, D_MLSTM), F32),
            jax.ShapeDtypeStruct((L, D_S5), BF16),
            jax.ShapeDtypeStruct((N_GATES, L), F32),
        ],
        compiler_params=pltpu.CompilerParams(
            dimension_semantics=("parallel",), vmem_limit_bytes=VMEM_LIMIT_BYTES),
        name="in_proj",
    )(x, x, x, norm_w, w_qk, w_vou, w_gr, conv_w, conv_b)


def _bf16_split3(x):
    hi = x.astype(BF16)
    r1 = x - hi.astype(F32)
    mid = r1.astype(BF16)
    lo = (r1 - mid.astype(F32)).astype(BF16)
    return hi, mid, lo


def _lane_cummax(x, backward):
    n = x.shape[-1]
    lane = lax.broadcasted_iota(jnp.int32, x.shape, x.ndim - 1)
    sh = 1
    while sh < n:
        if backward:
            x = jnp.maximum(x, jnp.where(lane < n - sh, pltpu.roll(x, n - sh, x.ndim - 1), -jnp.inf))
        else:
            x = jnp.maximum(x, jnp.where(lane >= sh, pltpu.roll(x, sh, x.ndim - 1), -jnp.inf))
        sh *= 2
    return x


def _mlstm_direction(q_ref, kt_ref, v_ref, gr_ref, bias_r_ref, h_ref, ct_ref, m_ref, d):
    T = CHUNK
    H = N_HEADS
    backward = d == 1
    rr = lax.broadcasted_iota(jnp.int32, (T, T), 0)
    cc = lax.broadcasted_iota(jnp.int32, (T, T), 1)
    mask = (cc >= rr) if backward else (cc <= rr)
    tri = mask.astype(BF16)
    tri_r = ((rr >= cc) if backward else (rr <= cc)).astype(BF16)
    eye = (rr == cc).astype(BF16)
    nt_dims = (((1,), (1,)), ((), ()))
    last = 0 if backward else T - 1

    g = gr_ref[...] + bias_r_ref[...]
    i_rows = g[d * H:(d + 1) * H]
    f_rows = g[(2 + d) * H:(3 + d) * H]
    lf_rows = jnp.minimum(f_rows, 0.0) - jnp.log1p(jnp.exp(-jnp.abs(f_rows)))
    lf_parts = _bf16_split3(lf_rows)
    b_rows = sum(jnp.dot(part, tri_r, preferred_element_type=F32) for part in lf_parts)
    b_cols = sum(lax.dot_general(tri, part, nt_dims, preferred_element_type=F32) for part in lf_parts)
    r_rows = i_rows - b_rows
    m_prev = m_ref[d]
    big_m = jnp.maximum(_lane_cummax(r_rows, backward), m_prev)
    big_m_bf = big_m.astype(BF16)
    m_cols = lax.dot_general(eye, big_m_bf, nt_dims, preferred_element_type=F32)
    clamp_cols = -(b_cols + m_cols)
    m_last = jnp.broadcast_to(big_m[:, last:last + 1], (H, T))
    b_tot = jnp.broadcast_to(b_rows[:, last:last + 1], (H, T))
    sc_rows = jnp.exp(m_prev - m_last)
    wkk_rows = jnp.exp(r_rows - m_last)
    ones = jnp.ones((T, HEAD_DIM), BF16)

    for h in range(H):
        ci = d * H + h
        hs = slice(h * HEAD_DIM, (h + 1) * HEAD_DIM)
        q = q_ref[:, hs]
        kt = kt_ref[hs, :]
        vaug = jnp.concatenate([v_ref[:, hs], ones], axis=1)
        ct_prev = ct_ref[ci]
        m_col = jnp.broadcast_to(m_cols[:, h:h + 1], (T, T))
        w = jnp.exp(jnp.where(mask, r_rows[h:h + 1, :] - m_col, -jnp.inf))
        s = jnp.dot(q, kt, preferred_element_type=F32) * w
        sv = jnp.dot(s.astype(BF16), vaug, preferred_element_type=F32)
        qc = jnp.dot(q, ct_prev.astype(BF16), preferred_element_type=F32)
        s_inter = jnp.exp(m_prev[h:h + 1, :] - m_col)
        num = sv[:, :HEAD_DIM] + s_inter * qc[:, :HEAD_DIM]
        den = sv[:, HEAD_DIM:] + s_inter * qc[:, HEAD_DIM:]
        floor = jnp.exp(jnp.broadcast_to(clamp_cols[:, h:h + 1], (T, HEAD_DIM)))
        h_ref[:, hs] = num / jnp.maximum(jnp.abs(den), floor)

        kw = (kt.astype(F32) * wkk_rows[h:h + 1, :]).astype(BF16)
        upd = jnp.dot(kw, vaug, preferred_element_type=F32)
        sc = sc_rows[h:h + 1, :]
        ct_ref[ci] = jnp.concatenate([sc, sc], axis=1) * ct_prev + upd

    m_ref[d] = b_tot + m_last


def _mlstm_body(qf_ref, ktf_ref, vf_ref, grf_ref, qb_ref, ktb_ref, vb_ref, grb_ref, bias_r_ref,
                hf_ref, hb_ref, ct_ref, m_ref):
    @pl.when(pl.program_id(0) == 0)
    def _():
        ct_ref[...] = jnp.zeros_like(ct_ref)
        m_ref[...] = jnp.full_like(m_ref, M_INIT)

    _mlstm_direction(qf_ref, ktf_ref, vf_ref, grf_ref, bias_r_ref, hf_ref, ct_ref, m_ref, 0)
    _mlstm_direction(qb_ref, ktb_ref, vb_ref, grb_ref, bias_r_ref, hb_ref, ct_ref, m_ref, 1)


def _mlstm(q, kt, v, grow, bias_r):
    L = q.shape[0]
    T = CHUNK
    assert L % T == 0 and T == HEAD_DIM
    nc = L // T
    fwd = lambda c: (c, 0)
    bwd = lambda c: (nc - 1 - c, 0)
    fwd_r = lambda c: (0, c)
    bwd_r = lambda c: (0, nc - 1 - c)
    return pl.pallas_call(
        _mlstm_body,
        grid=(nc,),
        in_specs=[
            pl.BlockSpec((T, D_MLSTM), fwd), pl.BlockSpec((D_MLSTM, T), fwd_r),
            pl.BlockSpec((T, D_MLSTM), fwd), pl.BlockSpec((N_GATES, T), fwd_r),
            pl.BlockSpec((T, D_MLSTM), bwd), pl.BlockSpec((D_MLSTM, T), bwd_r),
            pl.BlockSpec((T, D_MLSTM), bwd), pl.BlockSpec((N_GATES, T), bwd_r),
            pl.BlockSpec((N_GATES, 1), lambda c: (0, 0)),
        ],
        out_specs=[pl.BlockSpec((T, D_MLSTM), fwd), pl.BlockSpec((T, D_MLSTM), bwd)],
        out_shape=[jax.ShapeDtypeStruct((L, D_MLSTM), F32), jax.ShapeDtypeStruct((L, D_MLSTM), F32)],
        scratch_shapes=[
            pltpu.VMEM((2 * N_HEADS, HEAD_DIM, 2 * HEAD_DIM), F32),
            pltpu.VMEM((2, N_HEADS, T), F32),
        ],
        compiler_params=pltpu.CompilerParams(
            dimension_semantics=("arbitrary",), vmem_limit_bytes=VMEM_LIMIT_BYTES),
        name="mlstm",
    )(q, kt, v, grow, q, kt, v, grow, bias_r)


S5_PAIR_CH = 2 * S5_GC
S5_ROW = S5_BLK * S5_PAIR_CH
S5_ST = 2 * S5_P
S5_TILE = 8
S5_LANE_PAIRS = 4
S5_TT = 2048
S5_NBT = S5_TT // S5_BLK


def _s5_scan(a, sin_ref, nb):
    R = S5_TILE
    ntile = nb // R
    row = lax.broadcasted_iota(jnp.int32, (R, S5_ST), 0)
    zero = jnp.zeros((1, S5_ST), F32)

    def cmul(x, y):
        return x[0] * y[0] - x[1] * y[1], x[0] * y[1] + x[1] * y[0]

    def bcast(x):
        return tuple(jnp.broadcast_to(t, (R, S5_ST)) for t in x)

    def tables(ar, ai, backward):
        pw = {1: (ar, ai)}
        for e in range(2, R + 1):
            pw[e] = cmul(pw[e // 2], pw[e - e // 2])
        steps = []
        for sh in (1, 2, 4):
            keep = (row < R - sh) if backward else (row >= sh)
            steps.append(tuple(jnp.where(keep, t, 0.0) for t in bcast(pw[sh])))
        order = range(R, 0, -1) if backward else range(1, R + 1)
        carry_pw = tuple(jnp.concatenate([pw[e][j] for e in order], axis=0) for j in (0, 1))
        return steps, carry_pw

    def scan_tile(x, carry, steps, cpw, backward):
        for sh, am in zip((1, 2, 4), steps):
            rs = (R - sh) if backward else sh
            x = tuple(p + q for p, q in zip(x, cmul(am, (pltpu.roll(x[0], rs, 0), pltpu.roll(x[1], rs, 0)))))
        cb = bcast(carry)
        x = tuple(p + q for p, q in zip(x, cmul(cpw, cb)))
        edge, rs = (R - 1, R - 1) if backward else (0, 1)
        enter = tuple(jnp.where(row == edge, c, pltpu.roll(t, rs, 0)) for t, c in zip(x, cb))
        last = 0 if backward else R - 1
        return enter, (x[0][last:last + 1], x[1][last:last + 1])

    tabs = []
    for p in range(S5_LANE_PAIRS):
        ap = a[p]
        tabs.append((tables(ap[0:1], ap[1:2], False), tables(ap[2:3], ap[3:4], True)))

    def step(i, carry):
        rf = pl.multiple_of(i * R, R)
        rb = pl.multiple_of((ntile - 1 - i) * R, R)
        out = []
        for p in range(S5_LANE_PAIRS):
            (steps_f, cpw_f), (steps_b, cpw_b) = tabs[p]
            cf, cb = carry[p]
            xf = (sin_ref[p, pl.ds(rf, R), 0:S5_ST], sin_ref[p, pl.ds(rf, R), S5_ST:2 * S5_ST])
            xb = (sin_ref[p, pl.ds(rb, R), 2 * S5_ST:3 * S5_ST], sin_ref[p, pl.ds(rb, R), 3 * S5_ST:4 * S5_ST])
            ef, cf = scan_tile(xf, cf, steps_f, cpw_f, False)
            eb, cb = scan_tile(xb, cb, steps_b, cpw_b, True)
            sin_ref[p, pl.ds(rf, R), 0:S5_ST] = ef[0]
            sin_ref[p, pl.ds(rf, R), S5_ST:2 * S5_ST] = ef[1]
            sin_ref[p, pl.ds(rb, R), 2 * S5_ST:3 * S5_ST] = eb[0]
            sin_ref[p, pl.ds(rb, R), 3 * S5_ST:4 * S5_ST] = eb[1]
            out.append((cf, cb))
        return tuple(out)

    init = tuple(((zero, zero), (zero, zero)) for _ in range(S5_LANE_PAIRS))
    lax.fori_loop(0, ntile, step, init)


def _s5_body(u_ref, m_ref, win_ref, wout_ref, a_ref, y_ref, tok_ref, u2_ref, sin_ref, y2_ref, stage_ref):
    phase = pl.program_id(1)
    t = pl.program_id(2)
    nb = u2_ref.shape[1]
    r0 = pl.multiple_of(t * S5_NBT, S5_NBT)
    rows = pl.ds(r0, S5_NBT)

    @pl.when(phase == 0)
    def _():
        tok_ref[...] = u_ref[...].astype(F32)
        for s in range(S5_BLK):
            tok_s = tok_ref[pl.ds(s, S5_NBT, stride=S5_BLK), :]
            for p in range(S5_LANE_PAIRS):
                u2_ref[p, rows, s * S5_PAIR_CH:(s + 1) * S5_PAIR_CH] = (
                    tok_s[:, p * S5_PAIR_CH:(p + 1) * S5_PAIR_CH].astype(BF16))
        for p in range(S5_LANE_PAIRS):
            sin_ref[p, rows, :] = jnp.dot(u2_ref[p, rows, :], win_ref[p], preferred_element_type=F32)

    @pl.when((phase == 0) & (t == pl.num_programs(2) - 1))
    def _():
        _s5_scan(a_ref[...], sin_ref, nb)

    @pl.when(phase == 1)
    def _():
        for p in range(S5_LANE_PAIRS):
            y2_ref[p] = (jnp.dot(u2_ref[p, rows, :], m_ref[p], preferred_element_type=F32)
                         + jnp.dot(sin_ref[p, rows, :].astype(BF16), wout_ref[p], preferred_element_type=F32))
        for s in range(S5_BLK):
            for p in range(S5_LANE_PAIRS):
                stage_ref[:, p * S5_PAIR_CH:(p + 1) * S5_PAIR_CH] = y2_ref[p, :, s * S5_PAIR_CH:(s + 1) * S5_PAIR_CH]
            y_ref[pl.ds(s, S5_NBT, stride=S5_BLK), :] = stage_ref[...]


def _s5(u, m2, win2, wout2, a2):
    L = u.shape[0]
    assert L % S5_TT == 0 and D_S5 == S5_PAIRS * S5_PAIR_CH
    nb = L // S5_BLK
    nq = S5_PAIRS // S5_LANE_PAIRS
    lane_tile = S5_LANE_PAIRS * S5_PAIR_CH
    wspec = pl.BlockSpec((S5_LANE_PAIRS, S5_ROW, S5_ROW), lambda q, ph, t: (q, 0, 0))
    return pl.pallas_call(
        _s5_body,
        grid=(nq, 2, L // S5_TT),
        in_specs=[
            pl.BlockSpec((S5_TT, lane_tile), lambda q, ph, t: (t, q)),
            wspec, wspec, wspec,
            pl.BlockSpec((S5_LANE_PAIRS, 4, S5_ST), lambda q, ph, t: (q, 0, 0)),
        ],
        out_specs=pl.BlockSpec((S5_TT, lane_tile), lambda q, ph, t: (t * ph, q)),
        out_shape=jax.ShapeDtypeStruct((L, D_S5), F32),
        scratch_shapes=[
            pltpu.VMEM((S5_TT, lane_tile), F32),
            pltpu.VMEM((S5_LANE_PAIRS, nb, S5_ROW), BF16),
            pltpu.VMEM((S5_LANE_PAIRS, nb, 4 * S5_ST), F32),
            pltpu.VMEM((S5_LANE_PAIRS, S5_NBT, S5_ROW), F32),
            pltpu.VMEM((S5_NBT, lane_tile), F32),
        ],
        compiler_params=pltpu.CompilerParams(
            dimension_semantics=("parallel", "arbitrary", "arbitrary"), vmem_limit_bytes=VMEM_LIMIT_BYTES),
        name="s5",
    )(u, m2, win2, wout2, a2)


def _s5_weights(a_re, a_im, log_dt, b_re, b_im, c_re, c_im, d_skip):
    nt = S5_BLK
    lam = lax.complex(a_re, a_im)
    dt = jnp.exp(log_dt)[..., None]
    lam_bar = jnp.exp(lam * dt)
    b_bar = ((lam_bar - 1.0) / lam)[..., None] * lax.complex(b_re, b_im)
    c_mat = lax.complex(c_re, c_im)
    taus = jnp.arange(nt + 1, dtype=F32)
    pw = jnp.exp((lam * dt)[..., None] * taus)
    kern = jnp.real(jnp.einsum('zgop,zgpt,zgpi->zgtoi', c_mat, pw[..., :nt], b_bar, precision='highest'))
    t_idx = jnp.arange(nt)
    diff = t_idx[None, :] - t_idx[:, None]
    kf = kern[0][:, jnp.clip(diff, 0, nt - 1)]
    kb = kern[1][:, jnp.clip(-diff, 0, nt - 1)]
    mf = jnp.where((diff >= 0)[None, :, :, None, None], kf, 0.0)
    mb = jnp.where((diff <= 0)[None, :, :, None, None], kb, 0.0)
    eye = (diff == 0)[None, :, :, None, None] * jnp.eye(S5_GC, dtype=F32)[None, None, None] \
        * d_skip.reshape(S5_GROUPS, 1, 1, 1, S5_GC)
    m = (mf + mb + eye).transpose(0, 1, 4, 2, 3).reshape(S5_GROUPS, nt * S5_GC, nt * S5_GC)

    pw_in_f = pw[0][..., ::-1][..., 1:]
    pw_in_b = pw[1][..., :nt]
    win_f = jnp.einsum('gps,gpi->gsip', pw_in_f, b_bar[0]).reshape(S5_GROUPS, nt * S5_GC, S5_P)
    win_b = jnp.einsum('gps,gpi->gsip', pw_in_b, b_bar[1]).reshape(S5_GROUPS, nt * S5_GC, S5_P)
    pw_out_f = pw[0][..., 1:]
    pw_out_b = pw[1][..., ::-1][..., :nt]
    wo_f = jnp.einsum('gop,gpt->gpto', c_mat[0], pw_out_f).reshape(S5_GROUPS, S5_P, nt * S5_GC)
    wo_b = jnp.einsum('gop,gpt->gpto', c_mat[1], pw_out_b).reshape(S5_GROUPS, S5_P, nt * S5_GC)
    a_blk = pw[..., nt]

    eye2 = jnp.eye(2, dtype=F32)
    nc = S5_GC

    def pair_in_out(x):
        xp = x.reshape(S5_PAIRS, 2, nt, nc, nt, nc)
        return jnp.einsum('pjscto,jk->psjctko', xp, eye2).reshape(S5_PAIRS, S5_ROW, S5_ROW)

    def pair_in(x):
        xp = x.reshape(S5_PAIRS, 2, nt, nc, S5_P)
        return jnp.einsum('pjscn,jk->psjckn', xp, eye2).reshape(S5_PAIRS, S5_ROW, S5_ST)

    def pair_out(x):
        xp = x.reshape(S5_PAIRS, 2, S5_P, nt, nc)
        return jnp.einsum('pjnto,jk->pjntko', xp, eye2).reshape(S5_PAIRS, S5_ST, S5_ROW)

    m2 = pair_in_out(m)
    win2 = jnp.concatenate([pair_in(jnp.real(win_f)), pair_in(jnp.imag(win_f)),
                            pair_in(jnp.real(win_b)), pair_in(jnp.imag(win_b))], axis=2)
    wout2 = jnp.concatenate([pair_out(jnp.real(wo_f)), pair_out(-jnp.imag(wo_f)),
                             pair_out(jnp.real(wo_b)), pair_out(-jnp.imag(wo_b))], axis=1)
    a2 = jnp.stack([jnp.real(a_blk[0]), jnp.imag(a_blk[0]), jnp.real(a_blk[1]), jnp.imag(a_blk[1])],
                   axis=1).reshape(S5_PAIRS, 2, 4, S5_P).transpose(0, 2, 1, 3).reshape(S5_PAIRS, 4, S5_ST)
    return m2.astype(BF16), win2.astype(BF16), wout2.astype(BF16), a2


def _mix_body(x_ref, hf_ref, hb_ref, o_ref, y_ref, nw_ref, wglu_ref, wout_ref, out_ref):
    h = hf_ref[...] + hb_ref[...]
    parts = []
    for hd in range(N_HEADS):
        hh = h[:, hd * HEAD_DIM:(hd + 1) * HEAD_DIM]
        mu = jnp.mean(hh, axis=-1, keepdims=True)
        var = jnp.mean(jnp.square(hh - mu), axis=-1, keepdims=True)
        parts.append((hh - mu) * lax.rsqrt(var + EPS))
    hn = jnp.concatenate(parts, axis=1)
    h_m = hn * nw_ref[...] * _sigmoid(o_ref[...])
    y = y_ref[...]
    gelu = 0.5 * y * (1.0 + jnp.tanh(math.sqrt(2.0 / math.pi) * (y + 0.044715 * (y * y * y))))
    ab = jnp.dot(gelu.astype(BF16), wglu_ref[...], preferred_element_type=F32)
    h_s = ab[:, :D_S5] * _sigmoid(ab[:, D_S5:])
    mix = jnp.dot(h_m.astype(BF16), wout_ref[0:D_MLSTM, :], preferred_element_type=F32)
    mix += jnp.dot(h_s.astype(BF16), wout_ref[D_MLSTM:, :], preferred_element_type=F32)
    out_ref[...] = x_ref[...] + mix


def _mix(x, hf, hb, o_in, y, norm_w, w_glu, w_out):
    L = x.shape[0]
    tm = MIX_TM
    assert L % tm == 0
    whole = pl.BlockSpec(memory_space=pltpu.VMEM)
    row = lambda n: pl.BlockSpec((tm, n), lambda i: (i, 0))
    return pl.pallas_call(
        _mix_body,
        grid=(L // tm,),
        in_specs=[row(D_MODEL), row(D_MLSTM), row(D_MLSTM), row(D_MLSTM), row(D_S5), whole, whole, whole],
        out_specs=row(D_MODEL),
        out_shape=jax.ShapeDtypeStruct((L, D_MODEL), F32),
        compiler_params=pltpu.CompilerParams(
            dimension_semantics=("parallel",), vmem_limit_bytes=VMEM_LIMIT_BYTES),
        name="mix",
    )(x, hf, hb, o_in, y, norm_w, w_glu, w_out)


def _encode(x, p):
    x = _ffn(x, p["norm_ffn1"], p["ffn1_w_gate"], p["ffn1_w_up"], p["ffn1_w_down"], p["norm_final"],
             final_norm=False)
    q, kt, v, o_in, u, grow = _in_proj(x, p["norm_mix"], p["w_qk"], p["w_vou"], p["w_gr"], p["conv_w"], p["conv_b"])
    hf, hb = _mlstm(q, kt, v, grow, p["gate_bias_r"])
    y = _s5(u, p["s5_m"], p["s5_win"], p["s5_wout"], p["s5_a"])
    x = _mix(x, hf, hb, o_in, y, p["mlstm_norm_w"], p["s5_w_glu"], p["w_out"])
    return _ffn(x, p["norm_ffn2"], p["ffn2_w_gate"], p["ffn2_w_up"], p["ffn2_w_down"], p["norm_final"],
                final_norm=True)


def _prepare(norm_ffn1, ffn1_w_gate, ffn1_w_up, ffn1_w_down, norm_mix, w_in, conv_w, conv_b, b_igate, b_fgate,
             mlstm_norm_w, s5_a_re, s5_a_im, s5_log_dt, s5_b_re, s5_b_im, s5_c_re, s5_c_im, s5_d, s5_w_glu,
             w_out, norm_ffn2, ffn2_w_gate, ffn2_w_up, ffn2_w_down, norm_final):
    l = 0
    w = w_in[l]
    g0 = 4 * D_MLSTM
    w_g = w[:, g0:g0 + N_GATES]
    gate_bias = jnp.concatenate([b_igate[l].reshape(-1), b_fgate[l].reshape(-1)])
    m2, win2, wout2, a2 = _s5_weights(s5_a_re[l], s5_a_im[l], s5_log_dt[l], s5_b_re[l], s5_b_im[l],
                                      s5_c_re[l], s5_c_im[l], s5_d[l])
    row = lambda a: a.reshape(1, -1).astype(F32)
    return {
        "norm_ffn1": row(norm_ffn1[l]), "norm_ffn2": row(norm_ffn2[l]), "norm_final": row(norm_final),
        "ffn1_w_gate": ffn1_w_gate[l].astype(BF16), "ffn1_w_up": ffn1_w_up[l].astype(BF16),
        "ffn1_w_down": ffn1_w_down[l].astype(BF16),
        "ffn2_w_gate": ffn2_w_gate[l].astype(BF16), "ffn2_w_up": ffn2_w_up[l].astype(BF16),
        "ffn2_w_down": ffn2_w_down[l].astype(BF16),
        "norm_mix": row(norm_mix[l]),
        "w_qk": w[:, :QK_COLS].astype(BF16),
        "w_vou": jnp.concatenate([w[:, QK_COLS:g0], w[:, g0 + N_GATES:]], axis=1).astype(BF16),
        "w_gr": w_g.T.astype(BF16),
        "conv_w": conv_w[l].astype(F32), "conv_b": row(conv_b[l]),
        "gate_bias_r": gate_bias.reshape(N_GATES, 1),
        "mlstm_norm_w": row(mlstm_norm_w[l]),
        "s5_m": m2, "s5_win": win2, "s5_wout": wout2, "s5_a": a2,
        "s5_w_glu": s5_w_glu[l].astype(BF16), "w_out": w_out[l].astype(BF16),
    }


def kernel(x_prompt, x_sample, norm_ffn1, ffn1_w_gate, ffn1_w_up, ffn1_w_down, norm_mix, w_in, conv_w, conv_b, b_igate, b_fgate, mlstm_norm_w, s5_a_re, s5_a_im, s5_log_dt, s5_b_re, s5_b_im, s5_c_re, s5_c_im, s5_d, s5_w_glu, w_out, norm_ffn2, ffn2_w_gate, ffn2_w_up, ffn2_w_down, norm_final):
    assert norm_ffn1.shape[0] == 1 and x_prompt.shape[0] == 1 and x_sample.shape[0] == 1
    p = _prepare(norm_ffn1, ffn1_w_gate, ffn1_w_up, ffn1_w_down, norm_mix, w_in, conv_w, conv_b, b_igate,
                 b_fgate, mlstm_norm_w, s5_a_re, s5_a_im, s5_log_dt, s5_b_re, s5_b_im, s5_c_re, s5_c_im, s5_d,
                 s5_w_glu, w_out, norm_ffn2, ffn2_w_gate, ffn2_w_up, ffn2_w_down, norm_final)
    y_prompt = _encode(x_prompt[0], p)[None]
    y_sample = _encode(x_sample[0], p)[None]
    return (y_prompt, y_sample)
```

```python
import functools
import math

import jax
import jax.numpy as jnp
from jax import lax
from jax.experimental import pallas as pl
from jax.experimental.pallas import tpu as pltpu

F32 = jnp.float32
BF16 = jnp.bfloat16

D_MODEL = 2048
D_MLSTM = 1024
D_S5 = 1024
N_HEADS = 8
HEAD_DIM = 128
CHUNK = 128
N_GATES = 32
S5_GROUPS = 64
S5_GC = 16
S5_P = 64
S5_BLK = 16
S5_PAIRS = S5_GROUPS // 2
D_FF = 5632
EPS = 1e-6
M_INIT = -1e30

VMEM_LIMIT_BYTES = 56 * 1024 * 1024

FFN_TM = 512
FFN_TF = 512
PROJ_TM = 256
PROJ_HALO = 8
MIX_TM = 256


def _sigmoid(x):
    return 1.0 / (1.0 + jnp.exp(-x))


def _rmsnorm(x, w):
    return x * lax.rsqrt(jnp.mean(x * x, axis=-1, keepdims=True) + EPS) * w


def _ffn_body(x_ref, nw_ref, wg_ref, wu_ref, wd_ref, nf_ref, o_ref, xn_ref, *, final_norm):
    j = pl.program_id(1)

    @pl.when(j == 0)
    def _():
        x = x_ref[...]
        xn_ref[...] = _rmsnorm(x, nw_ref[...]).astype(BF16)
        o_ref[...] = x

    xn = xn_ref[...]
    g = jnp.dot(xn, wg_ref[...], preferred_element_type=F32)
    u = jnp.dot(xn, wu_ref[...], preferred_element_type=F32)
    h = (0.5 * g * _sigmoid(g)) * u
    o_ref[...] += jnp.dot(h.astype(BF16), wd_ref[...], preferred_element_type=F32)

    if final_norm:
        @pl.when(j == pl.num_programs(1) - 1)
        def _():
            o_ref[...] = _rmsnorm(o_ref[...], nf_ref[...])


def _ffn(x, norm_w, w_gate, w_up, w_down, norm_final, *, final_norm):
    L = x.shape[0]
    assert L % FFN_TM == 0 and D_FF % FFN_TF == 0
    return pl.pallas_call(
        functools.partial(_ffn_body, final_norm=final_norm),
        grid=(L // FFN_TM, D_FF // FFN_TF),
        in_specs=[
            pl.BlockSpec((FFN_TM, D_MODEL), lambda i, j: (i, 0)),
            pl.BlockSpec((1, D_MODEL), lambda i, j: (0, 0)),
            pl.BlockSpec((D_MODEL, FFN_TF), lambda i, j: (0, j)),
            pl.BlockSpec((D_MODEL, FFN_TF), lambda i, j: (0, j)),
            pl.BlockSpec((FFN_TF, D_MODEL), lambda i, j: (j, 0)),
            pl.BlockSpec((1, D_MODEL), lambda i, j: (0, 0)),
        ],
        out_specs=pl.BlockSpec((FFN_TM, D_MODEL), lambda i, j: (i, 0)),
        out_shape=jax.ShapeDtypeStruct((L, D_MODEL), F32),
        scratch_shapes=[pltpu.VMEM((FFN_TM, D_MODEL), BF16)],
        compiler_params=pltpu.CompilerParams(
            dimension_semantics=("parallel", "arbitrary"), vmem_limit_bytes=VMEM_LIMIT_BYTES),
        name="ffn_final" if final_norm else "ffn",
    )(x, norm_w, w_gate, w_up, w_down, norm_final)


QK_COLS = 2 * D_MLSTM
QK_CB = 512


def _in_proj_body(xp_ref, x_ref, xnx_ref, nw_ref, wqk_ref, wvou_ref, wgr_ref,
                  cw_ref, cb_ref, q_ref, kt_ref, v_ref, o_ref, u_ref, gr_ref):
    i = pl.program_id(0)
    tm = x_ref.shape[0]
    nw = nw_ref[...]
    xn = _rmsnorm(x_ref[...], nw)
    xn_prev = jnp.where(i == 0, 0.0, _rmsnorm(xp_ref[...], nw))
    xn_next = jnp.where(i == pl.num_programs(0) - 1, 0.0, _rmsnorm(xnx_ref[...], nw))
    xn_b = xn.astype(BF16)
    xe_b = jnp.concatenate([xn_prev, xn, xn_next], axis=0).astype(BF16)
    rows = tm + 2 * PROJ_HALO

    for c in range(QK_COLS // QK_CB):
        cs = slice(c * QK_CB, (c + 1) * QK_CB)
        z = jnp.dot(xe_b, wqk_ref[:, cs], preferred_element_type=F32)
        z_m1 = pltpu.roll(z, 1, 0)[PROJ_HALO:PROJ_HALO + tm]
        z_0 = z[PROJ_HALO:PROJ_HALO + tm]
        z_p1 = pltpu.roll(z, rows - 1, 0)[PROJ_HALO:PROJ_HALO + tm]
        y = z_m1 * cw_ref[0:1, cs] + z_0 * cw_ref[1:2, cs] + z_p1 * cw_ref[2:3, cs] + cb_ref[:, cs]
        y = y * _sigmoid(y)
        if c * QK_CB < D_MLSTM:
            q_ref[:, cs] = (y * (HEAD_DIM ** -0.5)).astype(q_ref.dtype)
        else:
            kt_ref[c * QK_CB - D_MLSTM:(c + 1) * QK_CB - D_MLSTM, :] = y.T.astype(kt_ref.dtype)

    v_ref[...] = jnp.dot(xn_b, wvou_ref[:, 0:D_MLSTM], preferred_element_type=F32).astype(v_ref.dtype)
    o_ref[...] = jnp.dot(xn_b, wvou_ref[:, D_MLSTM:2 * D_MLSTM], preferred_element_type=F32)
    u_ref[...] = jnp.dot(xn_b, wvou_ref[:, 2 * D_MLSTM:], preferred_element_type=F32).astype(u_ref.dtype)
    gr_ref[...] = lax.dot_general(wgr_ref[...], xn_b, (((1,), (1,)), ((), ())), preferred_element_type=F32)


def _in_proj(x, norm_w, w_qk, w_vou, w_gr, conv_w, conv_b):
    L = x.shape[0]
    tm = PROJ_TM
    assert L % tm == 0
    hb = tm // PROJ_HALO
    nblk8 = L // PROJ_HALO
    whole = pl.BlockSpec(memory_space=pltpu.VMEM)
    return pl.pallas_call(
        _in_proj_body,
        grid=(L // tm,),
        in_specs=[
            pl.BlockSpec((PROJ_HALO, D_MODEL), lambda i: (jnp.maximum(i * hb - 1, 0), 0)),
            pl.BlockSpec((tm, D_MODEL), lambda i: (i, 0)),
            pl.BlockSpec((PROJ_HALO, D_MODEL), lambda i: (jnp.minimum((i + 1) * hb, nblk8 - 1), 0)),
            whole, whole, whole, whole, whole, whole,
        ],
        out_specs=[
            pl.BlockSpec((tm, D_MLSTM), lambda i: (i, 0)),
            pl.BlockSpec((D_MLSTM, tm), lambda i: (0, i)),
            pl.BlockSpec((tm, D_MLSTM), lambda i: (i, 0)),
            pl.BlockSpec((tm, D_MLSTM), lambda i: (i, 0)),
            pl.BlockSpec((tm, D_S5), lambda i: (i, 0)),
            pl.BlockSpec((N_GATES, tm), lambda i: (0, i)),
        ],
        out_shape=[
            jax.ShapeDtypeStruct((L, D_MLSTM), BF16),
            jax.ShapeDtypeStruct((D_MLSTM, L), BF16),
            jax.ShapeDtypeStruct((L, D_MLSTM), BF16),
            jax.ShapeDtypeStruct((L, D_MLSTM), F32),
            jax.ShapeDtypeStruct((L, D_S5), BF16),
            jax.ShapeDtypeStruct((N_GATES, L), F32),
        ],
        compiler_params=pltpu.CompilerParams(
            dimension_semantics=("parallel",), vmem_limit_bytes=VMEM_LIMIT_BYTES),
        name="in_proj",
    )(x, x, x, norm_w, w_qk, w_vou, w_gr, conv_w, conv_b)


def _bf16_split3(x):
    hi = x.astype(BF16)
    r1 = x - hi.astype(F32)
    mid = r1.astype(BF16)
    lo = (r1 - mid.astype(F32)).astype(BF16)
    return hi, mid, lo


def _lane_cummax(x, backward):
    n = x.shape[-1]
    lane = lax.broadcasted_iota(jnp.int32, x.shape, x.ndim - 1)
    sh = 1
    while sh < n:
        if backward:
            x = jnp.maximum(x, jnp.where(lane < n - sh, pltpu.roll(x, n - sh, x.ndim - 1), -jnp.inf))
        else:
            x = jnp.maximum(x, jnp.where(lane >= sh, pltpu.roll(x, sh, x.ndim - 1), -jnp.inf))
        sh *= 2
    return x


def _mlstm_direction(q_ref, kt_ref, v_ref, gr_ref, bias_r_ref, h_ref, ct_ref, m_ref, d):
    T = CHUNK
    H = N_HEADS
    backward = d == 1
    rr = lax.broadcasted_iota(jnp.int32, (T, T), 0)
    cc = lax.broadcasted_iota(jnp.int32, (T, T), 1)
    mask = (cc >= rr) if backward else (cc <= rr)
    tri = mask.astype(BF16)
    tri_r = ((rr >= cc) if backward else (rr <= cc)).astype(BF16)
    eye = (rr == cc).astype(BF16)
    nt_dims = (((1,), (1,)), ((), ()))
    last = 0 if backward else T - 1

    g = gr_ref[...] + bias_r_ref[...]
    i_rows = g[d * H:(d + 1) * H]
    f_rows = g[(2 + d) * H:(3 + d) * H]
    lf_rows = jnp.minimum(f_rows, 0.0) - jnp.log1p(jnp.exp(-jnp.abs(f_rows)))
    lf_parts = _bf16_split3(lf_rows)
    b_rows = sum(jnp.dot(part, tri_r, preferred_element_type=F32) for part in lf_parts)
    b_cols = sum(lax.dot_general(tri, part, nt_dims, preferred_element_type=F32) for part in lf_parts)
    r_rows = i_rows - b_rows
    m_prev = m_ref[d]
    big_m = jnp.maximum(_lane_cummax(r_rows, backward), m_prev)
    big_m_bf = big_m.astype(BF16)
    m_cols = lax.dot_general(eye, big_m_bf, nt_dims, preferred_element_type=F32)
    clamp_cols = -(b_cols + m_cols)
    m_last = jnp.broadcast_to(big_m[:, last:last + 1], (H, T))
    b_tot = jnp.broadcast_to(b_rows[:, last:last + 1], (H, T))
    sc_rows = jnp.exp(m_prev - m_last)
    wkk_rows = jnp.exp(r_rows - m_last)
    ones = jnp.ones((T, HEAD_DIM), BF16)

    for h in range(H):
        ci = d * H + h
        hs = slice(h * HEAD_DIM, (h + 1) * HEAD_DIM)
        q = q_ref[:, hs]
        kt = kt_ref[hs, :]
        vaug = jnp.concatenate([v_ref[:, hs], ones], axis=1)
        ct_prev = ct_ref[ci]
        m_col = jnp.broadcast_to(m_cols[:, h:h + 1], (T, T))
        w = jnp.exp(jnp.where(mask, r_rows[h:h + 1, :] - m_col, -jnp.inf))
        s = jnp.dot(q, kt, preferred_element_type=F32) * w
        sv = jnp.dot(s.astype(BF16), vaug, preferred_element_type=F32)
        qc = jnp.dot(q, ct_prev.astype(BF16), preferred_element_type=F32)
        s_inter = jnp.exp(m_prev[h:h + 1, :] - m_col)
        num = sv[:, :HEAD_DIM] + s_inter * qc[:, :HEAD_DIM]
        den = sv[:, HEAD_DIM:] + s_inter * qc[:, HEAD_DIM:]
        floor = jnp.exp(jnp.broadcast_to(clamp_cols[:, h:h + 1], (T, HEAD_DIM)))
        h_ref[:, hs] = num / jnp.maximum(jnp.abs(den), floor)

        kw = (kt.astype(F32) * wkk_rows[h:h + 1, :]).astype(BF16)
        upd = jnp.dot(kw, vaug, preferred_element_type=F32)
        sc = sc_rows[h:h + 1, :]
        ct_ref[ci] = jnp.concatenate([sc, sc], axis=1) * ct_prev + upd

    m_ref[d] = b_tot + m_last


def _mlstm_body(qf_ref, ktf_ref, vf_ref, grf_ref, qb_ref, ktb_ref, vb_ref, grb_ref, bias_r_ref,
                hf_ref, hb_ref, ct_ref, m_ref):
    @pl.when(pl.program_id(0) == 0)
    def _():
        ct_ref[...] = jnp.zeros_like(ct_ref)
        m_ref[...] = jnp.full_like(m_ref, M_INIT)

    _mlstm_direction(qf_ref, ktf_ref, vf_ref, grf_ref, bias_r_ref, hf_ref, ct_ref, m_ref, 0)
    _mlstm_direction(qb_ref, ktb_ref, vb_ref, grb_ref, bias_r_ref, hb_ref, ct_ref, m_ref, 1)


def _mlstm(q, kt, v, grow, bias_r):
    L = q.shape[0]
    T = CHUNK
    assert L % T == 0 and T == HEAD_DIM
    nc = L // T
    fwd = lambda c: (c, 0)
    bwd = lambda c: (nc - 1 - c, 0)
    fwd_r = lambda c: (0, c)
    bwd_r = lambda c: (0, nc - 1 - c)
    return pl.pallas_call(
        _mlstm_body,
        grid=(nc,),
        in_specs=[
            pl.BlockSpec((T, D_MLSTM), fwd), pl.BlockSpec((D_MLSTM, T), fwd_r),
            pl.BlockSpec((T, D_MLSTM), fwd), pl.BlockSpec((N_GATES, T), fwd_r),
            pl.BlockSpec((T, D_MLSTM), bwd), pl.BlockSpec((D_MLSTM, T), bwd_r),
            pl.BlockSpec((T, D_MLSTM), bwd), pl.BlockSpec((N_GATES, T), bwd_r),
            pl.BlockSpec((N_GATES, 1), lambda c: (0, 0)),
        ],
        out_specs=[pl.BlockSpec((T, D_MLSTM), fwd), pl.BlockSpec((T, D_MLSTM), bwd)],
        out_shape=[jax.ShapeDtypeStruct((L, D_MLSTM), F32), jax.ShapeDtypeStruct((L, D_MLSTM), F32)],
        scratch_shapes=[
            pltpu.VMEM((2 * N_HEADS, HEAD_DIM, 2 * HEAD_DIM), F32),
            pltpu.VMEM((2, N_HEADS, T), F32),
        ],
        compiler_params=pltpu.CompilerParams(
            dimension_semantics=("arbitrary",), vmem_limit_bytes=VMEM_LIMIT_BYTES),
        name="mlstm",
    )(q, kt, v, grow, q, kt, v, grow, bias_r)


S5_PAIR_CH = 2 * S5_GC
S5_ROW = S5_BLK * S5_PAIR_CH
S5_ST = 2 * S5_P
S5_TILE = 8
S5_LANE_PAIRS = 4
S5_TT = 2048
S5_NBT = S5_TT // S5_BLK


def _s5_scan(a, sin_ref, nb):
    R = S5_TILE
    ntile = nb // R
    row = lax.broadcasted_iota(jnp.int32, (R, S5_ST), 0)
    zero = jnp.zeros((1, S5_ST), F32)

    def cmul(x, y):
        return x[0] * y[0] - x[1] * y[1], x[0] * y[1] + x[1] * y[0]

    def bcast(x):
        return tuple(jnp.broadcast_to(t, (R, S5_ST)) for t in x)

    def tables(ar, ai, backward):
        pw = {1: (ar, ai)}
        for e in range(2, R + 1):
            pw[e] = cmul(pw[e // 2], pw[e - e // 2])
        steps = []
        for sh in (1, 2, 4):
            keep = (row < R - sh) if backward else (row >= sh)
            steps.append(tuple(jnp.where(keep, t, 0.0) for t in bcast(pw[sh])))
        order = range(R, 0, -1) if backward else range(1, R + 1)
        carry_pw = tuple(jnp.concatenate([pw[e][j] for e in order], axis=0) for j in (0, 1))
        return steps, carry_pw

    def scan_tile(x, carry, steps, cpw, backward):
        for sh, am in zip((1, 2, 4), steps):
            rs = (R - sh) if backward else sh
            x = tuple(p + q for p, q in zip(x, cmul(am, (pltpu.roll(x[0], rs, 0), pltpu.roll(x[1], rs, 0)))))
        cb = bcast(carry)
        x = tuple(p + q for p, q in zip(x, cmul(cpw, cb)))
        edge, rs = (R - 1, R - 1) if backward else (0, 1)
        enter = tuple(jnp.where(row == edge, c, pltpu.roll(t, rs, 0)) for t, c in zip(x, cb))
        last = 0 if backward else R - 1
        return enter, (x[0][last:last + 1], x[1][last:last + 1])

    tabs = []
    for p in range(S5_LANE_PAIRS):
        ap = a[p]
        tabs.append((tables(ap[0:1], ap[1:2], False), tables(ap[2:3], ap[3:4], True)))

    def step(i, carry):
        rf = pl.multiple_of(i * R, R)
        rb = pl.multiple_of((ntile - 1 - i) * R, R)
        out = []
        for p in range(S5_LANE_PAIRS):
            (steps_f, cpw_f), (steps_b, cpw_b) = tabs[p]
            cf, cb = carry[p]
            xf = (sin_ref[p, pl.ds(rf, R), 0:S5_ST], sin_ref[p, pl.ds(rf, R), S5_ST:2 * S5_ST])
            xb = (sin_ref[p, pl.ds(rb, R), 2 * S5_ST:3 * S5_ST], sin_ref[p, pl.ds(rb, R), 3 * S5_ST:4 * S5_ST])
            ef, cf = scan_tile(xf, cf, steps_f, cpw_f, False)
            eb, cb = scan_tile(xb, cb, steps_b, cpw_b, True)
            sin_ref[p, pl.ds(rf, R), 0:S5_ST] = ef[0]
            sin_ref[p, pl.ds(rf, R), S5_ST:2 * S5_ST] = ef[1]
            sin_ref[p, pl.ds(rb, R), 2 * S5_ST:3 * S5_ST] = eb[0]
            sin_ref[p, pl.ds(rb, R), 3 * S5_ST:4 * S5_ST] = eb[1]
            out.append((cf, cb))
        return tuple(out)

    init = tuple(((zero, zero), (zero, zero)) for _ in range(S5_LANE_PAIRS))
    lax.fori_loop(0, ntile, step, init)


def _s5_body(u_ref, m_ref, win_ref, wout_ref, a_ref, y_ref, tok_ref, u2_ref, sin_ref, y2_ref, stage_ref):
    phase = pl.program_id(1)
    t = pl.program_id(2)
    nb = u2_ref.shape[1]
    r0 = pl.multiple_of(t * S5_NBT, S5_NBT)
    rows = pl.ds(r0, S5_NBT)

    @pl.when(phase == 0)
    def _():
        tok_ref[...] = u_ref[...].astype(F32)
        for s in range(S5_BLK):
            tok_s = tok_ref[pl.ds(s, S5_NBT, stride=S5_BLK), :]
            for p in range(S5_LANE_PAIRS):
                u2_ref[p, rows, s * S5_PAIR_CH:(s + 1) * S5_PAIR_CH] = (
                    tok_s[:, p * S5_PAIR_CH:(p + 1) * S5_PAIR_CH].astype(BF16))
        for p in range(S5_LANE_PAIRS):
            sin_ref[p, rows, :] = jnp.dot(u2_ref[p, rows, :], win_ref[p], preferred_element_type=F32)

    @pl.when((phase == 0) & (t == pl.num_programs(2) - 1))
    def _():
        _s5_scan(a_ref[...], sin_ref, nb)

    @pl.when(phase == 1)
    def _():
        for p in range(S5_LANE_PAIRS):
            y2_ref[p] = (jnp.dot(u2_ref[p, rows, :], m_ref[p], preferred_element_type=F32)
                         + jnp.dot(sin_ref[p, rows, :].astype(BF16), wout_ref[p], preferred_element_type=F32))
        for s in range(S5_BLK):
            for p in range(S5_LANE_PAIRS):
                stage_ref[:, p * S5_PAIR_CH:(p + 1) * S5_PAIR_CH] = y2_ref[p, :, s * S5_PAIR_CH:(s + 1) * S5_PAIR_CH]
            y_ref[pl.ds(s, S5_NBT, stride=S5_BLK), :] = stage_ref[...]


def _s5(u, m2, win2, wout2, a2):
    L = u.shape[0]
    assert L % S5_TT == 0 and D_S5 == S5_PAIRS * S5_PAIR_CH
    nb = L // S5_BLK
    nq = S5_PAIRS // S5_LANE_PAIRS
    lane_tile = S5_LANE_PAIRS * S5_PAIR_CH
    wspec = pl.BlockSpec((S5_LANE_PAIRS, S5_ROW, S5_ROW), lambda q, ph, t: (q, 0, 0))
    return pl.pallas_call(
        _s5_body,
        grid=(nq, 2, L // S5_TT),
        in_specs=[
            pl.BlockSpec((S5_TT, lane_tile), lambda q, ph, t: (t, q)),
            wspec, wspec, wspec,
            pl.BlockSpec((S5_LANE_PAIRS, 4, S5_ST), lambda q, ph, t: (q, 0, 0)),
        ],
        out_specs=pl.BlockSpec((S5_TT, lane_tile), lambda q, ph, t: (t * ph, q)),
        out_shape=jax.ShapeDtypeStruct((L, D_S5), F32),
        scratch_shapes=[
            pltpu.VMEM((S5_TT, lane_tile), F32),
            pltpu.VMEM((S5_LANE_PAIRS, nb, S5_ROW), BF16),
            pltpu.VMEM((S5_LANE_PAIRS, nb, 4 * S5_ST), F32),
            pltpu.VMEM((S5_LANE_PAIRS, S5_NBT, S5_ROW), F32),
            pltpu.VMEM((S5_NBT, lane_tile), F32),
        ],
        compiler_params=pltpu.CompilerParams(
            dimension_semantics=("parallel", "arbitrary", "arbitrary"), vmem_limit_bytes=VMEM_LIMIT_BYTES),
        name="s5",
    )(u, m2, win2, wout2, a2)


def _s5_weights(a_re, a_im, log_dt, b_re, b_im, c_re, c_im, d_skip):
    nt, npair = S5_BLK, S5_PAIRS
    lam = lax.complex(a_re, a_im)
    dt = jnp.exp(log_dt)[..., None]
    lam_bar = jnp.exp(lam * dt)
    b_bar = ((lam_bar - 1.0) / lam)[..., None] * lax.complex(b_re, b_im)
    c_mat = lax.complex(c_re, c_im)
    taus = jnp.arange(nt + 1, dtype=F32)
    pw = jnp.exp((lam * dt)[..., None] * taus)
    eye2 = jnp.eye(2, dtype=F32)
    t_idx = jnp.arange(nt)

    bb = b_bar.reshape(2, npair, 2, S5_P, S5_GC)
    b2 = (bb[:, :, :, :, None, :] * eye2[None, None, :, None, :, None]).reshape(2, npair, S5_ST, S5_PAIR_CH)
    ct = c_mat.transpose(0, 1, 3, 2).reshape(2, npair, 2, S5_P, S5_GC)
    pw2 = pw.reshape(2, npair, 2, S5_P, nt + 1)

    def c_times_pw(z, expo):
        x = (pw2[z][..., expo][:, :, :, :, None, None] * ct[z][:, :, :, None, None, :]
             * eye2[None, :, None, None, :, None])
        return x.reshape(npair, S5_ST, S5_ROW)

    def kernel_rows(z, expo):
        x = c_times_pw(z, expo)
        b = b2[z]
        return (jnp.einsum('Pkc,Pkx->Pcx', jnp.real(b), jnp.real(x), precision='highest')
                - jnp.einsum('Pkc,Pkx->Pcx', jnp.imag(b), jnp.imag(x), precision='highest'))

    kf = kernel_rows(0, t_idx)
    kb = kernel_rows(1, nt - 1 - t_idx)
    w = S5_PAIR_CH
    rows = []
    for s in range(nt):
        f = jnp.pad(kf[..., :S5_ROW - w * s], ((0, 0), (0, 0), (w * s, 0)))
        b = jnp.pad(kb[..., w * (nt - 1 - s):], ((0, 0), (0, 0), (0, w * (nt - 1 - s))))
        rows.append(f + b)
    d2 = jnp.tile(d_skip.reshape(npair, S5_PAIR_CH), (1, nt))
    m2 = jnp.stack(rows, axis=1).reshape(npair, S5_ROW, S5_ROW) + jnp.eye(S5_ROW, dtype=F32) * d2[:, None, :]

    def state_in(z, expo):
        pws = pw2[z][..., expo].transpose(0, 3, 1, 2).reshape(npair, nt, 1, S5_ST)
        return (b2[z].transpose(0, 2, 1)[:, None] * pws).reshape(npair, S5_ROW, S5_ST)

    win_f, win_b = state_in(0, nt - 1 - t_idx), state_in(1, t_idx)
    win2 = jnp.concatenate([jnp.real(win_f), jnp.imag(win_f), jnp.real(win_b), jnp.imag(win_b)], axis=2)
    wo_f, wo_b = c_times_pw(0, t_idx + 1), c_times_pw(1, nt - t_idx)
    wout2 = jnp.concatenate([jnp.real(wo_f), -jnp.imag(wo_f), jnp.real(wo_b), -jnp.imag(wo_b)], axis=1)
    a_blk = pw2[..., nt].reshape(2, npair, S5_ST)
    a2 = jnp.stack([jnp.real(a_blk[0]), jnp.imag(a_blk[0]), jnp.real(a_blk[1]), jnp.imag(a_blk[1])], axis=1)
    return m2.astype(BF16), win2.astype(BF16), wout2.astype(BF16), a2


def _mix_body(x_ref, hf_ref, hb_ref, o_ref, y_ref, nw_ref, wglu_ref, wout_ref, out_ref):
    h = hf_ref[...] + hb_ref[...]
    parts = []
    for hd in range(N_HEADS):
        hh = h[:, hd * HEAD_DIM:(hd + 1) * HEAD_DIM]
        mu = jnp.mean(hh, axis=-1, keepdims=True)
        var = jnp.mean(jnp.square(hh - mu), axis=-1, keepdims=True)
        parts.append((hh - mu) * lax.rsqrt(var + EPS))
    hn = jnp.concatenate(parts, axis=1)
    h_m = hn * nw_ref[...] * _sigmoid(o_ref[...])
    y = y_ref[...]
    gelu = 0.5 * y * (1.0 + jnp.tanh(math.sqrt(2.0 / math.pi) * (y + 0.044715 * (y * y * y))))
    ab = jnp.dot(gelu.astype(BF16), wglu_ref[...], preferred_element_type=F32)
    h_s = ab[:, :D_S5] * _sigmoid(ab[:, D_S5:])
    mix = jnp.dot(h_m.astype(BF16), wout_ref[0:D_MLSTM, :], preferred_element_type=F32)
    mix += jnp.dot(h_s.astype(BF16), wout_ref[D_MLSTM:, :], preferred_element_type=F32)
    out_ref[...] = x_ref[...] + mix


def _mix(x, hf, hb, o_in, y, norm_w, w_glu, w_out):
    L = x.shape[0]
    tm = MIX_TM
    assert L % tm == 0
    whole = pl.BlockSpec(memory_space=pltpu.VMEM)
    row = lambda n: pl.BlockSpec((tm, n), lambda i: (i, 0))
    return pl.pallas_call(
        _mix_body,
        grid=(L // tm,),
        in_specs=[row(D_MODEL), row(D_MLSTM), row(D_MLSTM), row(D_MLSTM), row(D_S5), whole, whole, whole],
        out_specs=row(D_MODEL),
        out_shape=jax.ShapeDtypeStruct((L, D_MODEL), F32),
        compiler_params=pltpu.CompilerParams(
            dimension_semantics=("parallel",), vmem_limit_bytes=VMEM_LIMIT_BYTES),
        name="mix",
    )(x, hf, hb, o_in, y, norm_w, w_glu, w_out)


def _encode(x, p):
    x = _ffn(x, p["norm_ffn1"], p["ffn1_w_gate"], p["ffn1_w_up"], p["ffn1_w_down"], p["norm_final"],
             final_norm=False)
    q, kt, v, o_in, u, grow = _in_proj(x, p["norm_mix"], p["w_qk"], p["w_vou"], p["w_gr"], p["conv_w"], p["conv_b"])
    hf, hb = _mlstm(q, kt, v, grow, p["gate_bias_r"])
    y = _s5(u, p["s5_m"], p["s5_win"], p["s5_wout"], p["s5_a"])
    x = _mix(x, hf, hb, o_in, y, p["mlstm_norm_w"], p["s5_w_glu"], p["w_out"])
    return _ffn(x, p["norm_ffn2"], p["ffn2_w_gate"], p["ffn2_w_up"], p["ffn2_w_down"], p["norm_final"],
                final_norm=True)


def _prepare(norm_ffn1, ffn1_w_gate, ffn1_w_up, ffn1_w_down, norm_mix, w_in, conv_w, conv_b, b_igate, b_fgate,
             mlstm_norm_w, s5_a_re, s5_a_im, s5_log_dt, s5_b_re, s5_b_im, s5_c_re, s5_c_im, s5_d, s5_w_glu,
             w_out, norm_ffn2, ffn2_w_gate, ffn2_w_up, ffn2_w_down, norm_final):
    l = 0
    w = w_in[l]
    g0 = 4 * D_MLSTM
    w_g = w[:, g0:g0 + N_GATES]
    gate_bias = jnp.concatenate([b_igate[l].reshape(-1), b_fgate[l].reshape(-1)])
    m2, win2, wout2, a2 = _s5_weights(s5_a_re[l], s5_a_im[l], s5_log_dt[l], s5_b_re[l], s5_b_im[l],
                                      s5_c_re[l], s5_c_im[l], s5_d[l])
    row = lambda a: a.reshape(1, -1).astype(F32)
    return {
        "norm_ffn1": row(norm_ffn1[l]), "norm_ffn2": row(norm_ffn2[l]), "norm_final": row(norm_final),
        "ffn1_w_gate": ffn1_w_gate[l].astype(BF16), "ffn1_w_up": ffn1_w_up[l].astype(BF16),
        "ffn1_w_down": ffn1_w_down[l].astype(BF16),
        "ffn2_w_gate": ffn2_w_gate[l].astype(BF16), "ffn2_w_up": ffn2_w_up[l].astype(BF16),
        "ffn2_w_down": ffn2_w_down[l].astype(BF16),
        "norm_mix": row(norm_mix[l]),
        "w_qk": w[:, :QK_COLS].astype(BF16),
        "w_vou": jnp.concatenate([w[:, QK_COLS:g0], w[:, g0 + N_GATES:]], axis=1).astype(BF16),
        "w_gr": w_g.T.astype(BF16),
        "conv_w": conv_w[l].astype(F32), "conv_b": row(conv_b[l]),
        "gate_bias_r": gate_bias.reshape(N_GATES, 1),
        "mlstm_norm_w": row(mlstm_norm_w[l]),
        "s5_m": m2, "s5_win": win2, "s5_wout": wout2, "s5_a": a2,
        "s5_w_glu": s5_w_glu[l].astype(BF16), "w_out": w_out[l].astype(BF16),
    }


def kernel(x_prompt, x_sample, norm_ffn1, ffn1_w_gate, ffn1_w_up, ffn1_w_down, norm_mix, w_in, conv_w, conv_b, b_igate, b_fgate, mlstm_norm_w, s5_a_re, s5_a_im, s5_log_dt, s5_b_re, s5_b_im, s5_c_re, s5_c_im, s5_d, s5_w_glu, w_out, norm_ffn2, ffn2_w_gate, ffn2_w_up, ffn2_w_down, norm_final):
    assert norm_ffn1.shape[0] == 1 and x_prompt.shape[0] == 1 and x_sample.shape[0] == 1
    p = _prepare(norm_ffn1, ffn1_w_gate, ffn1_w_up, ffn1_w_down, norm_mix, w_in, conv_w, conv_b, b_igate,
                 b_fgate, mlstm_norm_w, s5_a_re, s5_a_im, s5_log_dt, s5_b_re, s5_b_im, s5_c_re, s5_c_im, s5_d,
                 s5_w_glu, w_out, norm_ffn2, ffn2_w_gate, ffn2_w_up, ffn2_w_down, norm_final)
    y_prompt = _encode(x_prompt[0], p)[None]
    y_sample = _encode(x_sample[0], p)[None]
    return (y_prompt, y_sample)
```

```python
import functools
import math

import jax
import jax.numpy as jnp
from jax import lax
from jax.experimental import pallas as pl
from jax.experimental.pallas import tpu as pltpu

F32 = jnp.float32
BF16 = jnp.bfloat16

D_MODEL = 2048
D_MLSTM = 1024
D_S5 = 1024
N_HEADS = 8
HEAD_DIM = 128
CHUNK = 128
N_GATES = 32
GATE_COL_LANES = 128
S5_GROUPS = 64
S5_GC = 16
S5_P = 64
S5_BLK = 16
S5_PAIRS = S5_GROUPS // 2
D_FF = 5632
EPS = 1e-6
M_INIT = -1e30

VMEM_LIMIT_BYTES = 56 * 1024 * 1024

FFN_TM = 512
FFN_TF = 512
PROJ_TM = 256
PROJ_HALO = 8
MIX_TM = 256


def _sigmoid(x):
    return 1.0 / (1.0 + jnp.exp(-x))


def _rmsnorm(x, w):
    return x * lax.rsqrt(jnp.mean(x * x, axis=-1, keepdims=True) + EPS) * w


def _ffn_body(x_ref, nw_ref, wg_ref, wu_ref, wd_ref, nf_ref, o_ref, xn_ref, *, final_norm):
    j = pl.program_id(1)

    @pl.when(j == 0)
    def _():
        x = x_ref[...]
        xn_ref[...] = _rmsnorm(x, nw_ref[...]).astype(BF16)
        o_ref[...] = x

    xn = xn_ref[...]
    g = jnp.dot(xn, wg_ref[...], preferred_element_type=F32)
    u = jnp.dot(xn, wu_ref[...], preferred_element_type=F32)
    h = (0.5 * g * _sigmoid(g)) * u
    o_ref[...] += jnp.dot(h.astype(BF16), wd_ref[...], preferred_element_type=F32)

    if final_norm:
        @pl.when(j == pl.num_programs(1) - 1)
        def _():
            o_ref[...] = _rmsnorm(o_ref[...], nf_ref[...])


def _ffn(x, norm_w, w_gate, w_up, w_down, norm_final, *, final_norm):
    L = x.shape[0]
    assert L % FFN_TM == 0 and D_FF % FFN_TF == 0
    return pl.pallas_call(
        functools.partial(_ffn_body, final_norm=final_norm),
        grid=(L // FFN_TM, D_FF // FFN_TF),
        in_specs=[
            pl.BlockSpec((FFN_TM, D_MODEL), lambda i, j: (i, 0)),
            pl.BlockSpec((1, D_MODEL), lambda i, j: (0, 0)),
            pl.BlockSpec((D_MODEL, FFN_TF), lambda i, j: (0, j)),
            pl.BlockSpec((D_MODEL, FFN_TF), lambda i, j: (0, j)),
            pl.BlockSpec((FFN_TF, D_MODEL), lambda i, j: (j, 0)),
            pl.BlockSpec((1, D_MODEL), lambda i, j: (0, 0)),
        ],
        out_specs=pl.BlockSpec((FFN_TM, D_MODEL), lambda i, j: (i, 0)),
        out_shape=jax.ShapeDtypeStruct((L, D_MODEL), F32),
        scratch_shapes=[pltpu.VMEM((FFN_TM, D_MODEL), BF16)],
        compiler_params=pltpu.CompilerParams(
            dimension_semantics=("parallel", "arbitrary"), vmem_limit_bytes=VMEM_LIMIT_BYTES),
        name="ffn_final" if final_norm else "ffn",
    )(x, norm_w, w_gate, w_up, w_down, norm_final)


QK_COLS = 2 * D_MLSTM
QK_CB = 512


def _chunk_scan(x, op, identity, backward):
    n = x.shape[-1]
    pos = lax.broadcasted_iota(jnp.int32, x.shape, x.ndim - 1) % CHUNK
    sh = 1
    while sh < CHUNK:
        if backward:
            x = op(x, jnp.where(pos < CHUNK - sh, pltpu.roll(x, n - sh, x.ndim - 1), identity))
        else:
            x = op(x, jnp.where(pos >= sh, pltpu.roll(x, sh, x.ndim - 1), identity))
        sh *= 2
    return x


def _in_proj_body(xp_ref, x_ref, xnx_ref, nw_ref, wqk_ref, wvou_ref, wgr_ref, gb_ref,
                  cw_ref, cb_ref, q_ref, kt_ref, v_ref, o_ref, u_ref, rb_ref, col_ref):
    i = pl.program_id(0)
    tm = x_ref.shape[0]
    nw = nw_ref[...]
    xn = _rmsnorm(x_ref[...], nw)
    xn_prev = jnp.where(i == 0, 0.0, _rmsnorm(xp_ref[...], nw))
    xn_next = jnp.where(i == pl.num_programs(0) - 1, 0.0, _rmsnorm(xnx_ref[...], nw))
    xn_b = xn.astype(BF16)
    xe_b = jnp.concatenate([xn_prev, xn, xn_next], axis=0).astype(BF16)
    rows = tm + 2 * PROJ_HALO

    g = lax.dot_general(wgr_ref[...], xn_b, (((1,), (1,)), ((), ())), preferred_element_type=F32) + gb_ref[...]
    H = N_HEADS
    f_pre = g[2 * H:]
    lf = jnp.minimum(f_pre, 0.0) - jnp.log1p(jnp.exp(-jnp.abs(f_pre)))
    b_f = _chunk_scan(lf[:H], jnp.add, 0.0, False)
    b_b = _chunk_scan(lf[H:], jnp.add, 0.0, True)
    r_f = g[:H] - b_f
    r_b = g[H:2 * H] - b_b
    cm_f = _chunk_scan(r_f, jnp.maximum, -jnp.inf, False)
    cm_b = _chunk_scan(r_b, jnp.maximum, -jnp.inf, True)

    for c in range(QK_COLS // QK_CB):
        cs = slice(c * QK_CB, (c + 1) * QK_CB)
        z = jnp.dot(xe_b, wqk_ref[:, cs], preferred_element_type=F32)
        z_m1 = pltpu.roll(z, 1, 0)[PROJ_HALO:PROJ_HALO + tm]
        z_0 = z[PROJ_HALO:PROJ_HALO + tm]
        z_p1 = pltpu.roll(z, rows - 1, 0)[PROJ_HALO:PROJ_HALO + tm]
        y = z_m1 * cw_ref[0:1, cs] + z_0 * cw_ref[1:2, cs] + z_p1 * cw_ref[2:3, cs] + cb_ref[:, cs]
        y = y * _sigmoid(y)
        if c * QK_CB < D_MLSTM:
            q_ref[:, cs] = (y * (HEAD_DIM ** -0.5)).astype(q_ref.dtype)
        else:
            kt_ref[c * QK_CB - D_MLSTM:(c + 1) * QK_CB - D_MLSTM, :] = y.T.astype(kt_ref.dtype)

    v_ref[...] = jnp.dot(xn_b, wvou_ref[:, 0:D_MLSTM], preferred_element_type=F32).astype(v_ref.dtype)
    o_ref[...] = jnp.dot(xn_b, wvou_ref[:, D_MLSTM:2 * D_MLSTM], preferred_element_type=F32)
    u_ref[...] = jnp.dot(xn_b, wvou_ref[:, 2 * D_MLSTM:], preferred_element_type=F32).astype(u_ref.dtype)
    rb_ref[...] = jnp.concatenate([r_f, r_b, b_f, b_b], axis=0)
    pad = jnp.zeros((GATE_COL_LANES - 4 * H, tm), F32)
    col_ref[...] = jnp.concatenate([b_f, b_b, cm_f, cm_b, pad], axis=0).T


def _in_proj(x, norm_w, w_qk, w_vou, w_gr, gate_bias, conv_w, conv_b):
    L = x.shape[0]
    tm = PROJ_TM
    assert L % tm == 0 and tm % CHUNK == 0
    hb = tm // PROJ_HALO
    nblk8 = L // PROJ_HALO
    whole = pl.BlockSpec(memory_space=pltpu.VMEM)
    return pl.pallas_call(
        _in_proj_body,
        grid=(L // tm,),
        in_specs=[
            pl.BlockSpec((PROJ_HALO, D_MODEL), lambda i: (jnp.maximum(i * hb - 1, 0), 0)),
            pl.BlockSpec((tm, D_MODEL), lambda i: (i, 0)),
            pl.BlockSpec((PROJ_HALO, D_MODEL), lambda i: (jnp.minimum((i + 1) * hb, nblk8 - 1), 0)),
            whole, whole, whole, whole, whole, whole, whole,
        ],
        out_specs=[
            pl.BlockSpec((tm, D_MLSTM), lambda i: (i, 0)),
            pl.BlockSpec((D_MLSTM, tm), lambda i: (0, i)),
            pl.BlockSpec((tm, D_MLSTM), lambda i: (i, 0)),
            pl.BlockSpec((tm, D_MLSTM), lambda i: (i, 0)),
            pl.BlockSpec((tm, D_S5), lambda i: (i, 0)),
            pl.BlockSpec((N_GATES, tm), lambda i: (0, i)),
            pl.BlockSpec((tm, GATE_COL_LANES), lambda i: (i, 0)),
        ],
        out_shape=[
            jax.ShapeDtypeStruct((L, D_MLSTM), BF16),
            jax.ShapeDtypeStruct((D_MLSTM, L), BF16),
            jax.ShapeDtypeStruct((L, D_MLSTM), BF16),
            jax.ShapeDtypeStruct((L, D_MLSTM), F32),
            jax.ShapeDtypeStruct((L, D_S5), BF16),
            jax.ShapeDtypeStruct((N_GATES, L), F32),
            jax.ShapeDtypeStruct((L, GATE_COL_LANES), F32),
        ],
        compiler_params=pltpu.CompilerParams(
            dimension_semantics=("parallel",), vmem_limit_bytes=VMEM_LIMIT_BYTES),
        name="in_proj",
    )(x, x, x, norm_w, w_qk, w_vou, w_gr, gate_bias, conv_w, conv_b)


def _bf16_split3(x):
    hi = x.astype(BF16)
    r1 = x - hi.astype(F32)
    mid = r1.astype(BF16)
    lo = (r1 - mid.astype(F32)).astype(BF16)
    return hi, mid, lo


def _mlstm_direction(q_ref, kt_ref, v_ref, rb_ref, col_ref, h_ref, ct_ref, m_ref, ml_ref, d):
    T = CHUNK
    H = N_HEADS
    backward = d == 1
    rr = lax.broadcasted_iota(jnp.int32, (T, T), 0)
    cc = lax.broadcasted_iota(jnp.int32, (T, T), 1)
    mask = (cc >= rr) if backward else (cc <= rr)
    last = 0 if backward else T - 1

    r_rows = rb_ref[d * H:(d + 1) * H, :]
    b_rows = rb_ref[(2 + d) * H:(3 + d) * H, :]
    b_cols = col_ref[:, d * H:(d + 1) * H]
    cm_cols = col_ref[:, (2 + d) * H:(3 + d) * H]
    m_prev = m_ref[d]
    m_prev_l = ml_ref[d]
    m_cols = jnp.maximum(cm_cols, m_prev_l)
    clamp_cols = -(b_cols + m_cols)
    m_last = jnp.maximum(jnp.max(r_rows, axis=1, keepdims=True), m_prev)
    b_tot = jnp.broadcast_to(b_rows[:, last:last + 1], (H, T))
    sc_rows = jnp.exp(m_prev - m_last)
    wkk_rows = jnp.exp(r_rows - m_last)
    ones = jnp.ones((T, HEAD_DIM), BF16)

    for h in range(H):
        ci = d * H + h
        hs = slice(h * HEAD_DIM, (h + 1) * HEAD_DIM)
        q = q_ref[:, hs]
        kt = kt_ref[hs, :]
        vaug = jnp.concatenate([v_ref[:, hs], ones], axis=1)
        ct_prev = ct_ref[ci]
        m_col = jnp.broadcast_to(m_cols[:, h:h + 1], (T, T))
        w = jnp.exp(jnp.where(mask, r_rows[h:h + 1, :] - m_col, -jnp.inf))
        s = jnp.dot(q, kt, preferred_element_type=F32) * w
        sv = jnp.dot(s.astype(BF16), vaug, preferred_element_type=F32)
        qc = jnp.dot(q, ct_prev.astype(BF16), preferred_element_type=F32)
        s_inter = jnp.exp(m_prev[h:h + 1, :] - m_col)
        num = sv[:, :HEAD_DIM] + s_inter * qc[:, :HEAD_DIM]
        den = sv[:, HEAD_DIM:] + s_inter * qc[:, HEAD_DIM:]
        floor = jnp.exp(jnp.broadcast_to(clamp_cols[:, h:h + 1], (T, HEAD_DIM)))
        h_ref[:, hs] = num / jnp.maximum(jnp.abs(den), floor)

        kw = (kt.astype(F32) * wkk_rows[h:h + 1, :]).astype(BF16)
        upd = jnp.dot(kw, vaug, preferred_element_type=F32)
        sc = sc_rows[h:h + 1, :]
        ct_ref[ci] = jnp.concatenate([sc, sc], axis=1) * ct_prev + upd

    m_ref[d] = b_tot + m_last
    ml_ref[d] = b_cols[last:last + 1, :] + m_cols[last:last + 1, :]


def _mlstm_body(qf_ref, ktf_ref, vf_ref, rbf_ref, colf_ref, qb_ref, ktb_ref, vb_ref, rbb_ref, colb_ref,
                hf_ref, hb_ref, ct_ref, m_ref, ml_ref):
    @pl.when(pl.program_id(0) == 0)
    def _():
        ct_ref[...] = jnp.zeros_like(ct_ref)
        m_ref[...] = jnp.full_like(m_ref, M_INIT)
        ml_ref[...] = jnp.full_like(ml_ref, M_INIT)

    _mlstm_direction(qf_ref, ktf_ref, vf_ref, rbf_ref, colf_ref, hf_ref, ct_ref, m_ref, ml_ref, 0)
    _mlstm_direction(qb_ref, ktb_ref, vb_ref, rbb_ref, colb_ref, hb_ref, ct_ref, m_ref, ml_ref, 1)


def _mlstm(q, kt, v, rb, col):
    L = q.shape[0]
    T = CHUNK
    assert L % T == 0 and T == HEAD_DIM
    nc = L // T
    fwd = lambda c: (c, 0)
    bwd = lambda c: (nc - 1 - c, 0)
    fwd_r = lambda c: (0, c)
    bwd_r = lambda c: (0, nc - 1 - c)
    return pl.pallas_call(
        _mlstm_body,
        grid=(nc,),
        in_specs=[
            pl.BlockSpec((T, D_MLSTM), fwd), pl.BlockSpec((D_MLSTM, T), fwd_r), pl.BlockSpec((T, D_MLSTM), fwd),
            pl.BlockSpec((N_GATES, T), fwd_r), pl.BlockSpec((T, GATE_COL_LANES), fwd),
            pl.BlockSpec((T, D_MLSTM), bwd), pl.BlockSpec((D_MLSTM, T), bwd_r), pl.BlockSpec((T, D_MLSTM), bwd),
            pl.BlockSpec((N_GATES, T), bwd_r), pl.BlockSpec((T, GATE_COL_LANES), bwd),
        ],
        out_specs=[pl.BlockSpec((T, D_MLSTM), fwd), pl.BlockSpec((T, D_MLSTM), bwd)],
        out_shape=[jax.ShapeDtypeStruct((L, D_MLSTM), F32), jax.ShapeDtypeStruct((L, D_MLSTM), F32)],
        scratch_shapes=[
            pltpu.VMEM((2 * N_HEADS, HEAD_DIM, 2 * HEAD_DIM), F32),
            pltpu.VMEM((2, N_HEADS, T), F32),
            pltpu.VMEM((2, 1, N_HEADS), F32),
        ],
        compiler_params=pltpu.CompilerParams(
            dimension_semantics=("arbitrary",), vmem_limit_bytes=VMEM_LIMIT_BYTES),
        name="mlstm",
    )(q, kt, v, rb, col, q, kt, v, rb, col)


S5_PAIR_CH = 2 * S5_GC
S5_ROW = S5_BLK * S5_PAIR_CH
S5_ST = 2 * S5_P
S5_TILE = 8
S5_LANE_PAIRS = 4
S5_TT = 2048
S5_NBT = S5_TT // S5_BLK


def _s5_scan(a, sin_ref, nb):
    R = S5_TILE
    ntile = nb // R
    row = lax.broadcasted_iota(jnp.int32, (R, S5_ST), 0)
    zero = jnp.zeros((1, S5_ST), F32)

    def cmul(x, y):
        return x[0] * y[0] - x[1] * y[1], x[0] * y[1] + x[1] * y[0]

    def bcast(x):
        return tuple(jnp.broadcast_to(t, (R, S5_ST)) for t in x)

    def tables(ar, ai, backward):
        pw = {1: (ar, ai)}
        for e in range(2, R + 1):
            pw[e] = cmul(pw[e // 2], pw[e - e // 2])
        steps = []
        for sh in (1, 2, 4):
            keep = (row < R - sh) if backward else (row >= sh)
            steps.append(tuple(jnp.where(keep, t, 0.0) for t in bcast(pw[sh])))
        order = range(R, 0, -1) if backward else range(1, R + 1)
        carry_pw = tuple(jnp.concatenate([pw[e][j] for e in order], axis=0) for j in (0, 1))
        return steps, carry_pw

    def scan_tile(x, carry, steps, cpw, backward):
        for sh, am in zip((1, 2, 4), steps):
            rs = (R - sh) if backward else sh
            x = tuple(p + q for p, q in zip(x, cmul(am, (pltpu.roll(x[0], rs, 0), pltpu.roll(x[1], rs, 0)))))
        cb = bcast(carry)
        x = tuple(p + q for p, q in zip(x, cmul(cpw, cb)))
        edge, rs = (R - 1, R - 1) if backward else (0, 1)
        enter = tuple(jnp.where(row == edge, c, pltpu.roll(t, rs, 0)) for t, c in zip(x, cb))
        last = 0 if backward else R - 1
        return enter, (x[0][last:last + 1], x[1][last:last + 1])

    tabs = []
    for p in range(S5_LANE_PAIRS):
        ap = a[p]
        tabs.append((tables(ap[0:1], ap[1:2], False), tables(ap[2:3], ap[3:4], True)))

    def step(i, carry):
        rf = pl.multiple_of(i * R, R)
        rb = pl.multiple_of((ntile - 1 - i) * R, R)
        out = []
        for p in range(S5_LANE_PAIRS):
            (steps_f, cpw_f), (steps_b, cpw_b) = tabs[p]
            cf, cb = carry[p]
            xf = (sin_ref[p, pl.ds(rf, R), 0:S5_ST], sin_ref[p, pl.ds(rf, R), S5_ST:2 * S5_ST])
            xb = (sin_ref[p, pl.ds(rb, R), 2 * S5_ST:3 * S5_ST], sin_ref[p, pl.ds(rb, R), 3 * S5_ST:4 * S5_ST])
            ef, cf = scan_tile(xf, cf, steps_f, cpw_f, False)
            eb, cb = scan_tile(xb, cb, steps_b, cpw_b, True)
            sin_ref[p, pl.ds(rf, R), 0:S5_ST] = ef[0]
            sin_ref[p, pl.ds(rf, R), S5_ST:2 * S5_ST] = ef[1]
            sin_ref[p, pl.ds(rb, R), 2 * S5_ST:3 * S5_ST] = eb[0]
            sin_ref[p, pl.ds(rb, R), 3 * S5_ST:4 * S5_ST] = eb[1]
            out.append((cf, cb))
        return tuple(out)

    init = tuple(((zero, zero), (zero, zero)) for _ in range(S5_LANE_PAIRS))
    lax.fori_loop(0, ntile, step, init)


def _s5_body(u_ref, m_ref, win_ref, wout_ref, a_ref, y_ref, tok_ref, u2_ref, sin_ref, y2_ref, stage_ref):
    phase = pl.program_id(1)
    t = pl.program_id(2)
    nb = u2_ref.shape[1]
    r0 = pl.multiple_of(t * S5_NBT, S5_NBT)
    rows = pl.ds(r0, S5_NBT)

    @pl.when(phase == 0)
    def _():
        tok_ref[...] = u_ref[...].astype(F32)
        for s in range(S5_BLK):
            tok_s = tok_ref[pl.ds(s, S5_NBT, stride=S5_BLK), :]
            for p in range(S5_LANE_PAIRS):
                u2_ref[p, rows, s * S5_PAIR_CH:(s + 1) * S5_PAIR_CH] = (
                    tok_s[:, p * S5_PAIR_CH:(p + 1) * S5_PAIR_CH].astype(BF16))
        for p in range(S5_LANE_PAIRS):
            sin_ref[p, rows, :] = jnp.dot(u2_ref[p, rows, :], win_ref[p], preferred_element_type=F32)

    @pl.when((phase == 0) & (t == pl.num_programs(2) - 1))
    def _():
        _s5_scan(a_ref[...], sin_ref, nb)

    @pl.when(phase == 1)
    def _():
        for p in range(S5_LANE_PAIRS):
            y2_ref[p] = (jnp.dot(u2_ref[p, rows, :], m_ref[p], preferred_element_type=F32)
                         + jnp.dot(sin_ref[p, rows, :].astype(BF16), wout_ref[p], preferred_element_type=F32))
        for s in range(S5_BLK):
            for p in range(S5_LANE_PAIRS):
                stage_ref[:, p * S5_PAIR_CH:(p + 1) * S5_PAIR_CH] = y2_ref[p, :, s * S5_PAIR_CH:(s + 1) * S5_PAIR_CH]
            y_ref[pl.ds(s, S5_NBT, stride=S5_BLK), :] = stage_ref[...]


def _s5(u, m2, win2, wout2, a2):
    L = u.shape[0]
    assert L % S5_TT == 0 and D_S5 == S5_PAIRS * S5_PAIR_CH
    nb = L // S5_BLK
    nq = S5_PAIRS // S5_LANE_PAIRS
    lane_tile = S5_LANE_PAIRS * S5_PAIR_CH
    wspec = pl.BlockSpec((S5_LANE_PAIRS, S5_ROW, S5_ROW), lambda q, ph, t: (q, 0, 0))
    return pl.pallas_call(
        _s5_body,
        grid=(nq, 2, L // S5_TT),
        in_specs=[
            pl.BlockSpec((S5_TT, lane_tile), lambda q, ph, t: (t, q)),
            wspec, wspec, wspec,
            pl.BlockSpec((S5_LANE_PAIRS, 4, S5_ST), lambda q, ph, t: (q, 0, 0)),
        ],
        out_specs=pl.BlockSpec((S5_TT, lane_tile), lambda q, ph, t: (t * ph, q)),
        out_shape=jax.ShapeDtypeStruct((L, D_S5), F32),
        scratch_shapes=[
            pltpu.VMEM((S5_TT, lane_tile), F32),
            pltpu.VMEM((S5_LANE_PAIRS, nb, S5_ROW), BF16),
            pltpu.VMEM((S5_LANE_PAIRS, nb, 4 * S5_ST), F32),
            pltpu.VMEM((S5_LANE_PAIRS, S5_NBT, S5_ROW), F32),
            pltpu.VMEM((S5_NBT, lane_tile), F32),
        ],
        compiler_params=pltpu.CompilerParams(
            dimension_semantics=("parallel", "arbitrary", "arbitrary"), vmem_limit_bytes=VMEM_LIMIT_BYTES),
        name="s5",
    )(u, m2, win2, wout2, a2)


S5_EXP = (S5_BLK + 1) * S5_PAIR_CH
S5_EXP_PAD = 640
S5_PWT_ROWS = 24


def _s5_prep_body(pwk_ref, pwt_ref, ct_ref, b2t_ref, d_ref, m_ref, win_ref, wout_ref):
    nt, w = S5_BLK, S5_PAIR_CH
    lane = lax.broadcasted_iota(jnp.int32, (S5_ST, S5_EXP_PAD), 1)
    row = lax.broadcasted_iota(jnp.int32, (S5_ST, S5_EXP_PAD), 0)
    sel_slot = (lane // w == row).astype(BF16)
    sel_chan = ((lane % S5_GC == row) & (row < S5_GC)).astype(BF16)
    same_group = (row // S5_P == (lane // S5_GC) % 2) & (lane < S5_EXP)

    def expand(x, sel):
        return sum(jnp.dot(p, sel, preferred_element_type=F32) for p in _bf16_split3(x))

    def split_dot(a, x):
        a_hi, a_lo, _ = _bf16_split3(a)
        x_hi, x_lo, _ = _bf16_split3(x)
        return (jnp.dot(a_hi, x_hi, preferred_element_type=F32) + jnp.dot(a_hi, x_lo, preferred_element_type=F32)
                + jnp.dot(a_lo, x_hi, preferred_element_type=F32))

    lane_m = lax.broadcasted_iota(jnp.int32, (w, S5_ROW), 1)
    row_m = lax.broadcasted_iota(jnp.int32, (w, S5_ROW), 0)
    krow = []
    for z in range(2):
        pr, pi = expand(pwk_ref[z, 0, 0], sel_slot), expand(pwk_ref[z, 1, 0], sel_slot)
        cr, ci = expand(ct_ref[z, 0, 0], sel_chan), expand(ct_ref[z, 1, 0], sel_chan)
        xr = jnp.where(same_group, pr * cr - pi * ci, 0.0)
        xi = jnp.where(same_group, pr * ci + pi * cr, 0.0)
        lo = w if z == 0 else 0
        wout_ref[0, (2 * z) * S5_ST:(2 * z + 1) * S5_ST, :] = xr[:, lo:lo + S5_ROW].astype(BF16)
        wout_ref[0, (2 * z + 1) * S5_ST:(2 * z + 2) * S5_ST, :] = (-xi[:, lo:lo + S5_ROW]).astype(BF16)
        k_all = split_dot(b2t_ref[z, 0, 0], xr) - split_dot(b2t_ref[z, 1, 0], xi)
        lo = 0 if z == 0 else w
        krow.append(k_all[:, lo:lo + S5_ROW])
        br, bi = b2t_ref[z, 0, 0], b2t_ref[z, 1, 0]
        for s in range(nt):
            e = nt - 1 - s if z == 0 else s
            qr, qi = pwt_ref[z, 0, 0, e:e + 1, :], pwt_ref[z, 1, 0, e:e + 1, :]
            win_ref[0, s * w:(s + 1) * w, (2 * z) * S5_ST:(2 * z + 1) * S5_ST] = (br * qr - bi * qi).astype(BF16)
            win_ref[0, s * w:(s + 1) * w, (2 * z + 1) * S5_ST:(2 * z + 2) * S5_ST] = (br * qi + bi * qr).astype(BF16)

    kf, kb = krow
    d_diag = jnp.where((lane_m % w) == row_m, d_ref[0], 0.0)
    for s in range(nt):
        f = jnp.where(lane_m >= w * s, pltpu.roll(kf, w * s, 1), 0.0) if s else kf
        sh = w * (nt - 1 - s)
        b = jnp.where(lane_m < S5_ROW - sh, pltpu.roll(kb, S5_ROW - sh, 1), 0.0) if sh else kb
        dd = jnp.where(lane_m // w == s, d_diag, 0.0)
        m_ref[0, s * w:(s + 1) * w, :] = (f + b + dd).astype(BF16)


def _s5_prep(pwk, pwt, ct, b2t, d2):
    npair = d2.shape[0]
    spec = lambda r, c: pl.BlockSpec((2, 2, 1, r, c), lambda p: (0, 0, p, 0, 0))
    out = pl.BlockSpec((1, S5_ROW, S5_ROW), lambda p: (p, 0, 0))
    shape = jax.ShapeDtypeStruct((npair, S5_ROW, S5_ROW), BF16)
    return pl.pallas_call(
        _s5_prep_body,
        grid=(npair,),
        in_specs=[spec(S5_ST, S5_ST), spec(S5_PWT_ROWS, S5_ST), spec(S5_ST, S5_ST), spec(S5_PAIR_CH, S5_ST),
                  pl.BlockSpec((1, 1, S5_ROW), lambda p: (p, 0, 0))],
        out_specs=[out, out, out],
        out_shape=[shape, shape, shape],
        compiler_params=pltpu.CompilerParams(
            dimension_semantics=("parallel",), vmem_limit_bytes=VMEM_LIMIT_BYTES),
        name="s5_prep",
    )(pwk, pwt, ct, b2t, d2)


def _s5_weights(a_re, a_im, log_dt, b_re, b_im, c_re, c_im, d_skip):
    nt, npair = S5_BLK, S5_PAIRS
    lam = lax.complex(a_re, a_im)
    dt = jnp.exp(log_dt)[..., None]
    lam_bar = jnp.exp(lam * dt)
    b_bar = ((lam_bar - 1.0) / lam)[..., None] * lax.complex(b_re, b_im)
    taus = jnp.arange(nt + 1, dtype=F32)
    pw = jnp.exp((lam * dt)[..., None] * taus).reshape(2, npair, S5_ST, nt + 1)
    ri = lambda x: jnp.stack([jnp.real(x), jnp.imag(x)], axis=1)
    pwk = jnp.stack([pw[0], pw[1, ..., ::-1]])
    pwk = jnp.pad(ri(pwk), ((0, 0), (0, 0), (0, 0), (0, 0), (0, S5_ST - (nt + 1))))
    pwt = jnp.pad(ri(pw).transpose(0, 1, 2, 4, 3), ((0, 0), (0, 0), (0, 0), (0, S5_PWT_ROWS - (nt + 1)), (0, 0)))
    ct = lax.complex(c_re, c_im).transpose(0, 1, 3, 2).reshape(2, npair, S5_ST, S5_GC)
    ct = jnp.pad(ri(ct), ((0, 0), (0, 0), (0, 0), (0, 0), (0, S5_ST - S5_GC)))
    bb = b_bar.reshape(2, npair, 2, S5_P, S5_GC)
    eye2 = jnp.eye(2, dtype=F32)
    b2t = (bb.transpose(0, 1, 2, 4, 3)[:, :, :, :, None, :] * eye2[None, None, :, None, :, None])
    b2t = ri(b2t.reshape(2, npair, S5_PAIR_CH, S5_ST))
    d2 = jnp.tile(d_skip.reshape(npair, 1, S5_PAIR_CH), (1, 1, nt))
    m2, win2, wout2 = _s5_prep(pwk, pwt, ct, b2t, d2)
    a_blk = pw[..., nt]
    a2 = jnp.stack([jnp.real(a_blk[0]), jnp.imag(a_blk[0]), jnp.real(a_blk[1]), jnp.imag(a_blk[1])], axis=1)
    return m2, win2, wout2, a2


def _mix_body(x_ref, hf_ref, hb_ref, o_ref, y_ref, nw_ref, wglu_ref, wout_ref, out_ref):
    h = hf_ref[...] + hb_ref[...]
    parts = []
    for hd in range(N_HEADS):
        hh = h[:, hd * HEAD_DIM:(hd + 1) * HEAD_DIM]
        mu = jnp.mean(hh, axis=-1, keepdims=True)
        var = jnp.mean(jnp.square(hh - mu), axis=-1, keepdims=True)
        parts.append((hh - mu) * lax.rsqrt(var + EPS))
    hn = jnp.concatenate(parts, axis=1)
    h_m = hn * nw_ref[...] * _sigmoid(o_ref[...])
    y = y_ref[...]
    gelu = 0.5 * y * (1.0 + jnp.tanh(math.sqrt(2.0 / math.pi) * (y + 0.044715 * (y * y * y))))
    ab = jnp.dot(gelu.astype(BF16), wglu_ref[...], preferred_element_type=F32)
    h_s = ab[:, :D_S5] * _sigmoid(ab[:, D_S5:])
    mix = jnp.dot(h_m.astype(BF16), wout_ref[0:D_MLSTM, :], preferred_element_type=F32)
    mix += jnp.dot(h_s.astype(BF16), wout_ref[D_MLSTM:, :], preferred_element_type=F32)
    out_ref[...] = x_ref[...] + mix


def _mix(x, hf, hb, o_in, y, norm_w, w_glu, w_out):
    L = x.shape[0]
    tm = MIX_TM
    assert L % tm == 0
    whole = pl.BlockSpec(memory_space=pltpu.VMEM)
    row = lambda n: pl.BlockSpec((tm, n), lambda i: (i, 0))
    return pl.pallas_call(
        _mix_body,
        grid=(L // tm,),
        in_specs=[row(D_MODEL), row(D_MLSTM), row(D_MLSTM), row(D_MLSTM), row(D_S5), whole, whole, whole],
        out_specs=row(D_MODEL),
        out_shape=jax.ShapeDtypeStruct((L, D_MODEL), F32),
        compiler_params=pltpu.CompilerParams(
            dimension_semantics=("parallel",), vmem_limit_bytes=VMEM_LIMIT_BYTES),
        name="mix",
    )(x, hf, hb, o_in, y, norm_w, w_glu, w_out)


def _encode(x, p):
    x = _ffn(x, p["norm_ffn1"], p["ffn1_w_gate"], p["ffn1_w_up"], p["ffn1_w_down"], p["norm_final"],
             final_norm=False)
    q, kt, v, o_in, u, rb, col = _in_proj(x, p["norm_mix"], p["w_qk"], p["w_vou"], p["w_gr"], p["gate_bias_r"],
                                          p["conv_w"], p["conv_b"])
    hf, hb = _mlstm(q, kt, v, rb, col)
    y = _s5(u, p["s5_m"], p["s5_win"], p["s5_wout"], p["s5_a"])
    x = _mix(x, hf, hb, o_in, y, p["mlstm_norm_w"], p["s5_w_glu"], p["w_out"])
    return _ffn(x, p["norm_ffn2"], p["ffn2_w_gate"], p["ffn2_w_up"], p["ffn2_w_down"], p["norm_final"],
                final_norm=True)


def _prepare(norm_ffn1, ffn1_w_gate, ffn1_w_up, ffn1_w_down, norm_mix, w_in, conv_w, conv_b, b_igate, b_fgate,
             mlstm_norm_w, s5_a_re, s5_a_im, s5_log_dt, s5_b_re, s5_b_im, s5_c_re, s5_c_im, s5_d, s5_w_glu,
             w_out, norm_ffn2, ffn2_w_gate, ffn2_w_up, ffn2_w_down, norm_final):
    l = 0
    w = w_in[l]
    g0 = 4 * D_MLSTM
    w_g = w[:, g0:g0 + N_GATES]
    gate_bias = jnp.concatenate([b_igate[l].reshape(-1), b_fgate[l].reshape(-1)])
    m2, win2, wout2, a2 = _s5_weights(s5_a_re[l], s5_a_im[l], s5_log_dt[l], s5_b_re[l], s5_b_im[l],
                                      s5_c_re[l], s5_c_im[l], s5_d[l])
    row = lambda a: a.reshape(1, -1).astype(F32)
    return {
        "norm_ffn1": row(norm_ffn1[l]), "norm_ffn2": row(norm_ffn2[l]), "norm_final": row(norm_final),
        "ffn1_w_gate": ffn1_w_gate[l].astype(BF16), "ffn1_w_up": ffn1_w_up[l].astype(BF16),
        "ffn1_w_down": ffn1_w_down[l].astype(BF16),
        "ffn2_w_gate": ffn2_w_gate[l].astype(BF16), "ffn2_w_up": ffn2_w_up[l].astype(BF16),
        "ffn2_w_down": ffn2_w_down[l].astype(BF16),
        "norm_mix": row(norm_mix[l]),
        "w_qk": w[:, :QK_COLS].astype(BF16),
        "w_vou": jnp.concatenate([w[:, QK_COLS:g0], w[:, g0 + N_GATES:]], axis=1).astype(BF16),
        "w_gr": w_g.T.astype(BF16),
        "conv_w": conv_w[l].astype(F32), "conv_b": row(conv_b[l]),
        "gate_bias_r": gate_bias.reshape(N_GATES, 1),
        "mlstm_norm_w": row(mlstm_norm_w[l]),
        "s5_m": m2, "s5_win": win2, "s5_wout": wout2, "s5_a": a2,
        "s5_w_glu": s5_w_glu[l].astype(BF16), "w_out": w_out[l].astype(BF16),
    }


def kernel(x_prompt, x_sample, norm_ffn1, ffn1_w_gate, ffn1_w_up, ffn1_w_down, norm_mix, w_in, conv_w, conv_b, b_igate, b_fgate, mlstm_norm_w, s5_a_re, s5_a_im, s5_log_dt, s5_b_re, s5_b_im, s5_c_re, s5_c_im, s5_d, s5_w_glu, w_out, norm_ffn2, ffn2_w_gate, ffn2_w_up, ffn2_w_down, norm_final):
    assert norm_ffn1.shape[0] == 1 and x_prompt.shape[0] == 1 and x_sample.shape[0] == 1
    p = _prepare(norm_ffn1, ffn1_w_gate, ffn1_w_up, ffn1_w_down, norm_mix, w_in, conv_w, conv_b, b_igate,
                 b_fgate, mlstm_norm_w, s5_a_re, s5_a_im, s5_log_dt, s5_b_re, s5_b_im, s5_c_re, s5_c_im, s5_d,
                 s5_w_glu, w_out, norm_ffn2, ffn2_w_gate, ffn2_w_up, ffn2_w_down, norm_final)
    y_prompt = _encode(x_prompt[0], p)[None]
    y_sample = _encode(x_sample[0], p)[None]
    return (y_prompt, y_sample)
```

```python
import functools
import math

import jax
import jax.numpy as jnp
from jax import lax
from jax.experimental import pallas as pl
from jax.experimental.pallas import tpu as pltpu

F32 = jnp.float32
BF16 = jnp.bfloat16

D_MODEL = 2048
D_MLSTM = 1024
D_S5 = 1024
N_HEADS = 8
HEAD_DIM = 128
CHUNK = 128
N_GATES = 32
GATE_COL_LANES = 128
S5_GROUPS = 64
S5_GC = 16
S5_P = 64
S5_BLK = 16
S5_PAIRS = S5_GROUPS // 2
D_FF = 5632
EPS = 1e-6
M_INIT = -1e30

VMEM_LIMIT_BYTES = 56 * 1024 * 1024

FFN_TM = 1024
FFN_TF = 512
PROJ_TM = 256
PROJ_HALO = 8
MIX_TM = 256


def _sigmoid(x):
    return 1.0 / (1.0 + jnp.exp(-x))


def _rmsnorm(x, w):
    return x * lax.rsqrt(jnp.mean(x * x, axis=-1, keepdims=True) + EPS) * w


def _ffn_body(x_ref, nw_ref, wg_ref, wu_ref, wd_ref, nf_ref, o_ref, xn_ref, *, final_norm):
    j = pl.program_id(1)

    @pl.when(j == 0)
    def _():
        x = x_ref[...]
        xn_ref[...] = _rmsnorm(x, nw_ref[...]).astype(BF16)
        o_ref[...] = x

    xn = xn_ref[...]
    g = jnp.dot(xn, wg_ref[...], preferred_element_type=F32)
    u = jnp.dot(xn, wu_ref[...], preferred_element_type=F32)
    h = (0.5 * g * _sigmoid(g)) * u
    o_ref[...] += jnp.dot(h.astype(BF16), wd_ref[...], preferred_element_type=F32)

    if final_norm:
        @pl.when(j == pl.num_programs(1) - 1)
        def _():
            o_ref[...] = _rmsnorm(o_ref[...], nf_ref[...])


def _ffn(x, norm_w, w_gate, w_up, w_down, norm_final, *, final_norm):
    L = x.shape[0]
    assert L % FFN_TM == 0 and D_FF % FFN_TF == 0
    return pl.pallas_call(
        functools.partial(_ffn_body, final_norm=final_norm),
        grid=(L // FFN_TM, D_FF // FFN_TF),
        in_specs=[
            pl.BlockSpec((FFN_TM, D_MODEL), lambda i, j: (i, 0)),
            pl.BlockSpec((1, D_MODEL), lambda i, j: (0, 0)),
            pl.BlockSpec((D_MODEL, FFN_TF), lambda i, j: (0, j)),
            pl.BlockSpec((D_MODEL, FFN_TF), lambda i, j: (0, j)),
            pl.BlockSpec((FFN_TF, D_MODEL), lambda i, j: (j, 0)),
            pl.BlockSpec((1, D_MODEL), lambda i, j: (0, 0)),
        ],
        out_specs=pl.BlockSpec((FFN_TM, D_MODEL), lambda i, j: (i, 0)),
        out_shape=jax.ShapeDtypeStruct((L, D_MODEL), F32),
        scratch_shapes=[pltpu.VMEM((FFN_TM, D_MODEL), BF16)],
        compiler_params=pltpu.CompilerParams(
            dimension_semantics=("parallel", "arbitrary"), vmem_limit_bytes=VMEM_LIMIT_BYTES),
        name="ffn_final" if final_norm else "ffn",
    )(x, norm_w, w_gate, w_up, w_down, norm_final)


QK_COLS = 2 * D_MLSTM
QK_CB = 512


def _chunk_scan(x, op, identity, backward):
    n = x.shape[-1]
    pos = lax.broadcasted_iota(jnp.int32, x.shape, x.ndim - 1) % CHUNK
    sh = 1
    while sh < CHUNK:
        if backward:
            x = op(x, jnp.where(pos < CHUNK - sh, pltpu.roll(x, n - sh, x.ndim - 1), identity))
        else:
            x = op(x, jnp.where(pos >= sh, pltpu.roll(x, sh, x.ndim - 1), identity))
        sh *= 2
    return x


def _in_proj_body(xp_ref, x_ref, xnx_ref, nw_ref, wqk_ref, wvou_ref, wgr_ref, gb_ref,
                  cw_ref, cb_ref, q_ref, kt_ref, v_ref, o_ref, u_ref, rb_ref, col_ref):
    i = pl.program_id(0)
    tm = x_ref.shape[0]
    nw = nw_ref[...]
    xn = _rmsnorm(x_ref[...], nw)
    xn_prev = jnp.where(i == 0, 0.0, _rmsnorm(xp_ref[...], nw))
    xn_next = jnp.where(i == pl.num_programs(0) - 1, 0.0, _rmsnorm(xnx_ref[...], nw))
    xn_b = xn.astype(BF16)
    xe_b = jnp.concatenate([xn_prev, xn, xn_next], axis=0).astype(BF16)
    rows = tm + 2 * PROJ_HALO

    g = lax.dot_general(wgr_ref[...], xn_b, (((1,), (1,)), ((), ())), preferred_element_type=F32) + gb_ref[...]
    H = N_HEADS
    f_pre = g[2 * H:]
    lf = jnp.minimum(f_pre, 0.0) - jnp.log1p(jnp.exp(-jnp.abs(f_pre)))
    b_f = _chunk_scan(lf[:H], jnp.add, 0.0, False)
    b_b = _chunk_scan(lf[H:], jnp.add, 0.0, True)
    r_f = g[:H] - b_f
    r_b = g[H:2 * H] - b_b
    cm_f = _chunk_scan(r_f, jnp.maximum, -jnp.inf, False)
    cm_b = _chunk_scan(r_b, jnp.maximum, -jnp.inf, True)

    for c in range(QK_COLS // QK_CB):
        cs = slice(c * QK_CB, (c + 1) * QK_CB)
        z = jnp.dot(xe_b, wqk_ref[:, cs], preferred_element_type=F32)
        z_m1 = pltpu.roll(z, 1, 0)[PROJ_HALO:PROJ_HALO + tm]
        z_0 = z[PROJ_HALO:PROJ_HALO + tm]
        z_p1 = pltpu.roll(z, rows - 1, 0)[PROJ_HALO:PROJ_HALO + tm]
        y = z_m1 * cw_ref[0:1, cs] + z_0 * cw_ref[1:2, cs] + z_p1 * cw_ref[2:3, cs] + cb_ref[:, cs]
        y = y * _sigmoid(y)
        if c * QK_CB < D_MLSTM:
            q_ref[:, cs] = (y * (HEAD_DIM ** -0.5)).astype(q_ref.dtype)
        else:
            kt_ref[c * QK_CB - D_MLSTM:(c + 1) * QK_CB - D_MLSTM, :] = y.T.astype(kt_ref.dtype)

    v_ref[...] = jnp.dot(xn_b, wvou_ref[:, 0:D_MLSTM], preferred_element_type=F32).astype(v_ref.dtype)
    o_ref[...] = jnp.dot(xn_b, wvou_ref[:, D_MLSTM:2 * D_MLSTM], preferred_element_type=F32)
    u_ref[...] = jnp.dot(xn_b, wvou_ref[:, 2 * D_MLSTM:], preferred_element_type=F32).astype(u_ref.dtype)
    rb_ref[...] = jnp.concatenate([r_f, r_b, b_f, b_b], axis=0)
    pad = jnp.zeros((GATE_COL_LANES - 4 * H, tm), F32)
    col_ref[...] = jnp.concatenate([b_f, b_b, cm_f, cm_b, pad], axis=0).T


def _in_proj(x, norm_w, w_qk, w_vou, w_gr, gate_bias, conv_w, conv_b):
    L = x.shape[0]
    tm = PROJ_TM
    assert L % tm == 0 and tm % CHUNK == 0
    hb = tm // PROJ_HALO
    nblk8 = L // PROJ_HALO
    whole = pl.BlockSpec(memory_space=pltpu.VMEM)
    return pl.pallas_call(
        _in_proj_body,
        grid=(L // tm,),
        in_specs=[
            pl.BlockSpec((PROJ_HALO, D_MODEL), lambda i: (jnp.maximum(i * hb - 1, 0), 0)),
            pl.BlockSpec((tm, D_MODEL), lambda i: (i, 0)),
            pl.BlockSpec((PROJ_HALO, D_MODEL), lambda i: (jnp.minimum((i + 1) * hb, nblk8 - 1), 0)),
            whole, whole, whole, whole, whole, whole, whole,
        ],
        out_specs=[
            pl.BlockSpec((tm, D_MLSTM), lambda i: (i, 0)),
            pl.BlockSpec((D_MLSTM, tm), lambda i: (0, i)),
            pl.BlockSpec((tm, D_MLSTM), lambda i: (i, 0)),
            pl.BlockSpec((tm, D_MLSTM), lambda i: (i, 0)),
            pl.BlockSpec((tm, D_S5), lambda i: (i, 0)),
            pl.BlockSpec((N_GATES, tm), lambda i: (0, i)),
            pl.BlockSpec((tm, GATE_COL_LANES), lambda i: (i, 0)),
        ],
        out_shape=[
            jax.ShapeDtypeStruct((L, D_MLSTM), BF16),
            jax.ShapeDtypeStruct((D_MLSTM, L), BF16),
            jax.ShapeDtypeStruct((L, D_MLSTM), BF16),
            jax.ShapeDtypeStruct((L, D_MLSTM), F32),
            jax.ShapeDtypeStruct((L, D_S5), BF16),
            jax.ShapeDtypeStruct((N_GATES, L), F32),
            jax.ShapeDtypeStruct((L, GATE_COL_LANES), F32),
        ],
        compiler_params=pltpu.CompilerParams(
            dimension_semantics=("parallel",), vmem_limit_bytes=VMEM_LIMIT_BYTES),
        name="in_proj",
    )(x, x, x, norm_w, w_qk, w_vou, w_gr, gate_bias, conv_w, conv_b)


def _bf16_split3(x):
    hi = x.astype(BF16)
    r1 = x - hi.astype(F32)
    mid = r1.astype(BF16)
    lo = (r1 - mid.astype(F32)).astype(BF16)
    return hi, mid, lo


def _mlstm_direction(q_ref, kt_ref, v_ref, rb_ref, col_ref, h_ref, ct_ref, m_ref, ml_ref, d):
    T = CHUNK
    H = N_HEADS
    backward = d == 1
    rr = lax.broadcasted_iota(jnp.int32, (T, T), 0)
    cc = lax.broadcasted_iota(jnp.int32, (T, T), 1)
    mask = (cc >= rr) if backward else (cc <= rr)
    last = 0 if backward else T - 1

    r_rows = rb_ref[d * H:(d + 1) * H, :]
    b_rows = rb_ref[(2 + d) * H:(3 + d) * H, :]
    b_cols = col_ref[:, d * H:(d + 1) * H]
    cm_cols = col_ref[:, (2 + d) * H:(3 + d) * H]
    m_prev = m_ref[d]
    m_prev_l = ml_ref[d]
    m_cols = jnp.maximum(cm_cols, m_prev_l)
    clamp_cols = -(b_cols + m_cols)
    m_last = jnp.maximum(jnp.max(r_rows, axis=1, keepdims=True), m_prev)
    b_tot = jnp.broadcast_to(b_rows[:, last:last + 1], (H, T))
    sc_rows = jnp.exp(m_prev - m_last)
    wkk_rows = jnp.exp(r_rows - m_last)
    ones = jnp.ones((T, HEAD_DIM), BF16)

    for h in range(H):
        ci = d * H + h
        hs = slice(h * HEAD_DIM, (h + 1) * HEAD_DIM)
        q = q_ref[:, hs]
        kt = kt_ref[hs, :]
        vaug = jnp.concatenate([v_ref[:, hs], ones], axis=1)
        ct_prev = ct_ref[ci]
        m_col = jnp.broadcast_to(m_cols[:, h:h + 1], (T, T))
        w = jnp.exp(jnp.where(mask, r_rows[h:h + 1, :] - m_col, -jnp.inf))
        sq = jnp.dot(q, jnp.concatenate([kt, ct_prev.astype(BF16)], axis=1), preferred_element_type=F32)
        s = sq[:, :T] * w
        qc = sq[:, T:]
        kw = (kt.astype(F32) * wkk_rows[h:h + 1, :]).astype(BF16)
        both = jnp.dot(jnp.concatenate([s.astype(BF16), kw], axis=0), vaug, preferred_element_type=F32)
        sv = both[:T]
        upd = both[T:]
        s_inter = jnp.exp(m_prev[h:h + 1, :] - m_col)
        num = sv[:, :HEAD_DIM] + s_inter * qc[:, :HEAD_DIM]
        den = sv[:, HEAD_DIM:] + s_inter * qc[:, HEAD_DIM:]
        floor = jnp.exp(jnp.broadcast_to(clamp_cols[:, h:h + 1], (T, HEAD_DIM)))
        h_ref[:, hs] = num / jnp.maximum(jnp.abs(den), floor)

        sc = sc_rows[h:h + 1, :]
        ct_ref[ci] = jnp.concatenate([sc, sc], axis=1) * ct_prev + upd

    m_ref[d] = b_tot + m_last
    ml_ref[d] = b_cols[last:last + 1, :] + m_cols[last:last + 1, :]


def _mlstm_body(qf_ref, ktf_ref, vf_ref, rbf_ref, colf_ref, qb_ref, ktb_ref, vb_ref, rbb_ref, colb_ref,
                hf_ref, hb_ref, ct_ref, m_ref, ml_ref):
    @pl.when(pl.program_id(0) == 0)
    def _():
        ct_ref[...] = jnp.zeros_like(ct_ref)
        m_ref[...] = jnp.full_like(m_ref, M_INIT)
        ml_ref[...] = jnp.full_like(ml_ref, M_INIT)

    _mlstm_direction(qf_ref, ktf_ref, vf_ref, rbf_ref, colf_ref, hf_ref, ct_ref, m_ref, ml_ref, 0)
    _mlstm_direction(qb_ref, ktb_ref, vb_ref, rbb_ref, colb_ref, hb_ref, ct_ref, m_ref, ml_ref, 1)


def _mlstm(q, kt, v, rb, col):
    L = q.shape[0]
    T = CHUNK
    assert L % T == 0 and T == HEAD_DIM
    nc = L // T
    fwd = lambda c: (c, 0)
    bwd = lambda c: (nc - 1 - c, 0)
    fwd_r = lambda c: (0, c)
    bwd_r = lambda c: (0, nc - 1 - c)
    return pl.pallas_call(
        _mlstm_body,
        grid=(nc,),
        in_specs=[
            pl.BlockSpec((T, D_MLSTM), fwd), pl.BlockSpec((D_MLSTM, T), fwd_r), pl.BlockSpec((T, D_MLSTM), fwd),
            pl.BlockSpec((N_GATES, T), fwd_r), pl.BlockSpec((T, GATE_COL_LANES), fwd),
            pl.BlockSpec((T, D_MLSTM), bwd), pl.BlockSpec((D_MLSTM, T), bwd_r), pl.BlockSpec((T, D_MLSTM), bwd),
            pl.BlockSpec((N_GATES, T), bwd_r), pl.BlockSpec((T, GATE_COL_LANES), bwd),
        ],
        out_specs=[pl.BlockSpec((T, D_MLSTM), fwd), pl.BlockSpec((T, D_MLSTM), bwd)],
        out_shape=[jax.ShapeDtypeStruct((L, D_MLSTM), F32), jax.ShapeDtypeStruct((L, D_MLSTM), F32)],
        scratch_shapes=[
            pltpu.VMEM((2 * N_HEADS, HEAD_DIM, 2 * HEAD_DIM), F32),
            pltpu.VMEM((2, N_HEADS, T), F32),
            pltpu.VMEM((2, 1, N_HEADS), F32),
        ],
        compiler_params=pltpu.CompilerParams(
            dimension_semantics=("arbitrary",), vmem_limit_bytes=VMEM_LIMIT_BYTES),
        name="mlstm",
    )(q, kt, v, rb, col, q, kt, v, rb, col)


S5_PAIR_CH = 2 * S5_GC
S5_ROW = S5_BLK * S5_PAIR_CH
S5_ST = 2 * S5_P
S5_TILE = 8
S5_LANE_PAIRS = 4
S5_TT = 2048
S5_NBT = S5_TT // S5_BLK


def _s5_scan(a, sin_ref, nb):
    R = S5_TILE
    ntile = nb // R
    row = lax.broadcasted_iota(jnp.int32, (R, S5_ST), 0)
    zero = jnp.zeros((1, S5_ST), F32)

    def cmul(x, y):
        return x[0] * y[0] - x[1] * y[1], x[0] * y[1] + x[1] * y[0]

    def bcast(x):
        return tuple(jnp.broadcast_to(t, (R, S5_ST)) for t in x)

    def tables(ar, ai, backward):
        pw = {1: (ar, ai)}
        for e in range(2, R + 1):
            pw[e] = cmul(pw[e // 2], pw[e - e // 2])
        steps = []
        for sh in (1, 2, 4):
            keep = (row < R - sh) if backward else (row >= sh)
            steps.append(tuple(jnp.where(keep, t, 0.0) for t in bcast(pw[sh])))
        order = range(R, 0, -1) if backward else range(1, R + 1)
        carry_pw = tuple(jnp.concatenate([pw[e][j] for e in order], axis=0) for j in (0, 1))
        return steps, carry_pw

    def scan_tile(x, carry, steps, cpw, backward):
        for sh, am in zip((1, 2, 4), steps):
            rs = (R - sh) if backward else sh
            x = tuple(p + q for p, q in zip(x, cmul(am, (pltpu.roll(x[0], rs, 0), pltpu.roll(x[1], rs, 0)))))
        cb = bcast(carry)
        x = tuple(p + q for p, q in zip(x, cmul(cpw, cb)))
        edge, rs = (R - 1, R - 1) if backward else (0, 1)
        enter = tuple(jnp.where(row == edge, c, pltpu.roll(t, rs, 0)) for t, c in zip(x, cb))
        last = 0 if backward else R - 1
        return enter, (x[0][last:last + 1], x[1][last:last + 1])

    tabs = []
    for p in range(S5_LANE_PAIRS):
        ap = a[p]
        tabs.append((tables(ap[0:1], ap[1:2], False), tables(ap[2:3], ap[3:4], True)))

    def step(i, carry):
        rf = pl.multiple_of(i * R, R)
        rb = pl.multiple_of((ntile - 1 - i) * R, R)
        out = []
        for p in range(S5_LANE_PAIRS):
            (steps_f, cpw_f), (steps_b, cpw_b) = tabs[p]
            cf, cb = carry[p]
            xf = (sin_ref[p, pl.ds(rf, R), 0:S5_ST], sin_ref[p, pl.ds(rf, R), S5_ST:2 * S5_ST])
            xb = (sin_ref[p, pl.ds(rb, R), 2 * S5_ST:3 * S5_ST], sin_ref[p, pl.ds(rb, R), 3 * S5_ST:4 * S5_ST])
            ef, cf = scan_tile(xf, cf, steps_f, cpw_f, False)
            eb, cb = scan_tile(xb, cb, steps_b, cpw_b, True)
            sin_ref[p, pl.ds(rf, R), 0:S5_ST] = ef[0]
            sin_ref[p, pl.ds(rf, R), S5_ST:2 * S5_ST] = ef[1]
            sin_ref[p, pl.ds(rb, R), 2 * S5_ST:3 * S5_ST] = eb[0]
            sin_ref[p, pl.ds(rb, R), 3 * S5_ST:4 * S5_ST] = eb[1]
            out.append((cf, cb))
        return tuple(out)

    init = tuple(((zero, zero), (zero, zero)) for _ in range(S5_LANE_PAIRS))
    lax.fori_loop(0, ntile, step, init)


def _s5_body(u_ref, m_ref, win_ref, wout_ref, a_ref, y_ref, tok_ref, u2_ref, sin_ref, y2_ref, stage_ref):
    phase = pl.program_id(1)
    t = pl.program_id(2)
    nb = u2_ref.shape[1]
    r0 = pl.multiple_of(t * S5_NBT, S5_NBT)
    rows = pl.ds(r0, S5_NBT)

    @pl.when(phase == 0)
    def _():
        tok_ref[...] = u_ref[...].astype(F32)
        for s in range(S5_BLK):
            tok_s = tok_ref[pl.ds(s, S5_NBT, stride=S5_BLK), :]
            for p in range(S5_LANE_PAIRS):
                u2_ref[p, rows, s * S5_PAIR_CH:(s + 1) * S5_PAIR_CH] = (
                    tok_s[:, p * S5_PAIR_CH:(p + 1) * S5_PAIR_CH].astype(BF16))
        for p in range(S5_LANE_PAIRS):
            sin_ref[p, rows, :] = jnp.dot(u2_ref[p, rows, :], win_ref[p], preferred_element_type=F32)

    @pl.when((phase == 0) & (t == pl.num_programs(2) - 1))
    def _():
        _s5_scan(a_ref[...], sin_ref, nb)

    @pl.when(phase == 1)
    def _():
        for p in range(S5_LANE_PAIRS):
            y2_ref[p] = (jnp.dot(u2_ref[p, rows, :], m_ref[p], preferred_element_type=F32)
                         + jnp.dot(sin_ref[p, rows, :].astype(BF16), wout_ref[p], preferred_element_type=F32))
        for s in range(S5_BLK):
            for p in range(S5_LANE_PAIRS):
                stage_ref[:, p * S5_PAIR_CH:(p + 1) * S5_PAIR_CH] = y2_ref[p, :, s * S5_PAIR_CH:(s + 1) * S5_PAIR_CH]
            y_ref[pl.ds(s, S5_NBT, stride=S5_BLK), :] = stage_ref[...]


def _s5(u, m2, win2, wout2, a2):
    L = u.shape[0]
    assert L % S5_TT == 0 and D_S5 == S5_PAIRS * S5_PAIR_CH
    nb = L // S5_BLK
    nq = S5_PAIRS // S5_LANE_PAIRS
    lane_tile = S5_LANE_PAIRS * S5_PAIR_CH
    wspec = pl.BlockSpec((S5_LANE_PAIRS, S5_ROW, S5_ROW), lambda q, ph, t: (q, 0, 0))
    return pl.pallas_call(
        _s5_body,
        grid=(nq, 2, L // S5_TT),
        in_specs=[
            pl.BlockSpec((S5_TT, lane_tile), lambda q, ph, t: (t, q)),
            wspec, wspec, wspec,
            pl.BlockSpec((S5_LANE_PAIRS, 4, S5_ST), lambda q, ph, t: (q, 0, 0)),
        ],
        out_specs=pl.BlockSpec((S5_TT, lane_tile), lambda q, ph, t: (t * ph, q)),
        out_shape=jax.ShapeDtypeStruct((L, D_S5), F32),
        scratch_shapes=[
            pltpu.VMEM((S5_TT, lane_tile), F32),
            pltpu.VMEM((S5_LANE_PAIRS, nb, S5_ROW), BF16),
            pltpu.VMEM((S5_LANE_PAIRS, nb, 4 * S5_ST), F32),
            pltpu.VMEM((S5_LANE_PAIRS, S5_NBT, S5_ROW), F32),
            pltpu.VMEM((S5_NBT, lane_tile), F32),
        ],
        compiler_params=pltpu.CompilerParams(
            dimension_semantics=("parallel", "arbitrary", "arbitrary"), vmem_limit_bytes=VMEM_LIMIT_BYTES),
        name="s5",
    )(u, m2, win2, wout2, a2)


S5_EXP = (S5_BLK + 1) * S5_PAIR_CH
S5_EXP_PAD = 640
S5_PWT_ROWS = 24


def _s5_prep_body(pwk_ref, pwt_ref, ct_ref, b2t_ref, d_ref, m_ref, win_ref, wout_ref):
    nt, w = S5_BLK, S5_PAIR_CH
    lane = lax.broadcasted_iota(jnp.int32, (S5_ST, S5_EXP_PAD), 1)
    row = lax.broadcasted_iota(jnp.int32, (S5_ST, S5_EXP_PAD), 0)
    sel_slot = (lane // w == row).astype(BF16)
    sel_chan = ((lane % S5_GC == row) & (row < S5_GC)).astype(BF16)
    same_group = (row // S5_P == (lane // S5_GC) % 2) & (lane < S5_EXP)

    def expand(x, sel):
        return sum(jnp.dot(p, sel, preferred_element_type=F32) for p in _bf16_split3(x))

    def split_dot(a, x):
        a_hi, a_lo, _ = _bf16_split3(a)
        x_hi, x_lo, _ = _bf16_split3(x)
        return (jnp.dot(a_hi, x_hi, preferred_element_type=F32) + jnp.dot(a_hi, x_lo, preferred_element_type=F32)
                + jnp.dot(a_lo, x_hi, preferred_element_type=F32))

    lane_m = lax.broadcasted_iota(jnp.int32, (w, S5_ROW), 1)
    row_m = lax.broadcasted_iota(jnp.int32, (w, S5_ROW), 0)
    krow = []
    for z in range(2):
        pr, pi = expand(pwk_ref[z, 0, 0], sel_slot), expand(pwk_ref[z, 1, 0], sel_slot)
        cr, ci = expand(ct_ref[z, 0, 0], sel_chan), expand(ct_ref[z, 1, 0], sel_chan)
        xr = jnp.where(same_group, pr * cr - pi * ci, 0.0)
        xi = jnp.where(same_group, pr * ci + pi * cr, 0.0)
        lo = w if z == 0 else 0
        wout_ref[0, (2 * z) * S5_ST:(2 * z + 1) * S5_ST, :] = xr[:, lo:lo + S5_ROW].astype(BF16)
        wout_ref[0, (2 * z + 1) * S5_ST:(2 * z + 2) * S5_ST, :] = (-xi[:, lo:lo + S5_ROW]).astype(BF16)
        k_all = split_dot(b2t_ref[z, 0, 0], xr) - split_dot(b2t_ref[z, 1, 0], xi)
        lo = 0 if z == 0 else w
        krow.append(k_all[:, lo:lo + S5_ROW])
        br, bi = b2t_ref[z, 0, 0], b2t_ref[z, 1, 0]
        for s in range(nt):
            e = nt - 1 - s if z == 0 else s
            qr, qi = pwt_ref[z, 0, 0, e:e + 1, :], pwt_ref[z, 1, 0, e:e + 1, :]
            win_ref[0, s * w:(s + 1) * w, (2 * z) * S5_ST:(2 * z + 1) * S5_ST] = (br * qr - bi * qi).astype(BF16)
            win_ref[0, s * w:(s + 1) * w, (2 * z + 1) * S5_ST:(2 * z + 2) * S5_ST] = (br * qi + bi * qr).astype(BF16)

    kf, kb = krow
    d_diag = jnp.where((lane_m % w) == row_m, d_ref[0], 0.0)
    for s in range(nt):
        f = jnp.where(lane_m >= w * s, pltpu.roll(kf, w * s, 1), 0.0) if s else kf
        sh = w * (nt - 1 - s)
        b = jnp.where(lane_m < S5_ROW - sh, pltpu.roll(kb, S5_ROW - sh, 1), 0.0) if sh else kb
        dd = jnp.where(lane_m // w == s, d_diag, 0.0)
        m_ref[0, s * w:(s + 1) * w, :] = (f + b + dd).astype(BF16)


def _s5_prep(pwk, pwt, ct, b2t, d2):
    npair = d2.shape[0]
    spec = lambda r, c: pl.BlockSpec((2, 2, 1, r, c), lambda p: (0, 0, p, 0, 0))
    out = pl.BlockSpec((1, S5_ROW, S5_ROW), lambda p: (p, 0, 0))
    shape = jax.ShapeDtypeStruct((npair, S5_ROW, S5_ROW), BF16)
    return pl.pallas_call(
        _s5_prep_body,
        grid=(npair,),
        in_specs=[spec(S5_ST, S5_ST), spec(S5_PWT_ROWS, S5_ST), spec(S5_ST, S5_ST), spec(S5_PAIR_CH, S5_ST),
                  pl.BlockSpec((1, 1, S5_ROW), lambda p: (p, 0, 0))],
        out_specs=[out, out, out],
        out_shape=[shape, shape, shape],
        compiler_params=pltpu.CompilerParams(
            dimension_semantics=("parallel",), vmem_limit_bytes=VMEM_LIMIT_BYTES),
        name="s5_prep",
    )(pwk, pwt, ct, b2t, d2)


def _s5_weights(a_re, a_im, log_dt, b_re, b_im, c_re, c_im, d_skip):
    nt, npair = S5_BLK, S5_PAIRS
    lam = lax.complex(a_re, a_im)
    dt = jnp.exp(log_dt)[..., None]
    lam_bar = jnp.exp(lam * dt)
    b_bar = ((lam_bar - 1.0) / lam)[..., None] * lax.complex(b_re, b_im)
    taus = jnp.arange(nt + 1, dtype=F32)
    pw = jnp.exp((lam * dt)[..., None] * taus).reshape(2, npair, S5_ST, nt + 1)
    ri = lambda x: jnp.stack([jnp.real(x), jnp.imag(x)], axis=1)
    pwk = jnp.stack([pw[0], pw[1, ..., ::-1]])
    pwk = jnp.pad(ri(pwk), ((0, 0), (0, 0), (0, 0), (0, 0), (0, S5_ST - (nt + 1))))
    pwt = jnp.pad(ri(pw).transpose(0, 1, 2, 4, 3), ((0, 0), (0, 0), (0, 0), (0, S5_PWT_ROWS - (nt + 1)), (0, 0)))
    ct = lax.complex(c_re, c_im).transpose(0, 1, 3, 2).reshape(2, npair, S5_ST, S5_GC)
    ct = jnp.pad(ri(ct), ((0, 0), (0, 0), (0, 0), (0, 0), (0, S5_ST - S5_GC)))
    bb = b_bar.reshape(2, npair, 2, S5_P, S5_GC)
    eye2 = jnp.eye(2, dtype=F32)
    b2t = (bb.transpose(0, 1, 2, 4, 3)[:, :, :, :, None, :] * eye2[None, None, :, None, :, None])
    b2t = ri(b2t.reshape(2, npair, S5_PAIR_CH, S5_ST))
    d2 = jnp.tile(d_skip.reshape(npair, 1, S5_PAIR_CH), (1, 1, nt))
    m2, win2, wout2 = _s5_prep(pwk, pwt, ct, b2t, d2)
    a_blk = pw[..., nt]
    a2 = jnp.stack([jnp.real(a_blk[0]), jnp.imag(a_blk[0]), jnp.real(a_blk[1]), jnp.imag(a_blk[1])], axis=1)
    return m2, win2, wout2, a2


def _mix_body(x_ref, hf_ref, hb_ref, o_ref, y_ref, nw_ref, wglu_ref, wout_ref, out_ref):
    h = hf_ref[...] + hb_ref[...]
    parts = []
    for hd in range(N_HEADS):
        hh = h[:, hd * HEAD_DIM:(hd + 1) * HEAD_DIM]
        mu = jnp.mean(hh, axis=-1, keepdims=True)
        var = jnp.mean(jnp.square(hh - mu), axis=-1, keepdims=True)
        parts.append((hh - mu) * lax.rsqrt(var + EPS))
    hn = jnp.concatenate(parts, axis=1)
    h_m = hn * nw_ref[...] * _sigmoid(o_ref[...])
    y = y_ref[...]
    gelu = 0.5 * y * (1.0 + jnp.tanh(math.sqrt(2.0 / math.pi) * (y + 0.044715 * (y * y * y))))
    ab = jnp.dot(gelu.astype(BF16), wglu_ref[...], preferred_element_type=F32)
    h_s = ab[:, :D_S5] * _sigmoid(ab[:, D_S5:])
    mix = jnp.dot(h_m.astype(BF16), wout_ref[0:D_MLSTM, :], preferred_element_type=F32)
    mix += jnp.dot(h_s.astype(BF16), wout_ref[D_MLSTM:, :], preferred_element_type=F32)
    out_ref[...] = x_ref[...] + mix


def _mix(x, hf, hb, o_in, y, norm_w, w_glu, w_out):
    L = x.shape[0]
    tm = MIX_TM
    assert L % tm == 0
    whole = pl.BlockSpec(memory_space=pltpu.VMEM)
    row = lambda n: pl.BlockSpec((tm, n), lambda i: (i, 0))
    return pl.pallas_call(
        _mix_body,
        grid=(L // tm,),
        in_specs=[row(D_MODEL), row(D_MLSTM), row(D_MLSTM), row(D_MLSTM), row(D_S5), whole, whole, whole],
        out_specs=row(D_MODEL),
        out_shape=jax.ShapeDtypeStruct((L, D_MODEL), F32),
        compiler_params=pltpu.CompilerParams(
            dimension_semantics=("parallel",), vmem_limit_bytes=VMEM_LIMIT_BYTES),
        name="mix",
    )(x, hf, hb, o_in, y, norm_w, w_glu, w_out)


def _encode(x, p):
    x = _ffn(x, p["norm_ffn1"], p["ffn1_w_gate"], p["ffn1_w_up"], p["ffn1_w_down"], p["norm_final"],
             final_norm=False)
    q, kt, v, o_in, u, rb, col = _in_proj(x, p["norm_mix"], p["w_qk"], p["w_vou"], p["w_gr"], p["gate_bias_r"],
                                          p["conv_w"], p["conv_b"])
    hf, hb = _mlstm(q, kt, v, rb, col)
    y = _s5(u, p["s5_m"], p["s5_win"], p["s5_wout"], p["s5_a"])
    x = _mix(x, hf, hb, o_in, y, p["mlstm_norm_w"], p["s5_w_glu"], p["w_out"])
    return _ffn(x, p["norm_ffn2"], p["ffn2_w_gate"], p["ffn2_w_up"], p["ffn2_w_down"], p["norm_final"],
                final_norm=True)


def _prepare(norm_ffn1, ffn1_w_gate, ffn1_w_up, ffn1_w_down, norm_mix, w_in, conv_w, conv_b, b_igate, b_fgate,
             mlstm_norm_w, s5_a_re, s5_a_im, s5_log_dt, s5_b_re, s5_b_im, s5_c_re, s5_c_im, s5_d, s5_w_glu,
             w_out, norm_ffn2, ffn2_w_gate, ffn2_w_up, ffn2_w_down, norm_final):
    l = 0
    w = w_in[l]
    g0 = 4 * D_MLSTM
    w_g = w[:, g0:g0 + N_GATES]
    gate_bias = jnp.concatenate([b_igate[l].reshape(-1), b_fgate[l].reshape(-1)])
    m2, win2, wout2, a2 = _s5_weights(s5_a_re[l], s5_a_im[l], s5_log_dt[l], s5_b_re[l], s5_b_im[l],
                                      s5_c_re[l], s5_c_im[l], s5_d[l])
    row = lambda a: a.reshape(1, -1).astype(F32)
    return {
        "norm_ffn1": row(norm_ffn1[l]), "norm_ffn2": row(norm_ffn2[l]), "norm_final": row(norm_final),
        "ffn1_w_gate": ffn1_w_gate[l].astype(BF16), "ffn1_w_up": ffn1_w_up[l].astype(BF16),
        "ffn1_w_down": ffn1_w_down[l].astype(BF16),
        "ffn2_w_gate": ffn2_w_gate[l].astype(BF16), "ffn2_w_up": ffn2_w_up[l].astype(BF16),
        "ffn2_w_down": ffn2_w_down[l].astype(BF16),
        "norm_mix": row(norm_mix[l]),
        "w_qk": w[:, :QK_COLS].astype(BF16),
        "w_vou": jnp.concatenate([w[:, QK_COLS:g0], w[:, g0 + N_GATES:]], axis=1).astype(BF16),
        "w_gr": w_g.T.astype(BF16),
        "conv_w": conv_w[l].astype(F32), "conv_b": row(conv_b[l]),
        "gate_bias_r": gate_bias.reshape(N_GATES, 1),
        "mlstm_norm_w": row(mlstm_norm_w[l]),
        "s5_m": m2, "s5_win": win2, "s5_wout": wout2, "s5_a": a2,
        "s5_w_glu": s5_w_glu[l].astype(BF16), "w_out": w_out[l].astype(BF16),
    }


def kernel(x_prompt, x_sample, norm_ffn1, ffn1_w_gate, ffn1_w_up, ffn1_w_down, norm_mix, w_in, conv_w, conv_b, b_igate, b_fgate, mlstm_norm_w, s5_a_re, s5_a_im, s5_log_dt, s5_b_re, s5_b_im, s5_c_re, s5_c_im, s5_d, s5_w_glu, w_out, norm_ffn2, ffn2_w_gate, ffn2_w_up, ffn2_w_down, norm_final):
    assert norm_ffn1.shape[0] == 1 and x_prompt.shape[0] == 1 and x_sample.shape[0] == 1
    p = _prepare(norm_ffn1, ffn1_w_gate, ffn1_w_up, ffn1_w_down, norm_mix, w_in, conv_w, conv_b, b_igate,
                 b_fgate, mlstm_norm_w, s5_a_re, s5_a_im, s5_log_dt, s5_b_re, s5_b_im, s5_c_re, s5_c_im, s5_d,
                 s5_w_glu, w_out, norm_ffn2, ffn2_w_gate, ffn2_w_up, ffn2_w_down, norm_final)
    y_prompt = _encode(x_prompt[0], p)[None]
    y_sample = _encode(x_sample[0], p)[None]
    return (y_prompt, y_sample)
```

```python
import functools
import math

import jax
import jax.numpy as jnp
from jax import lax
from jax.experimental import pallas as pl
from jax.experimental.pallas import tpu as pltpu

F32 = jnp.float32
BF16 = jnp.bfloat16

D_MODEL = 2048
D_MLSTM = 1024
D_S5 = 1024
N_HEADS = 8
HEAD_DIM = 128
CHUNK = 128
MLSTM_SUB = 4
N_GATES = 32
GATE_COL_LANES = 128
S5_GROUPS = 64
S5_GC = 16
S5_P = 64
S5_BLK = 16
S5_PAIRS = S5_GROUPS // 2
D_FF = 5632
EPS = 1e-6
M_INIT = -1e30

VMEM_LIMIT_BYTES = 56 * 1024 * 1024

FFN_TM = 1024
FFN_TF = 512
PROJ_TM = 256
PROJ_HALO = 8
MIX_TM = 512


def _sigmoid(x):
    return 1.0 / (1.0 + jnp.exp(-x))


def _rmsnorm(x, w):
    return x * lax.rsqrt(jnp.mean(x * x, axis=-1, keepdims=True) + EPS) * w


def _ffn_body(x_ref, nw_ref, wg_ref, wu_ref, wd_ref, nf_ref, o_ref, xn_ref, *, final_norm):
    j = pl.program_id(1)

    @pl.when(j == 0)
    def _():
        x = x_ref[...]
        xn_ref[...] = _rmsnorm(x, nw_ref[...]).astype(BF16)
        o_ref[...] = x

    xn = xn_ref[...]
    g = jnp.dot(xn, wg_ref[...], preferred_element_type=F32)
    u = jnp.dot(xn, wu_ref[...], preferred_element_type=F32)
    h = (0.5 * g * _sigmoid(g)) * u
    o_ref[...] += jnp.dot(h.astype(BF16), wd_ref[...], preferred_element_type=F32)

    if final_norm:
        @pl.when(j == pl.num_programs(1) - 1)
        def _():
            o_ref[...] = _rmsnorm(o_ref[...], nf_ref[...])


def _ffn(x, norm_w, w_gate, w_up, w_down, norm_final, *, final_norm):
    L = x.shape[0]
    assert L % FFN_TM == 0 and D_FF % FFN_TF == 0
    return pl.pallas_call(
        functools.partial(_ffn_body, final_norm=final_norm),
        grid=(L // FFN_TM, D_FF // FFN_TF),
        in_specs=[
            pl.BlockSpec((FFN_TM, D_MODEL), lambda i, j: (i, 0)),
            pl.BlockSpec((1, D_MODEL), lambda i, j: (0, 0)),
            pl.BlockSpec((D_MODEL, FFN_TF), lambda i, j: (0, j)),
            pl.BlockSpec((D_MODEL, FFN_TF), lambda i, j: (0, j)),
            pl.BlockSpec((FFN_TF, D_MODEL), lambda i, j: (j, 0)),
            pl.BlockSpec((1, D_MODEL), lambda i, j: (0, 0)),
        ],
        out_specs=pl.BlockSpec((FFN_TM, D_MODEL), lambda i, j: (i, 0)),
        out_shape=jax.ShapeDtypeStruct((L, D_MODEL), F32),
        scratch_shapes=[pltpu.VMEM((FFN_TM, D_MODEL), BF16)],
        compiler_params=pltpu.CompilerParams(
            dimension_semantics=("parallel", "arbitrary"), vmem_limit_bytes=VMEM_LIMIT_BYTES),
        name="ffn_final" if final_norm else "ffn",
    )(x, norm_w, w_gate, w_up, w_down, norm_final)


QK_COLS = 2 * D_MLSTM
QK_CB = 512


def _chunk_scan(x, op, identity, backward):
    n = x.shape[-1]
    pos = lax.broadcasted_iota(jnp.int32, x.shape, x.ndim - 1) % CHUNK
    sh = 1
    while sh < CHUNK:
        if backward:
            x = op(x, jnp.where(pos < CHUNK - sh, pltpu.roll(x, n - sh, x.ndim - 1), identity))
        else:
            x = op(x, jnp.where(pos >= sh, pltpu.roll(x, sh, x.ndim - 1), identity))
        sh *= 2
    return x


def _in_proj_body(xp_ref, x_ref, xnx_ref, nw_ref, wqk_ref, wvou_ref, wgr_ref, gb_ref,
                  cw_ref, cb_ref, q_ref, kt_ref, v_ref, o_ref, u_ref, rb_ref, col_ref):
    i = pl.program_id(0)
    tm = x_ref.shape[0]
    nw = nw_ref[...]
    xn = _rmsnorm(x_ref[...], nw)
    xn_prev = jnp.where(i == 0, 0.0, _rmsnorm(xp_ref[...], nw))
    xn_next = jnp.where(i == pl.num_programs(0) - 1, 0.0, _rmsnorm(xnx_ref[...], nw))
    xn_b = xn.astype(BF16)
    xe_b = jnp.concatenate([xn_prev, xn, xn_next], axis=0).astype(BF16)
    rows = tm + 2 * PROJ_HALO

    g = lax.dot_general(wgr_ref[...], xn_b, (((1,), (1,)), ((), ())), preferred_element_type=F32) + gb_ref[...]
    H = N_HEADS
    f_pre = g[2 * H:]
    lf = jnp.minimum(f_pre, 0.0) - jnp.log1p(jnp.exp(-jnp.abs(f_pre)))
    b_f = _chunk_scan(lf[:H], jnp.add, 0.0, False)
    b_b = _chunk_scan(lf[H:], jnp.add, 0.0, True)
    r_f = g[:H] - b_f
    r_b = g[H:2 * H] - b_b
    cm_f = _chunk_scan(r_f, jnp.maximum, -jnp.inf, False)
    cm_b = _chunk_scan(r_b, jnp.maximum, -jnp.inf, True)

    for c in range(QK_COLS // QK_CB):
        cs = slice(c * QK_CB, (c + 1) * QK_CB)
        z = jnp.dot(xe_b, wqk_ref[:, cs], preferred_element_type=F32)
        z_m1 = pltpu.roll(z, 1, 0)[PROJ_HALO:PROJ_HALO + tm]
        z_0 = z[PROJ_HALO:PROJ_HALO + tm]
        z_p1 = pltpu.roll(z, rows - 1, 0)[PROJ_HALO:PROJ_HALO + tm]
        y = z_m1 * cw_ref[0:1, cs] + z_0 * cw_ref[1:2, cs] + z_p1 * cw_ref[2:3, cs] + cb_ref[:, cs]
        y = y * _sigmoid(y)
        if c * QK_CB < D_MLSTM:
            q_ref[:, cs] = (y * (HEAD_DIM ** -0.5)).astype(q_ref.dtype)
        else:
            kt_ref[c * QK_CB - D_MLSTM:(c + 1) * QK_CB - D_MLSTM, :] = y.T.astype(kt_ref.dtype)

    v_ref[...] = jnp.dot(xn_b, wvou_ref[:, 0:D_MLSTM], preferred_element_type=F32).astype(v_ref.dtype)
    o_ref[...] = jnp.dot(xn_b, wvou_ref[:, D_MLSTM:2 * D_MLSTM], preferred_element_type=F32)
    u_ref[...] = jnp.dot(xn_b, wvou_ref[:, 2 * D_MLSTM:], preferred_element_type=F32).astype(u_ref.dtype)
    rb_ref[...] = jnp.concatenate([r_f, r_b, b_f, b_b], axis=0)
    pad = jnp.zeros((GATE_COL_LANES - 4 * H, tm), F32)
    col_ref[...] = jnp.concatenate([b_f, b_b, cm_f, cm_b, pad], axis=0).T


def _in_proj(x, norm_w, w_qk, w_vou, w_gr, gate_bias, conv_w, conv_b):
    L = x.shape[0]
    tm = PROJ_TM
    assert L % tm == 0 and tm % CHUNK == 0
    hb = tm // PROJ_HALO
    nblk8 = L // PROJ_HALO
    whole = pl.BlockSpec(memory_space=pltpu.VMEM)
    return pl.pallas_call(
        _in_proj_body,
        grid=(L // tm,),
        in_specs=[
            pl.BlockSpec((PROJ_HALO, D_MODEL), lambda i: (jnp.maximum(i * hb - 1, 0), 0)),
            pl.BlockSpec((tm, D_MODEL), lambda i: (i, 0)),
            pl.BlockSpec((PROJ_HALO, D_MODEL), lambda i: (jnp.minimum((i + 1) * hb, nblk8 - 1), 0)),
            whole, whole, whole, whole, whole, whole, whole,
        ],
        out_specs=[
            pl.BlockSpec((tm, D_MLSTM), lambda i: (i, 0)),
            pl.BlockSpec((D_MLSTM, tm), lambda i: (0, i)),
            pl.BlockSpec((tm, D_MLSTM), lambda i: (i, 0)),
            pl.BlockSpec((tm, D_MLSTM), lambda i: (i, 0)),
            pl.BlockSpec((tm, D_S5), lambda i: (i, 0)),
            pl.BlockSpec((N_GATES, tm), lambda i: (0, i)),
            pl.BlockSpec((tm, GATE_COL_LANES), lambda i: (i, 0)),
        ],
        out_shape=[
            jax.ShapeDtypeStruct((L, D_MLSTM), BF16),
            jax.ShapeDtypeStruct((D_MLSTM, L), BF16),
            jax.ShapeDtypeStruct((L, D_MLSTM), BF16),
            jax.ShapeDtypeStruct((L, D_MLSTM), F32),
            jax.ShapeDtypeStruct((L, D_S5), BF16),
            jax.ShapeDtypeStruct((N_GATES, L), F32),
            jax.ShapeDtypeStruct((L, GATE_COL_LANES), F32),
        ],
        compiler_params=pltpu.CompilerParams(
            dimension_semantics=("parallel",), vmem_limit_bytes=VMEM_LIMIT_BYTES),
        name="in_proj",
    )(x, x, x, norm_w, w_qk, w_vou, w_gr, gate_bias, conv_w, conv_b)


def _bf16_split3(x):
    hi = x.astype(BF16)
    r1 = x - hi.astype(F32)
    mid = r1.astype(BF16)
    lo = (r1 - mid.astype(F32)).astype(BF16)
    return hi, mid, lo


def _mlstm_direction(q_ref, kt_ref, v_ref, rb_ref, col_ref, h_ref, ct_ref, m_ref, ml_ref, d, sub):
    T = CHUNK
    H = N_HEADS
    backward = d == 1
    tt = slice(sub * T, (sub + 1) * T)
    rr = lax.broadcasted_iota(jnp.int32, (T, T), 0)
    cc = lax.broadcasted_iota(jnp.int32, (T, T), 1)
    mask = (cc >= rr) if backward else (cc <= rr)
    last = 0 if backward else T - 1

    r_rows = rb_ref[d * H:(d + 1) * H, tt]
    b_rows = rb_ref[(2 + d) * H:(3 + d) * H, tt]
    b_cols = col_ref[tt, d * H:(d + 1) * H]
    cm_cols = col_ref[tt, (2 + d) * H:(3 + d) * H]
    m_prev = m_ref[d]
    m_prev_l = ml_ref[d]
    m_cols = jnp.maximum(cm_cols, m_prev_l)
    clamp_cols = -(b_cols + m_cols)
    m_last = jnp.maximum(jnp.max(r_rows, axis=1, keepdims=True), m_prev)
    b_tot = jnp.broadcast_to(b_rows[:, last:last + 1], (H, T))
    sc_rows = jnp.exp(m_prev - m_last)
    wkk_rows = jnp.exp(r_rows - m_last)
    ones = jnp.ones((T, HEAD_DIM), BF16)

    for h in range(H):
        ci = d * H + h
        hs = slice(h * HEAD_DIM, (h + 1) * HEAD_DIM)
        q = q_ref[tt, hs]
        kt = kt_ref[hs, tt]
        vaug = jnp.concatenate([v_ref[tt, hs], ones], axis=1)
        ct_prev = ct_ref[ci]
        m_col = jnp.broadcast_to(m_cols[:, h:h + 1], (T, T))
        w = jnp.exp(jnp.where(mask, r_rows[h:h + 1, :] - m_col, -jnp.inf))
        sq = jnp.dot(q, jnp.concatenate([kt, ct_prev.astype(BF16)], axis=1), preferred_element_type=F32)
        s = sq[:, :T] * w
        qc = sq[:, T:]
        kw = (kt.astype(F32) * wkk_rows[h:h + 1, :]).astype(BF16)
        both = jnp.dot(jnp.concatenate([s.astype(BF16), kw], axis=0), vaug, preferred_element_type=F32)
        sv = both[:T]
        upd = both[T:]
        s_inter = jnp.exp(m_prev[h:h + 1, :] - m_col)
        num = sv[:, :HEAD_DIM] + s_inter * qc[:, :HEAD_DIM]
        den = sv[:, HEAD_DIM:] + s_inter * qc[:, HEAD_DIM:]
        floor = jnp.exp(jnp.broadcast_to(clamp_cols[:, h:h + 1], (T, HEAD_DIM)))
        h_ref[tt, hs] = num / jnp.maximum(jnp.abs(den), floor)

        sc = sc_rows[h:h + 1, :]
        ct_ref[ci] = jnp.concatenate([sc, sc], axis=1) * ct_prev + upd

    m_ref[d] = b_tot + m_last
    ml_ref[d] = b_cols[last:last + 1, :] + m_cols[last:last + 1, :]


def _mlstm_body(qf_ref, ktf_ref, vf_ref, rbf_ref, colf_ref, qb_ref, ktb_ref, vb_ref, rbb_ref, colb_ref,
                hf_ref, hb_ref, ct_ref, m_ref, ml_ref):
    @pl.when(pl.program_id(0) == 0)
    def _():
        ct_ref[...] = jnp.zeros_like(ct_ref)
        m_ref[...] = jnp.full_like(m_ref, M_INIT)
        ml_ref[...] = jnp.full_like(ml_ref, M_INIT)

    for sub in range(MLSTM_SUB):
        _mlstm_direction(qf_ref, ktf_ref, vf_ref, rbf_ref, colf_ref, hf_ref, ct_ref, m_ref, ml_ref, 0, sub)
        _mlstm_direction(qb_ref, ktb_ref, vb_ref, rbb_ref, colb_ref, hb_ref, ct_ref, m_ref, ml_ref, 1,
                         MLSTM_SUB - 1 - sub)


def _mlstm(q, kt, v, rb, col):
    L = q.shape[0]
    T = MLSTM_SUB * CHUNK
    assert L % T == 0 and CHUNK == HEAD_DIM
    nc = L // T
    fwd = lambda c: (c, 0)
    bwd = lambda c: (nc - 1 - c, 0)
    fwd_r = lambda c: (0, c)
    bwd_r = lambda c: (0, nc - 1 - c)
    return pl.pallas_call(
        _mlstm_body,
        grid=(nc,),
        in_specs=[
            pl.BlockSpec((T, D_MLSTM), fwd), pl.BlockSpec((D_MLSTM, T), fwd_r), pl.BlockSpec((T, D_MLSTM), fwd),
            pl.BlockSpec((N_GATES, T), fwd_r), pl.BlockSpec((T, GATE_COL_LANES), fwd),
            pl.BlockSpec((T, D_MLSTM), bwd), pl.BlockSpec((D_MLSTM, T), bwd_r), pl.BlockSpec((T, D_MLSTM), bwd),
            pl.BlockSpec((N_GATES, T), bwd_r), pl.BlockSpec((T, GATE_COL_LANES), bwd),
        ],
        out_specs=[pl.BlockSpec((T, D_MLSTM), fwd), pl.BlockSpec((T, D_MLSTM), bwd)],
        out_shape=[jax.ShapeDtypeStruct((L, D_MLSTM), F32), jax.ShapeDtypeStruct((L, D_MLSTM), F32)],
        scratch_shapes=[
            pltpu.VMEM((2 * N_HEADS, HEAD_DIM, 2 * HEAD_DIM), F32),
            pltpu.VMEM((2, N_HEADS, CHUNK), F32),
            pltpu.VMEM((2, 1, N_HEADS), F32),
        ],
        compiler_params=pltpu.CompilerParams(
            dimension_semantics=("arbitrary",), vmem_limit_bytes=VMEM_LIMIT_BYTES),
        name="mlstm",
    )(q, kt, v, rb, col, q, kt, v, rb, col)


S5_PAIR_CH = 2 * S5_GC
S5_ROW = S5_BLK * S5_PAIR_CH
S5_ST = 2 * S5_P
S5_TILE = 8
S5_LANE_PAIRS = 4
S5_TT = 4096
S5_NBT = S5_TT // S5_BLK


def _s5_scan(a, sin_ref, nb):
    R = S5_TILE
    ntile = nb // R
    row = lax.broadcasted_iota(jnp.int32, (R, S5_ST), 0)
    zero = jnp.zeros((1, S5_ST), F32)

    def cmul(x, y):
        return x[0] * y[0] - x[1] * y[1], x[0] * y[1] + x[1] * y[0]

    def bcast(x):
        return tuple(jnp.broadcast_to(t, (R, S5_ST)) for t in x)

    def tables(ar, ai, backward):
        pw = {1: (ar, ai)}
        for e in range(2, R + 1):
            pw[e] = cmul(pw[e // 2], pw[e - e // 2])
        steps = []
        for sh in (1, 2, 4):
            keep = (row < R - sh) if backward else (row >= sh)
            steps.append(tuple(jnp.where(keep, t, 0.0) for t in bcast(pw[sh])))
        order = range(R, 0, -1) if backward else range(1, R + 1)
        carry_pw = tuple(jnp.concatenate([pw[e][j] for e in order], axis=0) for j in (0, 1))
        return steps, carry_pw

    def scan_tile(x, carry, steps, cpw, backward):
        for sh, am in zip((1, 2, 4), steps):
            rs = (R - sh) if backward else sh
            x = tuple(p + q for p, q in zip(x, cmul(am, (pltpu.roll(x[0], rs, 0), pltpu.roll(x[1], rs, 0)))))
        cb = bcast(carry)
        x = tuple(p + q for p, q in zip(x, cmul(cpw, cb)))
        edge, rs = (R - 1, R - 1) if backward else (0, 1)
        enter = tuple(jnp.where(row == edge, c, pltpu.roll(t, rs, 0)) for t, c in zip(x, cb))
        last = 0 if backward else R - 1
        return enter, (x[0][last:last + 1], x[1][last:last + 1])

    tabs = []
    for p in range(S5_LANE_PAIRS):
        ap = a[p]
        tabs.append((tables(ap[0:1], ap[1:2], False), tables(ap[2:3], ap[3:4], True)))

    def step(i, carry):
        rf = pl.multiple_of(i * R, R)
        rb = pl.multiple_of((ntile - 1 - i) * R, R)
        out = []
        for p in range(S5_LANE_PAIRS):
            (steps_f, cpw_f), (steps_b, cpw_b) = tabs[p]
            cf, cb = carry[p]
            xf = (sin_ref[p, pl.ds(rf, R), 0:S5_ST], sin_ref[p, pl.ds(rf, R), S5_ST:2 * S5_ST])
            xb = (sin_ref[p, pl.ds(rb, R), 2 * S5_ST:3 * S5_ST], sin_ref[p, pl.ds(rb, R), 3 * S5_ST:4 * S5_ST])
            ef, cf = scan_tile(xf, cf, steps_f, cpw_f, False)
            eb, cb = scan_tile(xb, cb, steps_b, cpw_b, True)
            sin_ref[p, pl.ds(rf, R), 0:S5_ST] = ef[0]
            sin_ref[p, pl.ds(rf, R), S5_ST:2 * S5_ST] = ef[1]
            sin_ref[p, pl.ds(rb, R), 2 * S5_ST:3 * S5_ST] = eb[0]
            sin_ref[p, pl.ds(rb, R), 3 * S5_ST:4 * S5_ST] = eb[1]
            out.append((cf, cb))
        return tuple(out)

    init = tuple(((zero, zero), (zero, zero)) for _ in range(S5_LANE_PAIRS))
    lax.fori_loop(0, ntile, step, init)


def _s5_body(u_ref, m_ref, win_ref, wout_ref, a_ref, y_ref, tok_ref, u2_ref, sin_ref, y2_ref, stage_ref):
    phase = pl.program_id(1)
    t = pl.program_id(2)
    nb = u2_ref.shape[1]
    r0 = pl.multiple_of(t * S5_NBT, S5_NBT)
    rows = pl.ds(r0, S5_NBT)

    @pl.when(phase == 0)
    def _():
        tok_ref[...] = u_ref[...].astype(F32)
        for s in range(S5_BLK):
            tok_s = tok_ref[pl.ds(s, S5_NBT, stride=S5_BLK), :]
            for p in range(S5_LANE_PAIRS):
                u2_ref[p, rows, s * S5_PAIR_CH:(s + 1) * S5_PAIR_CH] = (
                    tok_s[:, p * S5_PAIR_CH:(p + 1) * S5_PAIR_CH].astype(BF16))
        for p in range(S5_LANE_PAIRS):
            sin_ref[p, rows, :] = jnp.dot(u2_ref[p, rows, :], win_ref[p], preferred_element_type=F32)

    @pl.when((phase == 0) & (t == pl.num_programs(2) - 1))
    def _():
        _s5_scan(a_ref[...], sin_ref, nb)

    @pl.when(phase == 1)
    def _():
        for p in range(S5_LANE_PAIRS):
            y2_ref[p] = (jnp.dot(u2_ref[p, rows, :], m_ref[p], preferred_element_type=F32)
                         + jnp.dot(sin_ref[p, rows, :].astype(BF16), wout_ref[p], preferred_element_type=F32))
        for s in range(S5_BLK):
            for p in range(S5_LANE_PAIRS):
                stage_ref[:, p * S5_PAIR_CH:(p + 1) * S5_PAIR_CH] = y2_ref[p, :, s * S5_PAIR_CH:(s + 1) * S5_PAIR_CH]
            y_ref[pl.ds(s, S5_NBT, stride=S5_BLK), :] = stage_ref[...]


def _s5(u, m2, win2, wout2, a2):
    L = u.shape[0]
    assert L % S5_TT == 0 and D_S5 == S5_PAIRS * S5_PAIR_CH
    nb = L // S5_BLK
    nq = S5_PAIRS // S5_LANE_PAIRS
    lane_tile = S5_LANE_PAIRS * S5_PAIR_CH
    wspec = pl.BlockSpec((S5_LANE_PAIRS, S5_ROW, S5_ROW), lambda q, ph, t: (q, 0, 0))
    return pl.pallas_call(
        _s5_body,
        grid=(nq, 2, L // S5_TT),
        in_specs=[
            pl.BlockSpec((S5_TT, lane_tile), lambda q, ph, t: (t, q)),
            wspec, wspec, wspec,
            pl.BlockSpec((S5_LANE_PAIRS, 4, S5_ST), lambda q, ph, t: (q, 0, 0)),
        ],
        out_specs=pl.BlockSpec((S5_TT, lane_tile), lambda q, ph, t: (t * ph, q)),
        out_shape=jax.ShapeDtypeStruct((L, D_S5), F32),
        scratch_shapes=[
            pltpu.VMEM((S5_TT, lane_tile), F32),
            pltpu.VMEM((S5_LANE_PAIRS, nb, S5_ROW), BF16),
            pltpu.VMEM((S5_LANE_PAIRS, nb, 4 * S5_ST), F32),
            pltpu.VMEM((S5_LANE_PAIRS, S5_NBT, S5_ROW), F32),
            pltpu.VMEM((S5_NBT, lane_tile), F32),
        ],
        compiler_params=pltpu.CompilerParams(
            dimension_semantics=("parallel", "arbitrary", "arbitrary"), vmem_limit_bytes=VMEM_LIMIT_BYTES),
        name="s5",
    )(u, m2, win2, wout2, a2)


S5_EXP = (S5_BLK + 1) * S5_PAIR_CH
S5_EXP_PAD = 640
S5_PWT_ROWS = 24


def _s5_prep_body(pwk_ref, pwt_ref, ct_ref, b2t_ref, d_ref, m_ref, win_ref, wout_ref):
    nt, w = S5_BLK, S5_PAIR_CH
    lane = lax.broadcasted_iota(jnp.int32, (S5_ST, S5_EXP_PAD), 1)
    row = lax.broadcasted_iota(jnp.int32, (S5_ST, S5_EXP_PAD), 0)
    sel_slot = (lane // w == row).astype(BF16)
    sel_chan = ((lane % S5_GC == row) & (row < S5_GC)).astype(BF16)
    same_group = (row // S5_P == (lane // S5_GC) % 2) & (lane < S5_EXP)

    def expand(x, sel):
        return sum(jnp.dot(p, sel, preferred_element_type=F32) for p in _bf16_split3(x))

    def split_dot(a, x):
        a_hi, a_lo, _ = _bf16_split3(a)
        x_hi, x_lo, _ = _bf16_split3(x)
        return (jnp.dot(a_hi, x_hi, preferred_element_type=F32) + jnp.dot(a_hi, x_lo, preferred_element_type=F32)
                + jnp.dot(a_lo, x_hi, preferred_element_type=F32))

    lane_m = lax.broadcasted_iota(jnp.int32, (w, S5_ROW), 1)
    row_m = lax.broadcasted_iota(jnp.int32, (w, S5_ROW), 0)
    krow = []
    for z in range(2):
        pr, pi = expand(pwk_ref[z, 0, 0], sel_slot), expand(pwk_ref[z, 1, 0], sel_slot)
        cr, ci = expand(ct_ref[z, 0, 0], sel_chan), expand(ct_ref[z, 1, 0], sel_chan)
        xr = jnp.where(same_group, pr * cr - pi * ci, 0.0)
        xi = jnp.where(same_group, pr * ci + pi * cr, 0.0)
        lo = w if z == 0 else 0
        wout_ref[0, (2 * z) * S5_ST:(2 * z + 1) * S5_ST, :] = xr[:, lo:lo + S5_ROW].astype(BF16)
        wout_ref[0, (2 * z + 1) * S5_ST:(2 * z + 2) * S5_ST, :] = (-xi[:, lo:lo + S5_ROW]).astype(BF16)
        k_all = split_dot(b2t_ref[z, 0, 0], xr) - split_dot(b2t_ref[z, 1, 0], xi)
        lo = 0 if z == 0 else w
        krow.append(k_all[:, lo:lo + S5_ROW])
        br, bi = b2t_ref[z, 0, 0], b2t_ref[z, 1, 0]
        for s in range(nt):
            e = nt - 1 - s if z == 0 else s
            qr, qi = pwt_ref[z, 0, 0, e:e + 1, :], pwt_ref[z, 1, 0, e:e + 1, :]
            win_ref[0, s * w:(s + 1) * w, (2 * z) * S5_ST:(2 * z + 1) * S5_ST] = (br * qr - bi * qi).astype(BF16)
            win_ref[0, s * w:(s + 1) * w, (2 * z + 1) * S5_ST:(2 * z + 2) * S5_ST] = (br * qi + bi * qr).astype(BF16)

    kf, kb = krow
    d_diag = jnp.where((lane_m % w) == row_m, d_ref[0], 0.0)
    for s in range(nt):
        f = jnp.where(lane_m >= w * s, pltpu.roll(kf, w * s, 1), 0.0) if s else kf
        sh = w * (nt - 1 - s)
        b = jnp.where(lane_m < S5_ROW - sh, pltpu.roll(kb, S5_ROW - sh, 1), 0.0) if sh else kb
        dd = jnp.where(lane_m // w == s, d_diag, 0.0)
        m_ref[0, s * w:(s + 1) * w, :] = (f + b + dd).astype(BF16)


def _s5_prep(pwk, pwt, ct, b2t, d2):
    npair = d2.shape[0]
    spec = lambda r, c: pl.BlockSpec((2, 2, 1, r, c), lambda p: (0, 0, p, 0, 0))
    out = pl.BlockSpec((1, S5_ROW, S5_ROW), lambda p: (p, 0, 0))
    shape = jax.ShapeDtypeStruct((npair, S5_ROW, S5_ROW), BF16)
    return pl.pallas_call(
        _s5_prep_body,
        grid=(npair,),
        in_specs=[spec(S5_ST, S5_ST), spec(S5_PWT_ROWS, S5_ST), spec(S5_ST, S5_ST), spec(S5_PAIR_CH, S5_ST),
                  pl.BlockSpec((1, 1, S5_ROW), lambda p: (p, 0, 0))],
        out_specs=[out, out, out],
        out_shape=[shape, shape, shape],
        compiler_params=pltpu.CompilerParams(
            dimension_semantics=("parallel",), vmem_limit_bytes=VMEM_LIMIT_BYTES),
        name="s5_prep",
    )(pwk, pwt, ct, b2t, d2)


def _s5_weights(a_re, a_im, log_dt, b_re, b_im, c_re, c_im, d_skip):
    nt, npair = S5_BLK, S5_PAIRS
    lam = lax.complex(a_re, a_im)
    dt = jnp.exp(log_dt)[..., None]
    lam_bar = jnp.exp(lam * dt)
    b_bar = ((lam_bar - 1.0) / lam)[..., None] * lax.complex(b_re, b_im)
    taus = jnp.arange(nt + 1, dtype=F32)
    pw = jnp.exp((lam * dt)[..., None] * taus).reshape(2, npair, S5_ST, nt + 1)
    ri = lambda x: jnp.stack([jnp.real(x), jnp.imag(x)], axis=1)
    pwk = jnp.stack([pw[0], pw[1, ..., ::-1]])
    pwk = jnp.pad(ri(pwk), ((0, 0), (0, 0), (0, 0), (0, 0), (0, S5_ST - (nt + 1))))
    pwt = jnp.pad(ri(pw).transpose(0, 1, 2, 4, 3), ((0, 0), (0, 0), (0, 0), (0, S5_PWT_ROWS - (nt + 1)), (0, 0)))
    ct = lax.complex(c_re, c_im).transpose(0, 1, 3, 2).reshape(2, npair, S5_ST, S5_GC)
    ct = jnp.pad(ri(ct), ((0, 0), (0, 0), (0, 0), (0, 0), (0, S5_ST - S5_GC)))
    bb = b_bar.reshape(2, npair, 2, S5_P, S5_GC)
    eye2 = jnp.eye(2, dtype=F32)
    b2t = (bb.transpose(0, 1, 2, 4, 3)[:, :, :, :, None, :] * eye2[None, None, :, None, :, None])
    b2t = ri(b2t.reshape(2, npair, S5_PAIR_CH, S5_ST))
    d2 = jnp.tile(d_skip.reshape(npair, 1, S5_PAIR_CH), (1, 1, nt))
    m2, win2, wout2 = _s5_prep(pwk, pwt, ct, b2t, d2)
    a_blk = pw[..., nt]
    a2 = jnp.stack([jnp.real(a_blk[0]), jnp.imag(a_blk[0]), jnp.real(a_blk[1]), jnp.imag(a_blk[1])], axis=1)
    return m2, win2, wout2, a2


def _mix_body(x_ref, hf_ref, hb_ref, o_ref, y_ref, nw_ref, wglu_ref, wout_ref, out_ref):
    h = hf_ref[...] + hb_ref[...]
    parts = []
    for hd in range(N_HEADS):
        hh = h[:, hd * HEAD_DIM:(hd + 1) * HEAD_DIM]
        mu = jnp.mean(hh, axis=-1, keepdims=True)
        var = jnp.mean(jnp.square(hh - mu), axis=-1, keepdims=True)
        parts.append((hh - mu) * lax.rsqrt(var + EPS))
    hn = jnp.concatenate(parts, axis=1)
    h_m = hn * nw_ref[...] * _sigmoid(o_ref[...])
    y = y_ref[...]
    gelu = 0.5 * y * (1.0 + jnp.tanh(math.sqrt(2.0 / math.pi) * (y + 0.044715 * (y * y * y))))
    ab = jnp.dot(gelu.astype(BF16), wglu_ref[...], preferred_element_type=F32)
    h_s = ab[:, :D_S5] * _sigmoid(ab[:, D_S5:])
    mix = jnp.dot(h_m.astype(BF16), wout_ref[0:D_MLSTM, :], preferred_element_type=F32)
    mix += jnp.dot(h_s.astype(BF16), wout_ref[D_MLSTM:, :], preferred_element_type=F32)
    out_ref[...] = x_ref[...] + mix


def _mix(x, hf, hb, o_in, y, norm_w, w_glu, w_out):
    L = x.shape[0]
    tm = MIX_TM
    assert L % tm == 0
    whole = pl.BlockSpec(memory_space=pltpu.VMEM)
    row = lambda n: pl.BlockSpec((tm, n), lambda i: (i, 0))
    return pl.pallas_call(
        _mix_body,
        grid=(L // tm,),
        in_specs=[row(D_MODEL), row(D_MLSTM), row(D_MLSTM), row(D_MLSTM), row(D_S5), whole, whole, whole],
        out_specs=row(D_MODEL),
        out_shape=jax.ShapeDtypeStruct((L, D_MODEL), F32),
        compiler_params=pltpu.CompilerParams(
            dimension_semantics=("parallel",), vmem_limit_bytes=VMEM_LIMIT_BYTES),
        name="mix",
    )(x, hf, hb, o_in, y, norm_w, w_glu, w_out)


def _encode(x, p):
    x = _ffn(x, p["norm_ffn1"], p["ffn1_w_gate"], p["ffn1_w_up"], p["ffn1_w_down"], p["norm_final"],
             final_norm=False)
    q, kt, v, o_in, u, rb, col = _in_proj(x, p["norm_mix"], p["w_qk"], p["w_vou"], p["w_gr"], p["gate_bias_r"],
                                          p["conv_w"], p["conv_b"])
    hf, hb = _mlstm(q, kt, v, rb, col)
    y = _s5(u, p["s5_m"], p["s5_win"], p["s5_wout"], p["s5_a"])
    x = _mix(x, hf, hb, o_in, y, p["mlstm_norm_w"], p["s5_w_glu"], p["w_out"])
    return _ffn(x, p["norm_ffn2"], p["ffn2_w_gate"], p["ffn2_w_up"], p["ffn2_w_down"], p["norm_final"],
                final_norm=True)


def _prepare(norm_ffn1, ffn1_w_gate, ffn1_w_up, ffn1_w_down, norm_mix, w_in, conv_w, conv_b, b_igate, b_fgate,
             mlstm_norm_w, s5_a_re, s5_a_im, s5_log_dt, s5_b_re, s5_b_im, s5_c_re, s5_c_im, s5_d, s5_w_glu,
             w_out, norm_ffn2, ffn2_w_gate, ffn2_w_up, ffn2_w_down, norm_final):
    l = 0
    w = w_in[l]
    g0 = 4 * D_MLSTM
    w_g = w[:, g0:g0 + N_GATES]
    gate_bias = jnp.concatenate([b_igate[l].reshape(-1), b_fgate[l].reshape(-1)])
    m2, win2, wout2, a2 = _s5_weights(s5_a_re[l], s5_a_im[l], s5_log_dt[l], s5_b_re[l], s5_b_im[l],
                                      s5_c_re[l], s5_c_im[l], s5_d[l])
    row = lambda a: a.reshape(1, -1).astype(F32)
    return {
        "norm_ffn1": row(norm_ffn1[l]), "norm_ffn2": row(norm_ffn2[l]), "norm_final": row(norm_final),
        "ffn1_w_gate": ffn1_w_gate[l].astype(BF16), "ffn1_w_up": ffn1_w_up[l].astype(BF16),
        "ffn1_w_down": ffn1_w_down[l].astype(BF16),
        "ffn2_w_gate": ffn2_w_gate[l].astype(BF16), "ffn2_w_up": ffn2_w_up[l].astype(BF16),
        "ffn2_w_down": ffn2_w_down[l].astype(BF16),
        "norm_mix": row(norm_mix[l]),
        "w_qk": w[:, :QK_COLS].astype(BF16),
        "w_vou": jnp.concatenate([w[:, QK_COLS:g0], w[:, g0 + N_GATES:]], axis=1).astype(BF16),
        "w_gr": w_g.T.astype(BF16),
        "conv_w": conv_w[l].astype(F32), "conv_b": row(conv_b[l]),
        "gate_bias_r": gate_bias.reshape(N_GATES, 1),
        "mlstm_norm_w": row(mlstm_norm_w[l]),
        "s5_m": m2, "s5_win": win2, "s5_wout": wout2, "s5_a": a2,
        "s5_w_glu": s5_w_glu[l].astype(BF16), "w_out": w_out[l].astype(BF16),
    }


def kernel(x_prompt, x_sample, norm_ffn1, ffn1_w_gate, ffn1_w_up, ffn1_w_down, norm_mix, w_in, conv_w, conv_b, b_igate, b_fgate, mlstm_norm_w, s5_a_re, s5_a_im, s5_log_dt, s5_b_re, s5_b_im, s5_c_re, s5_c_im, s5_d, s5_w_glu, w_out, norm_ffn2, ffn2_w_gate, ffn2_w_up, ffn2_w_down, norm_final):
    assert norm_ffn1.shape[0] == 1 and x_prompt.shape[0] == 1 and x_sample.shape[0] == 1
    p = _prepare(norm_ffn1, ffn1_w_gate, ffn1_w_up, ffn1_w_down, norm_mix, w_in, conv_w, conv_b, b_igate,
                 b_fgate, mlstm_norm_w, s5_a_re, s5_a_im, s5_log_dt, s5_b_re, s5_b_im, s5_c_re, s5_c_im, s5_d,
                 s5_w_glu, w_out, norm_ffn2, ffn2_w_gate, ffn2_w_up, ffn2_w_down, norm_final)
    y_prompt = _encode(x_prompt[0], p)[None]
    y_sample = _encode(x_sample[0], p)[None]
    return (y_prompt, y_sample)
```

```python
import functools
import math

import jax
import jax.numpy as jnp
from jax import lax
from jax.experimental import pallas as pl
from jax.experimental.pallas import tpu as pltpu

F32 = jnp.float32
BF16 = jnp.bfloat16

D_MODEL = 2048
D_MLSTM = 1024
D_S5 = 1024
N_HEADS = 8
HEAD_DIM = 128
CHUNK = 128
MLSTM_SUB = 4
N_GATES = 32
GATE_COL_LANES = 128
S5_GROUPS = 64
S5_GC = 16
S5_P = 64
S5_BLK = 16
S5_PAIRS = S5_GROUPS // 2
D_FF = 5632
EPS = 1e-6
M_INIT = -1e30

VMEM_LIMIT_BYTES = 56 * 1024 * 1024

FFN_TM = 1024
FFN_TF = 512
PROJ_TM = 256
PROJ_HALO = 8
MIX_TM = 512


def _sigmoid(x):
    return 1.0 / (1.0 + jnp.exp(-x))


def _rmsnorm(x, w):
    return x * lax.rsqrt(jnp.mean(x * x, axis=-1, keepdims=True) + EPS) * w


def _ffn_body(x_ref, nw_ref, wg_ref, wu_ref, wd_ref, nf_ref, o_ref, xn_ref, *, final_norm):
    j = pl.program_id(1)

    @pl.when(j == 0)
    def _():
        x = x_ref[...]
        xn_ref[...] = _rmsnorm(x, nw_ref[...]).astype(BF16)
        o_ref[...] = x

    xn = xn_ref[...]
    g = jnp.dot(xn, wg_ref[...], preferred_element_type=F32)
    u = jnp.dot(xn, wu_ref[...], preferred_element_type=F32)
    h = (0.5 * g * _sigmoid(g)) * u
    o_ref[...] += jnp.dot(h.astype(BF16), wd_ref[...], preferred_element_type=F32)

    if final_norm:
        @pl.when(j == pl.num_programs(1) - 1)
        def _():
            o_ref[...] = _rmsnorm(o_ref[...], nf_ref[...])


def _ffn(x, norm_w, w_gate, w_up, w_down, norm_final, *, final_norm):
    L = x.shape[0]
    assert L % FFN_TM == 0 and D_FF % FFN_TF == 0
    return pl.pallas_call(
        functools.partial(_ffn_body, final_norm=final_norm),
        grid=(L // FFN_TM, D_FF // FFN_TF),
        in_specs=[
            pl.BlockSpec((FFN_TM, D_MODEL), lambda i, j: (i, 0)),
            pl.BlockSpec((1, D_MODEL), lambda i, j: (0, 0)),
            pl.BlockSpec((D_MODEL, FFN_TF), lambda i, j: (0, j)),
            pl.BlockSpec((D_MODEL, FFN_TF), lambda i, j: (0, j)),
            pl.BlockSpec((FFN_TF, D_MODEL), lambda i, j: (j, 0)),
            pl.BlockSpec((1, D_MODEL), lambda i, j: (0, 0)),
        ],
        out_specs=pl.BlockSpec((FFN_TM, D_MODEL), lambda i, j: (i, 0)),
        out_shape=jax.ShapeDtypeStruct((L, D_MODEL), F32),
        scratch_shapes=[pltpu.VMEM((FFN_TM, D_MODEL), BF16)],
        compiler_params=pltpu.CompilerParams(
            dimension_semantics=("parallel", "arbitrary"), vmem_limit_bytes=VMEM_LIMIT_BYTES),
        name="ffn_final" if final_norm else "ffn",
    )(x, norm_w, w_gate, w_up, w_down, norm_final)


QK_COLS = 2 * D_MLSTM
QK_CB = 512


def _chunk_scan(x, op, identity, backward):
    n = x.shape[-1]
    pos = lax.broadcasted_iota(jnp.int32, x.shape, x.ndim - 1) % CHUNK
    sh = 1
    while sh < CHUNK:
        if backward:
            x = op(x, jnp.where(pos < CHUNK - sh, pltpu.roll(x, n - sh, x.ndim - 1), identity))
        else:
            x = op(x, jnp.where(pos >= sh, pltpu.roll(x, sh, x.ndim - 1), identity))
        sh *= 2
    return x


def _in_proj_body(xp_ref, x_ref, xnx_ref, nw_ref, w_ref, wu_ref, wgr_ref, gb_ref,
                  cw_ref, cb_ref, q_ref, kt_ref, v_ref, o_ref, u_ref, rb_ref, col_ref):
    i = pl.program_id(0)
    tm = x_ref.shape[0]
    nw = nw_ref[...]
    xn = _rmsnorm(x_ref[...], nw)
    xn_prev = jnp.where(i == 0, 0.0, _rmsnorm(xp_ref[...], nw))
    xn_next = jnp.where(i == pl.num_programs(0) - 1, 0.0, _rmsnorm(xnx_ref[...], nw))
    xn_b = xn.astype(BF16)
    xe_b = jnp.concatenate([xn_prev, xn, xn_next], axis=0).astype(BF16)
    rows = tm + 2 * PROJ_HALO

    g = lax.dot_general(wgr_ref[...], xn_b, (((1,), (1,)), ((), ())), preferred_element_type=F32) + gb_ref[...]
    H = N_HEADS
    f_pre = g[2 * H:]
    lf = jnp.minimum(f_pre, 0.0) - jnp.log1p(jnp.exp(-jnp.abs(f_pre)))
    b_f = _chunk_scan(lf[:H], jnp.add, 0.0, False)
    b_b = _chunk_scan(lf[H:], jnp.add, 0.0, True)
    r_f = g[:H] - b_f
    r_b = g[H:2 * H] - b_b
    cm_f = _chunk_scan(r_f, jnp.maximum, -jnp.inf, False)
    cm_b = _chunk_scan(r_b, jnp.maximum, -jnp.inf, True)

    for c in range(QK_COLS // QK_CB):
        cs = slice(c * QK_CB, (c + 1) * QK_CB)
        z = jnp.dot(xe_b, w_ref[:, cs], preferred_element_type=F32)
        z_m1 = pltpu.roll(z, 1, 0)[PROJ_HALO:PROJ_HALO + tm]
        z_0 = z[PROJ_HALO:PROJ_HALO + tm]
        z_p1 = pltpu.roll(z, rows - 1, 0)[PROJ_HALO:PROJ_HALO + tm]
        y = z_m1 * cw_ref[0:1, cs] + z_0 * cw_ref[1:2, cs] + z_p1 * cw_ref[2:3, cs] + cb_ref[:, cs]
        y = y * _sigmoid(y)
        if c * QK_CB < D_MLSTM:
            q_ref[:, cs] = (y * (HEAD_DIM ** -0.5)).astype(q_ref.dtype)
        else:
            kt_ref[c * QK_CB - D_MLSTM:(c + 1) * QK_CB - D_MLSTM, :] = y.T.astype(kt_ref.dtype)

    v_ref[...] = jnp.dot(xn_b, w_ref[:, QK_COLS:QK_COLS + D_MLSTM], preferred_element_type=F32).astype(v_ref.dtype)
    o_ref[...] = jnp.dot(xn_b, w_ref[:, QK_COLS + D_MLSTM:QK_COLS + 2 * D_MLSTM], preferred_element_type=F32)
    u_ref[...] = jnp.dot(xn_b, wu_ref[...], preferred_element_type=F32).astype(u_ref.dtype)
    rb_ref[...] = jnp.concatenate([r_f, r_b, b_f, b_b], axis=0)
    pad = jnp.zeros((GATE_COL_LANES - 4 * H, tm), F32)
    col_ref[...] = jnp.concatenate([b_f, b_b, cm_f, cm_b, pad], axis=0).T


def _in_proj(x, norm_w, w_all, w_u, w_gr, gate_bias, conv_w, conv_b):
    L = x.shape[0]
    tm = PROJ_TM
    assert L % tm == 0 and tm % CHUNK == 0
    hb = tm // PROJ_HALO
    nblk8 = L // PROJ_HALO
    whole = pl.BlockSpec(memory_space=pltpu.VMEM)
    return pl.pallas_call(
        _in_proj_body,
        grid=(L // tm,),
        in_specs=[
            pl.BlockSpec((PROJ_HALO, D_MODEL), lambda i: (jnp.maximum(i * hb - 1, 0), 0)),
            pl.BlockSpec((tm, D_MODEL), lambda i: (i, 0)),
            pl.BlockSpec((PROJ_HALO, D_MODEL), lambda i: (jnp.minimum((i + 1) * hb, nblk8 - 1), 0)),
            whole, whole, whole, whole, whole, whole, whole,
        ],
        out_specs=[
            pl.BlockSpec((tm, D_MLSTM), lambda i: (i, 0)),
            pl.BlockSpec((D_MLSTM, tm), lambda i: (0, i)),
            pl.BlockSpec((tm, D_MLSTM), lambda i: (i, 0)),
            pl.BlockSpec((tm, D_MLSTM), lambda i: (i, 0)),
            pl.BlockSpec((tm, D_S5), lambda i: (i, 0)),
            pl.BlockSpec((N_GATES, tm), lambda i: (0, i)),
            pl.BlockSpec((tm, GATE_COL_LANES), lambda i: (i, 0)),
        ],
        out_shape=[
            jax.ShapeDtypeStruct((L, D_MLSTM), BF16),
            jax.ShapeDtypeStruct((D_MLSTM, L), BF16),
            jax.ShapeDtypeStruct((L, D_MLSTM), BF16),
            jax.ShapeDtypeStruct((L, D_MLSTM), F32),
            jax.ShapeDtypeStruct((L, D_S5), BF16),
            jax.ShapeDtypeStruct((N_GATES, L), F32),
            jax.ShapeDtypeStruct((L, GATE_COL_LANES), F32),
        ],
        compiler_params=pltpu.CompilerParams(
            dimension_semantics=("parallel",), vmem_limit_bytes=VMEM_LIMIT_BYTES),
        name="in_proj",
    )(x, x, x, norm_w, w_all, w_u, w_gr, gate_bias, conv_w, conv_b)


def _bf16_split3(x):
    hi = x.astype(BF16)
    r1 = x - hi.astype(F32)
    mid = r1.astype(BF16)
    lo = (r1 - mid.astype(F32)).astype(BF16)
    return hi, mid, lo


def _mlstm_direction(q_ref, kt_ref, v_ref, rb_ref, col_ref, h_ref, ct_ref, m_ref, ml_ref, d, sub):
    T = CHUNK
    H = N_HEADS
    backward = d == 1
    tt = slice(sub * T, (sub + 1) * T)
    rr = lax.broadcasted_iota(jnp.int32, (T, T), 0)
    cc = lax.broadcasted_iota(jnp.int32, (T, T), 1)
    mask = (cc >= rr) if backward else (cc <= rr)
    last = 0 if backward else T - 1

    r_rows = rb_ref[d * H:(d + 1) * H, tt]
    b_rows = rb_ref[(2 + d) * H:(3 + d) * H, tt]
    b_cols = col_ref[tt, d * H:(d + 1) * H]
    cm_cols = col_ref[tt, (2 + d) * H:(3 + d) * H]
    m_prev = m_ref[d]
    m_prev_l = ml_ref[d]
    m_cols = jnp.maximum(cm_cols, m_prev_l)
    clamp_cols = -(b_cols + m_cols)
    m_last = jnp.maximum(jnp.max(r_rows, axis=1, keepdims=True), m_prev)
    b_tot = jnp.broadcast_to(b_rows[:, last:last + 1], (H, T))
    sc_rows = jnp.exp(m_prev - m_last)
    wkk_rows = jnp.exp(r_rows - m_last)
    ones = jnp.ones((T, HEAD_DIM), BF16)

    for h in range(H):
        ci = d * H + h
        hs = slice(h * HEAD_DIM, (h + 1) * HEAD_DIM)
        q = q_ref[tt, hs]
        kt = kt_ref[hs, tt]
        vaug = jnp.concatenate([v_ref[tt, hs], ones], axis=1)
        ct_prev = ct_ref[ci]
        m_col = jnp.broadcast_to(m_cols[:, h:h + 1], (T, T))
        w = jnp.exp(jnp.where(mask, r_rows[h:h + 1, :] - m_col, -jnp.inf))
        sq = jnp.dot(q, jnp.concatenate([kt, ct_prev.astype(BF16)], axis=1), preferred_element_type=F32)
        s = sq[:, :T] * w
        qc = sq[:, T:]
        kw = (kt.astype(F32) * wkk_rows[h:h + 1, :]).astype(BF16)
        both = jnp.dot(jnp.concatenate([s.astype(BF16), kw], axis=0), vaug, preferred_element_type=F32)
        sv = both[:T]
        upd = both[T:]
        s_inter = jnp.exp(m_prev[h:h + 1, :] - m_col)
        num = sv[:, :HEAD_DIM] + s_inter * qc[:, :HEAD_DIM]
        den = sv[:, HEAD_DIM:] + s_inter * qc[:, HEAD_DIM:]
        floor = jnp.exp(jnp.broadcast_to(clamp_cols[:, h:h + 1], (T, HEAD_DIM)))
        h_ref[tt, hs] = num / jnp.maximum(jnp.abs(den), floor)

        sc = sc_rows[h:h + 1, :]
        ct_ref[ci] = jnp.concatenate([sc, sc], axis=1) * ct_prev + upd

    m_ref[d] = b_tot + m_last
    ml_ref[d] = b_cols[last:last + 1, :] + m_cols[last:last + 1, :]


def _mlstm_body(qf_ref, ktf_ref, vf_ref, rbf_ref, colf_ref, qb_ref, ktb_ref, vb_ref, rbb_ref, colb_ref,
                hf_ref, hb_ref, ct_ref, m_ref, ml_ref):
    @pl.when(pl.program_id(0) == 0)
    def _():
        ct_ref[...] = jnp.zeros_like(ct_ref)
        m_ref[...] = jnp.full_like(m_ref, M_INIT)
        ml_ref[...] = jnp.full_like(ml_ref, M_INIT)

    for sub in range(MLSTM_SUB):
        _mlstm_direction(qf_ref, ktf_ref, vf_ref, rbf_ref, colf_ref, hf_ref, ct_ref, m_ref, ml_ref, 0, sub)
        _mlstm_direction(qb_ref, ktb_ref, vb_ref, rbb_ref, colb_ref, hb_ref, ct_ref, m_ref, ml_ref, 1,
                         MLSTM_SUB - 1 - sub)


def _mlstm(q, kt, v, rb, col):
    L = q.shape[0]
    T = MLSTM_SUB * CHUNK
    assert L % T == 0 and CHUNK == HEAD_DIM
    nc = L // T
    fwd = lambda c: (c, 0)
    bwd = lambda c: (nc - 1 - c, 0)
    fwd_r = lambda c: (0, c)
    bwd_r = lambda c: (0, nc - 1 - c)
    return pl.pallas_call(
        _mlstm_body,
        grid=(nc,),
        in_specs=[
            pl.BlockSpec((T, D_MLSTM), fwd), pl.BlockSpec((D_MLSTM, T), fwd_r), pl.BlockSpec((T, D_MLSTM), fwd),
            pl.BlockSpec((N_GATES, T), fwd_r), pl.BlockSpec((T, GATE_COL_LANES), fwd),
            pl.BlockSpec((T, D_MLSTM), bwd), pl.BlockSpec((D_MLSTM, T), bwd_r), pl.BlockSpec((T, D_MLSTM), bwd),
            pl.BlockSpec((N_GATES, T), bwd_r), pl.BlockSpec((T, GATE_COL_LANES), bwd),
        ],
        out_specs=[pl.BlockSpec((T, D_MLSTM), fwd), pl.BlockSpec((T, D_MLSTM), bwd)],
        out_shape=[jax.ShapeDtypeStruct((L, D_MLSTM), F32), jax.ShapeDtypeStruct((L, D_MLSTM), F32)],
        scratch_shapes=[
            pltpu.VMEM((2 * N_HEADS, HEAD_DIM, 2 * HEAD_DIM), F32),
            pltpu.VMEM((2, N_HEADS, CHUNK), F32),
            pltpu.VMEM((2, 1, N_HEADS), F32),
        ],
        compiler_params=pltpu.CompilerParams(
            dimension_semantics=("arbitrary",), vmem_limit_bytes=VMEM_LIMIT_BYTES),
        name="mlstm",
    )(q, kt, v, rb, col, q, kt, v, rb, col)


S5_PAIR_CH = 2 * S5_GC
S5_ROW = S5_BLK * S5_PAIR_CH
S5_ST = 2 * S5_P
S5_TILE = 8
S5_LANE_PAIRS = 4
S5_TT = 4096
S5_NBT = S5_TT // S5_BLK


def _s5_scan(a, sin_ref, nb):
    R = S5_TILE
    ntile = nb // R
    row = lax.broadcasted_iota(jnp.int32, (R, S5_ST), 0)
    zero = jnp.zeros((1, S5_ST), F32)

    def cmul(x, y):
        return x[0] * y[0] - x[1] * y[1], x[0] * y[1] + x[1] * y[0]

    def bcast(x):
        return tuple(jnp.broadcast_to(t, (R, S5_ST)) for t in x)

    def tables(ar, ai, backward):
        pw = {1: (ar, ai)}
        for e in range(2, R + 1):
            pw[e] = cmul(pw[e // 2], pw[e - e // 2])
        steps = []
        for sh in (1, 2, 4):
            keep = (row < R - sh) if backward else (row >= sh)
            steps.append(tuple(jnp.where(keep, t, 0.0) for t in bcast(pw[sh])))
        order = range(R, 0, -1) if backward else range(1, R + 1)
        carry_pw = tuple(jnp.concatenate([pw[e][j] for e in order], axis=0) for j in (0, 1))
        return steps, carry_pw

    def scan_tile(x, carry, steps, cpw, backward):
        for sh, am in zip((1, 2, 4), steps):
            rs = (R - sh) if backward else sh
            x = tuple(p + q for p, q in zip(x, cmul(am, (pltpu.roll(x[0], rs, 0), pltpu.roll(x[1], rs, 0)))))
        cb = bcast(carry)
        x = tuple(p + q for p, q in zip(x, cmul(cpw, cb)))
        edge, rs = (R - 1, R - 1) if backward else (0, 1)
        enter = tuple(jnp.where(row == edge, c, pltpu.roll(t, rs, 0)) for t, c in zip(x, cb))
        last = 0 if backward else R - 1
        return enter, (x[0][last:last + 1], x[1][last:last + 1])

    tabs = []
    for p in range(S5_LANE_PAIRS):
        ap = a[p]
        tabs.append((tables(ap[0:1], ap[1:2], False), tables(ap[2:3], ap[3:4], True)))

    def step(i, carry):
        rf = pl.multiple_of(i * R, R)
        rb = pl.multiple_of((ntile - 1 - i) * R, R)
        out = []
        for p in range(S5_LANE_PAIRS):
            (steps_f, cpw_f), (steps_b, cpw_b) = tabs[p]
            cf, cb = carry[p]
            xf = (sin_ref[p, pl.ds(rf, R), 0:S5_ST], sin_ref[p, pl.ds(rf, R), S5_ST:2 * S5_ST])
            xb = (sin_ref[p, pl.ds(rb, R), 2 * S5_ST:3 * S5_ST], sin_ref[p, pl.ds(rb, R), 3 * S5_ST:4 * S5_ST])
            ef, cf = scan_tile(xf, cf, steps_f, cpw_f, False)
            eb, cb = scan_tile(xb, cb, steps_b, cpw_b, True)
            sin_ref[p, pl.ds(rf, R), 0:S5_ST] = ef[0]
            sin_ref[p, pl.ds(rf, R), S5_ST:2 * S5_ST] = ef[1]
            sin_ref[p, pl.ds(rb, R), 2 * S5_ST:3 * S5_ST] = eb[0]
            sin_ref[p, pl.ds(rb, R), 3 * S5_ST:4 * S5_ST] = eb[1]
            out.append((cf, cb))
        return tuple(out)

    init = tuple(((zero, zero), (zero, zero)) for _ in range(S5_LANE_PAIRS))
    lax.fori_loop(0, ntile, step, init)


def _s5_body(u_ref, m_ref, win_ref, wout_ref, a_ref, y_ref, tok_ref, u2_ref, sin_ref, y2_ref, stage_ref):
    phase = pl.program_id(1)
    t = pl.program_id(2)
    nb = u2_ref.shape[1]
    r0 = pl.multiple_of(t * S5_NBT, S5_NBT)
    rows = pl.ds(r0, S5_NBT)

    @pl.when(phase == 0)
    def _():
        tok_ref[...] = u_ref[...].astype(F32)
        for s in range(S5_BLK):
            tok_s = tok_ref[pl.ds(s, S5_NBT, stride=S5_BLK), :]
            for p in range(S5_LANE_PAIRS):
                u2_ref[p, rows, s * S5_PAIR_CH:(s + 1) * S5_PAIR_CH] = (
                    tok_s[:, p * S5_PAIR_CH:(p + 1) * S5_PAIR_CH].astype(BF16))
        for p in range(S5_LANE_PAIRS):
            sin_ref[p, rows, :] = jnp.dot(u2_ref[p, rows, :], win_ref[p], preferred_element_type=F32)

    @pl.when((phase == 0) & (t == pl.num_programs(2) - 1))
    def _():
        _s5_scan(a_ref[...], sin_ref, nb)

    @pl.when(phase == 1)
    def _():
        for p in range(S5_LANE_PAIRS):
            y2_ref[p] = (jnp.dot(u2_ref[p, rows, :], m_ref[p], preferred_element_type=F32)
                         + jnp.dot(sin_ref[p, rows, :].astype(BF16), wout_ref[p], preferred_element_type=F32))
        for s in range(S5_BLK):
            for p in range(S5_LANE_PAIRS):
                stage_ref[:, p * S5_PAIR_CH:(p + 1) * S5_PAIR_CH] = y2_ref[p, :, s * S5_PAIR_CH:(s + 1) * S5_PAIR_CH]
            y_ref[pl.ds(s, S5_NBT, stride=S5_BLK), :] = stage_ref[...]


def _s5(u, m2, win2, wout2, a2):
    L = u.shape[0]
    assert L % S5_TT == 0 and D_S5 == S5_PAIRS * S5_PAIR_CH
    nb = L // S5_BLK
    nq = S5_PAIRS // S5_LANE_PAIRS
    lane_tile = S5_LANE_PAIRS * S5_PAIR_CH
    wspec = pl.BlockSpec((S5_LANE_PAIRS, S5_ROW, S5_ROW), lambda q, ph, t: (q, 0, 0))
    return pl.pallas_call(
        _s5_body,
        grid=(nq, 2, L // S5_TT),
        in_specs=[
            pl.BlockSpec((S5_TT, lane_tile), lambda q, ph, t: (t, q)),
            wspec, wspec, wspec,
            pl.BlockSpec((S5_LANE_PAIRS, 4, S5_ST), lambda q, ph, t: (q, 0, 0)),
        ],
        out_specs=pl.BlockSpec((S5_TT, lane_tile), lambda q, ph, t: (t * ph, q)),
        out_shape=jax.ShapeDtypeStruct((L, D_S5), F32),
        scratch_shapes=[
            pltpu.VMEM((S5_TT, lane_tile), F32),
            pltpu.VMEM((S5_LANE_PAIRS, nb, S5_ROW), BF16),
            pltpu.VMEM((S5_LANE_PAIRS, nb, 4 * S5_ST), F32),
            pltpu.VMEM((S5_LANE_PAIRS, S5_NBT, S5_ROW), F32),
            pltpu.VMEM((S5_NBT, lane_tile), F32),
        ],
        compiler_params=pltpu.CompilerParams(
            dimension_semantics=("parallel", "arbitrary", "arbitrary"), vmem_limit_bytes=VMEM_LIMIT_BYTES),
        name="s5",
    )(u, m2, win2, wout2, a2)


S5_EXP = (S5_BLK + 1) * S5_PAIR_CH
S5_EXP_PAD = 640
S5_PWT_ROWS = 24


def _s5_prep_body(pwk_ref, pwt_ref, ct_ref, b2t_ref, d_ref, m_ref, win_ref, wout_ref):
    nt, w = S5_BLK, S5_PAIR_CH
    lane = lax.broadcasted_iota(jnp.int32, (S5_ST, S5_EXP_PAD), 1)
    row = lax.broadcasted_iota(jnp.int32, (S5_ST, S5_EXP_PAD), 0)
    sel_slot = (lane // w == row).astype(BF16)
    sel_chan = ((lane % S5_GC == row) & (row < S5_GC)).astype(BF16)
    same_group = (row // S5_P == (lane // S5_GC) % 2) & (lane < S5_EXP)

    def expand(x, sel):
        return sum(jnp.dot(p, sel, preferred_element_type=F32) for p in _bf16_split3(x))

    def split_dot(a, x):
        a_hi, a_lo, _ = _bf16_split3(a)
        x_hi, x_lo, _ = _bf16_split3(x)
        return (jnp.dot(a_hi, x_hi, preferred_element_type=F32) + jnp.dot(a_hi, x_lo, preferred_element_type=F32)
                + jnp.dot(a_lo, x_hi, preferred_element_type=F32))

    lane_m = lax.broadcasted_iota(jnp.int32, (w, S5_ROW), 1)
    row_m = lax.broadcasted_iota(jnp.int32, (w, S5_ROW), 0)
    krow = []
    for z in range(2):
        pr, pi = expand(pwk_ref[z, 0, 0], sel_slot), expand(pwk_ref[z, 1, 0], sel_slot)
        cr, ci = expand(ct_ref[z, 0, 0], sel_chan), expand(ct_ref[z, 1, 0], sel_chan)
        xr = jnp.where(same_group, pr * cr - pi * ci, 0.0)
        xi = jnp.where(same_group, pr * ci + pi * cr, 0.0)
        lo = w if z == 0 else 0
        wout_ref[0, (2 * z) * S5_ST:(2 * z + 1) * S5_ST, :] = xr[:, lo:lo + S5_ROW].astype(BF16)
        wout_ref[0, (2 * z + 1) * S5_ST:(2 * z + 2) * S5_ST, :] = (-xi[:, lo:lo + S5_ROW]).astype(BF16)
        k_all = split_dot(b2t_ref[z, 0, 0], xr) - split_dot(b2t_ref[z, 1, 0], xi)
        lo = 0 if z == 0 else w
        krow.append(k_all[:, lo:lo + S5_ROW])
        br, bi = b2t_ref[z, 0, 0], b2t_ref[z, 1, 0]
        for s in range(nt):
            e = nt - 1 - s if z == 0 else s
            qr, qi = pwt_ref[z, 0, 0, e:e + 1, :], pwt_ref[z, 1, 0, e:e + 1, :]
            win_ref[0, s * w:(s + 1) * w, (2 * z) * S5_ST:(2 * z + 1) * S5_ST] = (br * qr - bi * qi).astype(BF16)
            win_ref[0, s * w:(s + 1) * w, (2 * z + 1) * S5_ST:(2 * z + 2) * S5_ST] = (br * qi + bi * qr).astype(BF16)

    kf, kb = krow
    d_diag = jnp.where((lane_m % w) == row_m, d_ref[0], 0.0)
    for s in range(nt):
        f = jnp.where(lane_m >= w * s, pltpu.roll(kf, w * s, 1), 0.0) if s else kf
        sh = w * (nt - 1 - s)
        b = jnp.where(lane_m < S5_ROW - sh, pltpu.roll(kb, S5_ROW - sh, 1), 0.0) if sh else kb
        dd = jnp.where(lane_m // w == s, d_diag, 0.0)
        m_ref[0, s * w:(s + 1) * w, :] = (f + b + dd).astype(BF16)


def _s5_prep(pwk, pwt, ct, b2t, d2):
    npair = d2.shape[0]
    spec = lambda r, c: pl.BlockSpec((2, 2, 1, r, c), lambda p: (0, 0, p, 0, 0))
    out = pl.BlockSpec((1, S5_ROW, S5_ROW), lambda p: (p, 0, 0))
    shape = jax.ShapeDtypeStruct((npair, S5_ROW, S5_ROW), BF16)
    return pl.pallas_call(
        _s5_prep_body,
        grid=(npair,),
        in_specs=[spec(S5_ST, S5_ST), spec(S5_PWT_ROWS, S5_ST), spec(S5_ST, S5_ST), spec(S5_PAIR_CH, S5_ST),
                  pl.BlockSpec((1, 1, S5_ROW), lambda p: (p, 0, 0))],
        out_specs=[out, out, out],
        out_shape=[shape, shape, shape],
        compiler_params=pltpu.CompilerParams(
            dimension_semantics=("parallel",), vmem_limit_bytes=VMEM_LIMIT_BYTES),
        name="s5_prep",
    )(pwk, pwt, ct, b2t, d2)


def _s5_weights(a_re, a_im, log_dt, b_re, b_im, c_re, c_im, d_skip):
    nt, npair = S5_BLK, S5_PAIRS
    lam = lax.complex(a_re, a_im)
    dt = jnp.exp(log_dt)[..., None]
    lam_bar = jnp.exp(lam * dt)
    b_bar = ((lam_bar - 1.0) / lam)[..., None] * lax.complex(b_re, b_im)
    taus = jnp.arange(nt + 1, dtype=F32)
    pw = jnp.exp((lam * dt)[..., None] * taus).reshape(2, npair, S5_ST, nt + 1)
    ri = lambda x: jnp.stack([jnp.real(x), jnp.imag(x)], axis=1)
    pwk = jnp.stack([pw[0], pw[1, ..., ::-1]])
    pwk = jnp.pad(ri(pwk), ((0, 0), (0, 0), (0, 0), (0, 0), (0, S5_ST - (nt + 1))))
    pwt = jnp.pad(ri(pw).transpose(0, 1, 2, 4, 3), ((0, 0), (0, 0), (0, 0), (0, S5_PWT_ROWS - (nt + 1)), (0, 0)))
    ct = lax.complex(c_re, c_im).transpose(0, 1, 3, 2).reshape(2, npair, S5_ST, S5_GC)
    ct = jnp.pad(ri(ct), ((0, 0), (0, 0), (0, 0), (0, 0), (0, S5_ST - S5_GC)))
    bb = b_bar.reshape(2, npair, 2, S5_P, S5_GC)
    eye2 = jnp.eye(2, dtype=F32)
    b2t = (bb.transpose(0, 1, 2, 4, 3)[:, :, :, :, None, :] * eye2[None, None, :, None, :, None])
    b2t = ri(b2t.reshape(2, npair, S5_PAIR_CH, S5_ST))
    d2 = jnp.tile(d_skip.reshape(npair, 1, S5_PAIR_CH), (1, 1, nt))
    m2, win2, wout2 = _s5_prep(pwk, pwt, ct, b2t, d2)
    a_blk = pw[..., nt]
    a2 = jnp.stack([jnp.real(a_blk[0]), jnp.imag(a_blk[0]), jnp.real(a_blk[1]), jnp.imag(a_blk[1])], axis=1)
    return m2, win2, wout2, a2


def _mix_body(x_ref, hf_ref, hb_ref, o_ref, y_ref, nw_ref, wglu_ref, wout_ref, out_ref):
    h = hf_ref[...] + hb_ref[...]
    parts = []
    for hd in range(N_HEADS):
        hh = h[:, hd * HEAD_DIM:(hd + 1) * HEAD_DIM]
        mu = jnp.mean(hh, axis=-1, keepdims=True)
        var = jnp.mean(jnp.square(hh - mu), axis=-1, keepdims=True)
        parts.append((hh - mu) * lax.rsqrt(var + EPS))
    hn = jnp.concatenate(parts, axis=1)
    h_m = hn * nw_ref[...] * _sigmoid(o_ref[...])
    y = y_ref[...]
    gelu = 0.5 * y * (1.0 + jnp.tanh(math.sqrt(2.0 / math.pi) * (y + 0.044715 * (y * y * y))))
    ab = jnp.dot(gelu.astype(BF16), wglu_ref[...], preferred_element_type=F32)
    h_s = ab[:, :D_S5] * _sigmoid(ab[:, D_S5:])
    mix = jnp.dot(h_m.astype(BF16), wout_ref[0:D_MLSTM, :], preferred_element_type=F32)
    mix += jnp.dot(h_s.astype(BF16), wout_ref[D_MLSTM:, :], preferred_element_type=F32)
    out_ref[...] = x_ref[...] + mix


def _mix(x, hf, hb, o_in, y, norm_w, w_glu, w_out):
    L = x.shape[0]
    tm = MIX_TM
    assert L % tm == 0
    whole = pl.BlockSpec(memory_space=pltpu.VMEM)
    row = lambda n: pl.BlockSpec((tm, n), lambda i: (i, 0))
    return pl.pallas_call(
        _mix_body,
        grid=(L // tm,),
        in_specs=[row(D_MODEL), row(D_MLSTM), row(D_MLSTM), row(D_MLSTM), row(D_S5), whole, whole, whole],
        out_specs=row(D_MODEL),
        out_shape=jax.ShapeDtypeStruct((L, D_MODEL), F32),
        compiler_params=pltpu.CompilerParams(
            dimension_semantics=("parallel",), vmem_limit_bytes=VMEM_LIMIT_BYTES),
        name="mix",
    )(x, hf, hb, o_in, y, norm_w, w_glu, w_out)


def _encode(x, p):
    x = _ffn(x, p["norm_ffn1"], p["ffn1_w_gate"], p["ffn1_w_up"], p["ffn1_w_down"], p["norm_final"],
             final_norm=False)
    q, kt, v, o_in, u, rb, col = _in_proj(x, p["norm_mix"], p["w_all"], p["w_u"], p["w_gr"], p["gate_bias_r"],
                                          p["conv_w"], p["conv_b"])
    hf, hb = _mlstm(q, kt, v, rb, col)
    y = _s5(u, p["s5_m"], p["s5_win"], p["s5_wout"], p["s5_a"])
    x = _mix(x, hf, hb, o_in, y, p["mlstm_norm_w"], p["s5_w_glu"], p["w_out"])
    return _ffn(x, p["norm_ffn2"], p["ffn2_w_gate"], p["ffn2_w_up"], p["ffn2_w_down"], p["norm_final"],
                final_norm=True)


def _prepare(norm_ffn1, ffn1_w_gate, ffn1_w_up, ffn1_w_down, norm_mix, w_in, conv_w, conv_b, b_igate, b_fgate,
             mlstm_norm_w, s5_a_re, s5_a_im, s5_log_dt, s5_b_re, s5_b_im, s5_c_re, s5_c_im, s5_d, s5_w_glu,
             w_out, norm_ffn2, ffn2_w_gate, ffn2_w_up, ffn2_w_down, norm_final):
    l = 0
    w = w_in[l]
    g0 = 4 * D_MLSTM
    w_g = w[:, g0:g0 + N_GATES]
    gate_bias = jnp.concatenate([b_igate[l].reshape(-1), b_fgate[l].reshape(-1)])
    m2, win2, wout2, a2 = _s5_weights(s5_a_re[l], s5_a_im[l], s5_log_dt[l], s5_b_re[l], s5_b_im[l],
                                      s5_c_re[l], s5_c_im[l], s5_d[l])
    row = lambda a: a.reshape(1, -1).astype(F32)
    return {
        "norm_ffn1": row(norm_ffn1[l]), "norm_ffn2": row(norm_ffn2[l]), "norm_final": row(norm_final),
        "ffn1_w_gate": ffn1_w_gate[l].astype(BF16), "ffn1_w_up": ffn1_w_up[l].astype(BF16),
        "ffn1_w_down": ffn1_w_down[l].astype(BF16),
        "ffn2_w_gate": ffn2_w_gate[l].astype(BF16), "ffn2_w_up": ffn2_w_up[l].astype(BF16),
        "ffn2_w_down": ffn2_w_down[l].astype(BF16),
        "norm_mix": row(norm_mix[l]),
        "w_all": w.astype(BF16),
        "w_u": w[:, g0 + N_GATES:].astype(BF16),
        "w_gr": w_g.T.astype(BF16),
        "conv_w": conv_w[l].astype(F32), "conv_b": row(conv_b[l]),
        "gate_bias_r": gate_bias.reshape(N_GATES, 1),
        "mlstm_norm_w": row(mlstm_norm_w[l]),
        "s5_m": m2, "s5_win": win2, "s5_wout": wout2, "s5_a": a2,
        "s5_w_glu": s5_w_glu[l].astype(BF16), "w_out": w_out[l].astype(BF16),
    }


def kernel(x_prompt, x_sample, norm_ffn1, ffn1_w_gate, ffn1_w_up, ffn1_w_down, norm_mix, w_in, conv_w, conv_b, b_igate, b_fgate, mlstm_norm_w, s5_a_re, s5_a_im, s5_log_dt, s5_b_re, s5_b_im, s5_c_re, s5_c_im, s5_d, s5_w_glu, w_out, norm_ffn2, ffn2_w_gate, ffn2_w_up, ffn2_w_down, norm_final):
    assert norm_ffn1.shape[0] == 1 and x_prompt.shape[0] == 1 and x_sample.shape[0] == 1
    p = _prepare(norm_ffn1, ffn1_w_gate, ffn1_w_up, ffn1_w_down, norm_mix, w_in, conv_w, conv_b, b_igate,
                 b_fgate, mlstm_norm_w, s5_a_re, s5_a_im, s5_log_dt, s5_b_re, s5_b_im, s5_c_re, s5_c_im, s5_d,
                 s5_w_glu, w_out, norm_ffn2, ffn2_w_gate, ffn2_w_up, ffn2_w_down, norm_final)
    y_prompt = _encode(x_prompt[0], p)[None]
    y_sample = _encode(x_sample[0], p)[None]
    return (y_prompt, y_sample)
```

```python
import functools
import math

import jax
import jax.numpy as jnp
from jax import lax
from jax.experimental import pallas as pl
from jax.experimental.pallas import tpu as pltpu

F32 = jnp.float32
BF16 = jnp.bfloat16

D_MODEL = 2048
D_MLSTM = 1024
D_S5 = 1024
N_HEADS = 8
HEAD_DIM = 128
CHUNK = 128
MLSTM_SUB = 4
N_GATES = 32
GATE_COL_LANES = 128
S5_GROUPS = 64
S5_GC = 16
S5_P = 64
S5_BLK = 16
S5_PAIRS = S5_GROUPS // 2
D_FF = 5632
EPS = 1e-6
M_INIT = -1e30

VMEM_LIMIT_BYTES = 56 * 1024 * 1024

FFN_TM = 1024
FFN_TF = 512
PROJ_TM = 256
PROJ_HALO = 8
MIX_TM = 512


def _sigmoid(x):
    return 1.0 / (1.0 + jnp.exp(-x))


def _rmsnorm(x, w):
    return x * lax.rsqrt(jnp.mean(x * x, axis=-1, keepdims=True) + EPS) * w


def _ffn_body(x_ref, nw_ref, wg_ref, wu_ref, wd_ref, nf_ref, o_ref, xn_ref, *, final_norm):
    j = pl.program_id(1)

    @pl.when(j == 0)
    def _():
        x = x_ref[...]
        xn_ref[...] = _rmsnorm(x, nw_ref[...]).astype(BF16)
        o_ref[...] = x

    xn = xn_ref[...]
    g = jnp.dot(xn, wg_ref[...], preferred_element_type=F32)
    u = jnp.dot(xn, wu_ref[...], preferred_element_type=F32)
    h = (0.5 * g * _sigmoid(g)) * u
    o_ref[...] += jnp.dot(h.astype(BF16), wd_ref[...], preferred_element_type=F32)

    if final_norm:
        @pl.when(j == pl.num_programs(1) - 1)
        def _():
            o_ref[...] = _rmsnorm(o_ref[...], nf_ref[...])


def _ffn(x, norm_w, w_gate, w_up, w_down, norm_final, *, final_norm):
    L = x.shape[0]
    assert L % FFN_TM == 0 and D_FF % FFN_TF == 0
    return pl.pallas_call(
        functools.partial(_ffn_body, final_norm=final_norm),
        grid=(L // FFN_TM, D_FF // FFN_TF),
        in_specs=[
            pl.BlockSpec((FFN_TM, D_MODEL), lambda i, j: (i, 0)),
            pl.BlockSpec((1, D_MODEL), lambda i, j: (0, 0)),
            pl.BlockSpec((D_MODEL, FFN_TF), lambda i, j: (0, j)),
            pl.BlockSpec((D_MODEL, FFN_TF), lambda i, j: (0, j)),
            pl.BlockSpec((FFN_TF, D_MODEL), lambda i, j: (j, 0)),
            pl.BlockSpec((1, D_MODEL), lambda i, j: (0, 0)),
        ],
        out_specs=pl.BlockSpec((FFN_TM, D_MODEL), lambda i, j: (i, 0)),
        out_shape=jax.ShapeDtypeStruct((L, D_MODEL), F32),
        scratch_shapes=[pltpu.VMEM((FFN_TM, D_MODEL), BF16)],
        compiler_params=pltpu.CompilerParams(
            dimension_semantics=("parallel", "arbitrary"), vmem_limit_bytes=VMEM_LIMIT_BYTES),
        name="ffn_final" if final_norm else "ffn",
    )(x, norm_w, w_gate, w_up, w_down, norm_final)


QK_COLS = 2 * D_MLSTM
QK_CB = 512


def _chunk_scan(x, op, identity, backward):
    n = x.shape[-1]
    pos = lax.broadcasted_iota(jnp.int32, x.shape, x.ndim - 1) % CHUNK
    sh = 1
    while sh < CHUNK:
        if backward:
            x = op(x, jnp.where(pos < CHUNK - sh, pltpu.roll(x, n - sh, x.ndim - 1), identity))
        else:
            x = op(x, jnp.where(pos >= sh, pltpu.roll(x, sh, x.ndim - 1), identity))
        sh *= 2
    return x


def _in_proj_body(xp_ref, x_ref, xnx_ref, nw_ref, w_ref, wu_ref, wgr_ref, gb_ref,
                  cw_ref, cb_ref, q_ref, kt_ref, v_ref, o_ref, u_ref, rb_ref, col_ref):
    i = pl.program_id(0)
    tm = x_ref.shape[0]
    nw = nw_ref[...]
    xn = _rmsnorm(x_ref[...], nw)
    xn_prev = jnp.where(i == 0, 0.0, _rmsnorm(xp_ref[...], nw))
    xn_next = jnp.where(i == pl.num_programs(0) - 1, 0.0, _rmsnorm(xnx_ref[...], nw))
    xn_b = xn.astype(BF16)
    xe_b = jnp.concatenate([xn_prev, xn, xn_next], axis=0).astype(BF16)
    rows = tm + 2 * PROJ_HALO

    g = lax.dot_general(wgr_ref[...], xn_b, (((1,), (1,)), ((), ())), preferred_element_type=F32) + gb_ref[...]
    H = N_HEADS
    f_pre = g[2 * H:]
    lf = jnp.minimum(f_pre, 0.0) - jnp.log1p(jnp.exp(-jnp.abs(f_pre)))
    b_f = _chunk_scan(lf[:H], jnp.add, 0.0, False)
    b_b = _chunk_scan(lf[H:], jnp.add, 0.0, True)
    r_f = g[:H] - b_f
    r_b = g[H:2 * H] - b_b
    cm_f = _chunk_scan(r_f, jnp.maximum, -jnp.inf, False)
    cm_b = _chunk_scan(r_b, jnp.maximum, -jnp.inf, True)

    for c in range(QK_COLS // QK_CB):
        cs = slice(c * QK_CB, (c + 1) * QK_CB)
        z = jnp.dot(xe_b, w_ref[:, cs], preferred_element_type=F32)
        z_m1 = pltpu.roll(z, 1, 0)[PROJ_HALO:PROJ_HALO + tm]
        z_0 = z[PROJ_HALO:PROJ_HALO + tm]
        z_p1 = pltpu.roll(z, rows - 1, 0)[PROJ_HALO:PROJ_HALO + tm]
        y = z_m1 * cw_ref[0:1, cs] + z_0 * cw_ref[1:2, cs] + z_p1 * cw_ref[2:3, cs] + cb_ref[:, cs]
        y = y * _sigmoid(y)
        if c * QK_CB < D_MLSTM:
            q_ref[:, cs] = (y * (HEAD_DIM ** -0.5)).astype(q_ref.dtype)
        else:
            kt_ref[c * QK_CB - D_MLSTM:(c + 1) * QK_CB - D_MLSTM, :] = y.T.astype(kt_ref.dtype)

    v_ref[...] = jnp.dot(xn_b, w_ref[:, QK_COLS:QK_COLS + D_MLSTM], preferred_element_type=F32).astype(v_ref.dtype)
    o_ref[...] = jnp.dot(xn_b, w_ref[:, QK_COLS + D_MLSTM:QK_COLS + 2 * D_MLSTM], preferred_element_type=F32)
    u_ref[...] = jnp.dot(xn_b, wu_ref[...], preferred_element_type=F32).astype(u_ref.dtype)
    rb_ref[...] = jnp.concatenate([r_f, r_b, b_f, b_b], axis=0)
    pad = jnp.zeros((GATE_COL_LANES - 4 * H, tm), F32)
    col_ref[...] = jnp.concatenate([b_f, b_b, cm_f, cm_b, pad], axis=0).T


def _in_proj(x, norm_w, w_all, w_u, w_gr, gate_bias, conv_w, conv_b):
    L = x.shape[0]
    tm = PROJ_TM
    assert L % tm == 0 and tm % CHUNK == 0
    hb = tm // PROJ_HALO
    nblk8 = L // PROJ_HALO
    whole = pl.BlockSpec(memory_space=pltpu.VMEM)
    return pl.pallas_call(
        _in_proj_body,
        grid=(L // tm,),
        in_specs=[
            pl.BlockSpec((PROJ_HALO, D_MODEL), lambda i: (jnp.maximum(i * hb - 1, 0), 0)),
            pl.BlockSpec((tm, D_MODEL), lambda i: (i, 0)),
            pl.BlockSpec((PROJ_HALO, D_MODEL), lambda i: (jnp.minimum((i + 1) * hb, nblk8 - 1), 0)),
            whole, whole, whole, whole, whole, whole, whole,
        ],
        out_specs=[
            pl.BlockSpec((tm, D_MLSTM), lambda i: (i, 0)),
            pl.BlockSpec((D_MLSTM, tm), lambda i: (0, i)),
            pl.BlockSpec((tm, D_MLSTM), lambda i: (i, 0)),
            pl.BlockSpec((tm, D_MLSTM), lambda i: (i, 0)),
            pl.BlockSpec((tm, D_S5), lambda i: (i, 0)),
            pl.BlockSpec((N_GATES, tm), lambda i: (0, i)),
            pl.BlockSpec((tm, GATE_COL_LANES), lambda i: (i, 0)),
        ],
        out_shape=[
            jax.ShapeDtypeStruct((L, D_MLSTM), BF16),
            jax.ShapeDtypeStruct((D_MLSTM, L), BF16),
            jax.ShapeDtypeStruct((L, D_MLSTM), BF16),
            jax.ShapeDtypeStruct((L, D_MLSTM), F32),
            jax.ShapeDtypeStruct((L, D_S5), BF16),
            jax.ShapeDtypeStruct((N_GATES, L), F32),
            jax.ShapeDtypeStruct((L, GATE_COL_LANES), F32),
        ],
        compiler_params=pltpu.CompilerParams(
            dimension_semantics=("parallel",), vmem_limit_bytes=VMEM_LIMIT_BYTES),
        name="in_proj",
    )(x, x, x, norm_w, w_all, w_u, w_gr, gate_bias, conv_w, conv_b)


def _bf16_split3(x):
    hi = x.astype(BF16)
    r1 = x - hi.astype(F32)
    mid = r1.astype(BF16)
    lo = (r1 - mid.astype(F32)).astype(BF16)
    return hi, mid, lo


def _mlstm_direction(q_ref, kt_ref, v_ref, rb_ref, col_ref, h_ref, ct_ref, m_ref, ml_ref, d, sub):
    T = CHUNK
    H = N_HEADS
    backward = d == 1
    tt = slice(sub * T, (sub + 1) * T)
    rr = lax.broadcasted_iota(jnp.int32, (T, T), 0)
    cc = lax.broadcasted_iota(jnp.int32, (T, T), 1)
    mask = (cc >= rr) if backward else (cc <= rr)
    last = 0 if backward else T - 1

    r_rows = rb_ref[d * H:(d + 1) * H, tt]
    b_rows = rb_ref[(2 + d) * H:(3 + d) * H, tt]
    b_cols = col_ref[tt, d * H:(d + 1) * H]
    cm_cols = col_ref[tt, (2 + d) * H:(3 + d) * H]
    m_prev = m_ref[d]
    m_prev_l = ml_ref[d]
    m_cols = jnp.maximum(cm_cols, m_prev_l)
    clamp_cols = -(b_cols + m_cols)
    m_last = jnp.maximum(jnp.max(r_rows, axis=1, keepdims=True), m_prev)
    b_tot = jnp.broadcast_to(b_rows[:, last:last + 1], (H, T))
    sc_rows = jnp.exp(m_prev - m_last)
    wkk_rows = jnp.exp(r_rows - m_last)
    ones = jnp.ones((T, HEAD_DIM), BF16)

    for h in range(H):
        ci = d * H + h
        hs = slice(h * HEAD_DIM, (h + 1) * HEAD_DIM)
        q = q_ref[tt, hs]
        kt = kt_ref[hs, tt]
        vaug = jnp.concatenate([v_ref[tt, hs], ones], axis=1)
        ct_prev = ct_ref[ci]
        m_col = jnp.broadcast_to(m_cols[:, h:h + 1], (T, T))
        w = jnp.exp(jnp.where(mask, r_rows[h:h + 1, :] - m_col, -jnp.inf))
        sq = jnp.dot(q, jnp.concatenate([kt, ct_prev.astype(BF16)], axis=1), preferred_element_type=F32)
        s = sq[:, :T] * w
        qc = sq[:, T:]
        kw = (kt.astype(F32) * wkk_rows[h:h + 1, :]).astype(BF16)
        both = jnp.dot(jnp.concatenate([s.astype(BF16), kw], axis=0), vaug, preferred_element_type=F32)
        sv = both[:T]
        upd = both[T:]
        s_inter = jnp.exp(m_prev[h:h + 1, :] - m_col)
        num = sv[:, :HEAD_DIM] + s_inter * qc[:, :HEAD_DIM]
        den = sv[:, HEAD_DIM:] + s_inter * qc[:, HEAD_DIM:]
        floor = jnp.exp(jnp.broadcast_to(clamp_cols[:, h:h + 1], (T, HEAD_DIM)))
        h_ref[tt, hs] = num / jnp.maximum(jnp.abs(den), floor)

        sc = sc_rows[h:h + 1, :]
        ct_ref[ci] = jnp.concatenate([sc, sc], axis=1) * ct_prev + upd

    m_ref[d] = b_tot + m_last
    ml_ref[d] = b_cols[last:last + 1, :] + m_cols[last:last + 1, :]


def _mlstm_body(qf_ref, ktf_ref, vf_ref, rbf_ref, colf_ref, qb_ref, ktb_ref, vb_ref, rbb_ref, colb_ref,
                hf_ref, hb_ref, ct_ref, m_ref, ml_ref):
    @pl.when(pl.program_id(0) == 0)
    def _():
        ct_ref[...] = jnp.zeros_like(ct_ref)
        m_ref[...] = jnp.full_like(m_ref, M_INIT)
        ml_ref[...] = jnp.full_like(ml_ref, M_INIT)

    for sub in range(MLSTM_SUB):
        _mlstm_direction(qf_ref, ktf_ref, vf_ref, rbf_ref, colf_ref, hf_ref, ct_ref, m_ref, ml_ref, 0, sub)
        _mlstm_direction(qb_ref, ktb_ref, vb_ref, rbb_ref, colb_ref, hb_ref, ct_ref, m_ref, ml_ref, 1,
                         MLSTM_SUB - 1 - sub)


def _mlstm(q, kt, v, rb, col):
    L = q.shape[0]
    T = MLSTM_SUB * CHUNK
    assert L % T == 0 and CHUNK == HEAD_DIM
    nc = L // T
    fwd = lambda c: (c, 0)
    bwd = lambda c: (nc - 1 - c, 0)
    fwd_r = lambda c: (0, c)
    bwd_r = lambda c: (0, nc - 1 - c)
    return pl.pallas_call(
        _mlstm_body,
        grid=(nc,),
        in_specs=[
            pl.BlockSpec((T, D_MLSTM), fwd), pl.BlockSpec((D_MLSTM, T), fwd_r), pl.BlockSpec((T, D_MLSTM), fwd),
            pl.BlockSpec((N_GATES, T), fwd_r), pl.BlockSpec((T, GATE_COL_LANES), fwd),
            pl.BlockSpec((T, D_MLSTM), bwd), pl.BlockSpec((D_MLSTM, T), bwd_r), pl.BlockSpec((T, D_MLSTM), bwd),
            pl.BlockSpec((N_GATES, T), bwd_r), pl.BlockSpec((T, GATE_COL_LANES), bwd),
        ],
        out_specs=[pl.BlockSpec((T, D_MLSTM), fwd), pl.BlockSpec((T, D_MLSTM), bwd)],
        out_shape=[jax.ShapeDtypeStruct((L, D_MLSTM), F32), jax.ShapeDtypeStruct((L, D_MLSTM), F32)],
        scratch_shapes=[
            pltpu.VMEM((2 * N_HEADS, HEAD_DIM, 2 * HEAD_DIM), F32),
            pltpu.VMEM((2, N_HEADS, CHUNK), F32),
            pltpu.VMEM((2, 1, N_HEADS), F32),
        ],
        compiler_params=pltpu.CompilerParams(
            dimension_semantics=("arbitrary",), vmem_limit_bytes=VMEM_LIMIT_BYTES),
        name="mlstm",
    )(q, kt, v, rb, col, q, kt, v, rb, col)


S5_PAIR_CH = 2 * S5_GC
S5_ROW = S5_BLK * S5_PAIR_CH
S5_ST = 2 * S5_P
S5_TILE = 8
S5_LANE_PAIRS = 4
S5_TT = 4096
S5_NBT = S5_TT // S5_BLK


def _s5_scan(a, sin_ref, nb):
    R = S5_TILE
    ntile = nb // R
    row = lax.broadcasted_iota(jnp.int32, (R, S5_ST), 0)
    zero = jnp.zeros((1, S5_ST), F32)

    def cmul(x, y):
        return x[0] * y[0] - x[1] * y[1], x[0] * y[1] + x[1] * y[0]

    def bcast(x):
        return tuple(jnp.broadcast_to(t, (R, S5_ST)) for t in x)

    def tables(ar, ai, backward):
        pw = {1: (ar, ai)}
        for e in range(2, R + 1):
            pw[e] = cmul(pw[e // 2], pw[e - e // 2])
        steps = []
        for sh in (1, 2, 4):
            keep = (row < R - sh) if backward else (row >= sh)
            steps.append(tuple(jnp.where(keep, t, 0.0) for t in bcast(pw[sh])))
        order = range(R, 0, -1) if backward else range(1, R + 1)
        carry_pw = tuple(jnp.concatenate([pw[e][j] for e in order], axis=0) for j in (0, 1))
        return steps, carry_pw

    def scan_tile(x, carry, steps, cpw, backward):
        for sh, am in zip((1, 2, 4), steps):
            rs = (R - sh) if backward else sh
            x = tuple(p + q for p, q in zip(x, cmul(am, (pltpu.roll(x[0], rs, 0), pltpu.roll(x[1], rs, 0)))))
        cb = bcast(carry)
        x = tuple(p + q for p, q in zip(x, cmul(cpw, cb)))
        edge, rs = (R - 1, R - 1) if backward else (0, 1)
        enter = tuple(jnp.where(row == edge, c, pltpu.roll(t, rs, 0)) for t, c in zip(x, cb))
        last = 0 if backward else R - 1
        return enter, (x[0][last:last + 1], x[1][last:last + 1])

    tabs = []
    for p in range(S5_LANE_PAIRS):
        ap = a[p]
        tabs.append((tables(ap[0:1], ap[1:2], False), tables(ap[2:3], ap[3:4], True)))

    def step(i, carry):
        rf = pl.multiple_of(i * R, R)
        rb = pl.multiple_of((ntile - 1 - i) * R, R)
        out = []
        for p in range(S5_LANE_PAIRS):
            (steps_f, cpw_f), (steps_b, cpw_b) = tabs[p]
            cf, cb = carry[p]
            xf = (sin_ref[p, pl.ds(rf, R), 0:S5_ST], sin_ref[p, pl.ds(rf, R), S5_ST:2 * S5_ST])
            xb = (sin_ref[p, pl.ds(rb, R), 2 * S5_ST:3 * S5_ST], sin_ref[p, pl.ds(rb, R), 3 * S5_ST:4 * S5_ST])
            ef, cf = scan_tile(xf, cf, steps_f, cpw_f, False)
            eb, cb = scan_tile(xb, cb, steps_b, cpw_b, True)
            sin_ref[p, pl.ds(rf, R), 0:S5_ST] = ef[0]
            sin_ref[p, pl.ds(rf, R), S5_ST:2 * S5_ST] = ef[1]
            sin_ref[p, pl.ds(rb, R), 2 * S5_ST:3 * S5_ST] = eb[0]
            sin_ref[p, pl.ds(rb, R), 3 * S5_ST:4 * S5_ST] = eb[1]
            out.append((cf, cb))
        return tuple(out)

    init = tuple(((zero, zero), (zero, zero)) for _ in range(S5_LANE_PAIRS))
    lax.fori_loop(0, ntile, step, init)


def _s5_body(u_ref, m_ref, win_ref, wout_ref, a_ref, y_ref, tok_ref, u2_ref, sin_ref, y2_ref):
    phase = pl.program_id(1)
    t = pl.program_id(2)
    nb = u2_ref.shape[1]
    r0 = pl.multiple_of(t * S5_NBT, S5_NBT)
    rows = pl.ds(r0, S5_NBT)

    n_slot = S5_LANE_PAIRS
    lt = S5_LANE_PAIRS * S5_PAIR_CH
    slot = lax.broadcasted_iota(jnp.int32, (S5_NBT, lt), 1) // S5_PAIR_CH

    def pick(parts, first):
        out = parts[n_slot - 1]
        for i in range(n_slot - 2, -1, -1):
            out = jnp.where(slot == (first + i) % n_slot, parts[i], out)
        return out

    @pl.when(phase == 0)
    def _():
        tok_ref[...] = u_ref[...].astype(F32)
        for q in range(S5_BLK // n_slot):
            rot = []
            for i in range(n_slot):
                tok = tok_ref[pl.ds(n_slot * q + i, S5_NBT, stride=S5_BLK), :]
                rot.append(pltpu.roll(tok, i * S5_PAIR_CH, 1) if i else tok)
            for p in range(S5_LANE_PAIRS):
                u2_ref[p, rows, q * lt:(q + 1) * lt] = pick(rot, p).astype(BF16)
        for p in range(S5_LANE_PAIRS):
            sin_ref[p, rows, :] = jnp.dot(u2_ref[p, rows, :], win_ref[p], preferred_element_type=F32)

    @pl.when((phase == 0) & (t == pl.num_programs(2) - 1))
    def _():
        _s5_scan(a_ref[...], sin_ref, nb)

    @pl.when(phase == 1)
    def _():
        for p in range(S5_LANE_PAIRS):
            y2_ref[p] = (jnp.dot(u2_ref[p, rows, :], m_ref[p], preferred_element_type=F32)
                         + jnp.dot(sin_ref[p, rows, :].astype(BF16), wout_ref[p], preferred_element_type=F32))
        for tkn in range(S5_BLK):
            q, i = divmod(tkn, n_slot)
            merged = pick([y2_ref[p, :, q * lt:(q + 1) * lt] for p in range(S5_LANE_PAIRS)], i)
            out = pltpu.roll(merged, ((n_slot - i) % n_slot) * S5_PAIR_CH, 1) if i else merged
            y_ref[pl.ds(tkn, S5_NBT, stride=S5_BLK), :] = out


def _s5(u, m2, win2, wout2, a2):
    L = u.shape[0]
    assert L % S5_TT == 0 and D_S5 == S5_PAIRS * S5_PAIR_CH
    nb = L // S5_BLK
    nq = S5_PAIRS // S5_LANE_PAIRS
    lane_tile = S5_LANE_PAIRS * S5_PAIR_CH
    wspec = pl.BlockSpec((S5_LANE_PAIRS, S5_ROW, S5_ROW), lambda q, ph, t: (q, 0, 0))
    return pl.pallas_call(
        _s5_body,
        grid=(nq, 2, L // S5_TT),
        in_specs=[
            pl.BlockSpec((S5_TT, lane_tile), lambda q, ph, t: (t, q)),
            wspec, wspec, wspec,
            pl.BlockSpec((S5_LANE_PAIRS, 4, S5_ST), lambda q, ph, t: (q, 0, 0)),
        ],
        out_specs=pl.BlockSpec((S5_TT, lane_tile), lambda q, ph, t: (t * ph, q)),
        out_shape=jax.ShapeDtypeStruct((L, D_S5), F32),
        scratch_shapes=[
            pltpu.VMEM((S5_TT, lane_tile), F32),
            pltpu.VMEM((S5_LANE_PAIRS, nb, S5_ROW), BF16),
            pltpu.VMEM((S5_LANE_PAIRS, nb, 4 * S5_ST), F32),
            pltpu.VMEM((S5_LANE_PAIRS, S5_NBT, S5_ROW), F32),
        ],
        compiler_params=pltpu.CompilerParams(
            dimension_semantics=("parallel", "arbitrary", "arbitrary"), vmem_limit_bytes=VMEM_LIMIT_BYTES),
        name="s5",
    )(u, m2, win2, wout2, a2)


S5_EXP = (S5_BLK + 1) * S5_PAIR_CH
S5_EXP_PAD = 640
S5_PWT_ROWS = 24


def _s5_prep_body(pwk_ref, pwt_ref, ct_ref, b2t_ref, d_ref, m_ref, win_ref, wout_ref):
    nt, w = S5_BLK, S5_PAIR_CH
    lane = lax.broadcasted_iota(jnp.int32, (S5_ST, S5_EXP_PAD), 1)
    row = lax.broadcasted_iota(jnp.int32, (S5_ST, S5_EXP_PAD), 0)
    sel_slot = (lane // w == row).astype(BF16)
    sel_chan = ((lane % S5_GC == row) & (row < S5_GC)).astype(BF16)
    same_group = (row // S5_P == (lane // S5_GC) % 2) & (lane < S5_EXP)

    def expand(x, sel):
        return sum(jnp.dot(p, sel, preferred_element_type=F32) for p in _bf16_split3(x))

    def split_dot(a, x):
        a_hi, a_lo, _ = _bf16_split3(a)
        x_hi, x_lo, _ = _bf16_split3(x)
        return (jnp.dot(a_hi, x_hi, preferred_element_type=F32) + jnp.dot(a_hi, x_lo, preferred_element_type=F32)
                + jnp.dot(a_lo, x_hi, preferred_element_type=F32))

    lane_m = lax.broadcasted_iota(jnp.int32, (w, S5_ROW), 1)
    row_m = lax.broadcasted_iota(jnp.int32, (w, S5_ROW), 0)
    krow, wout_rows, win_rows = [], [], []
    for z in range(2):
        pr, pi = expand(pwk_ref[z, 0, 0], sel_slot), expand(pwk_ref[z, 1, 0], sel_slot)
        cr, ci = expand(ct_ref[z, 0, 0], sel_chan), expand(ct_ref[z, 1, 0], sel_chan)
        xr = jnp.where(same_group, pr * cr - pi * ci, 0.0)
        xi = jnp.where(same_group, pr * ci + pi * cr, 0.0)
        lo = w if z == 0 else 0
        wout_rows += [xr[:, lo:lo + S5_ROW], -xi[:, lo:lo + S5_ROW]]
        k_all = split_dot(b2t_ref[z, 0, 0], xr) - split_dot(b2t_ref[z, 1, 0], xi)
        lo = 0 if z == 0 else w
        krow.append(k_all[:, lo:lo + S5_ROW])
        br, bi = b2t_ref[z, 0, 0], b2t_ref[z, 1, 0]
        blocks = []
        for s in range(nt):
            e = nt - 1 - s if z == 0 else s
            qr, qi = pwt_ref[z, 0, 0, e:e + 1, :], pwt_ref[z, 1, 0, e:e + 1, :]
            blocks.append((br * qr - bi * qi, br * qi + bi * qr))
        win_rows.append(blocks)

    kf, kb = krow
    d_diag = jnp.where((lane_m % w) == row_m, d_ref[0], 0.0)
    m_rows = []
    for s in range(nt):
        f = jnp.where(lane_m >= w * s, pltpu.roll(kf, w * s, 1), 0.0) if s else kf
        sh = w * (nt - 1 - s)
        b = jnp.where(lane_m < S5_ROW - sh, pltpu.roll(kb, S5_ROW - sh, 1), 0.0) if sh else kb
        dd = jnp.where(lane_m // w == s, d_diag, 0.0)
        m_rows.append(f + b + dd)

    n_slot = S5_LANE_PAIRS
    lt = n_slot * w

    def place(tkn, pp):
        return n_slot * (tkn // n_slot) + (tkn % n_slot + pp) % n_slot

    def order_cols(x, pp):
        if pp == 0:
            return x
        return jnp.concatenate([pltpu.roll(x[:, q * lt:(q + 1) * lt], pp * w, 1) for q in range(S5_ROW // lt)], axis=1)

    for pp in range(n_slot):
        @pl.when(pl.program_id(1) == pp)
        def _(pp=pp):
            for k, x in enumerate(wout_rows):
                wout_ref[0, k * S5_ST:(k + 1) * S5_ST, :] = order_cols(x.astype(BF16), pp)
            for s in range(nt):
                r0 = place(s, pp) * w
                m_ref[0, r0:r0 + w, :] = order_cols(m_rows[s].astype(BF16), pp)
                for z in range(2):
                    wr, wi = win_rows[z][s]
                    win_ref[0, r0:r0 + w, (2 * z) * S5_ST:(2 * z + 1) * S5_ST] = wr.astype(BF16)
                    win_ref[0, r0:r0 + w, (2 * z + 1) * S5_ST:(2 * z + 2) * S5_ST] = wi.astype(BF16)


def _s5_prep(pwk, pwt, ct, b2t, d2):
    npair = d2.shape[0]
    n = S5_LANE_PAIRS
    spec = lambda r, c: pl.BlockSpec((2, 2, 1, r, c), lambda g, p: (0, 0, g * n + p, 0, 0))
    out = pl.BlockSpec((1, S5_ROW, S5_ROW), lambda g, p: (g * n + p, 0, 0))
    shape = jax.ShapeDtypeStruct((npair, S5_ROW, S5_ROW), BF16)
    return pl.pallas_call(
        _s5_prep_body,
        grid=(npair // n, n),
        in_specs=[spec(S5_ST, S5_ST), spec(S5_PWT_ROWS, S5_ST), spec(S5_ST, S5_ST), spec(S5_PAIR_CH, S5_ST),
                  pl.BlockSpec((1, 1, S5_ROW), lambda g, p: (g * n + p, 0, 0))],
        out_specs=[out, out, out],
        out_shape=[shape, shape, shape],
        compiler_params=pltpu.CompilerParams(
            dimension_semantics=("parallel", "arbitrary"), vmem_limit_bytes=VMEM_LIMIT_BYTES),
        name="s5_prep",
    )(pwk, pwt, ct, b2t, d2)


def _s5_weights(a_re, a_im, log_dt, b_re, b_im, c_re, c_im, d_skip):
    nt, npair = S5_BLK, S5_PAIRS
    lam = lax.complex(a_re, a_im)
    dt = jnp.exp(log_dt)[..., None]
    lam_bar = jnp.exp(lam * dt)
    b_bar = ((lam_bar - 1.0) / lam)[..., None] * lax.complex(b_re, b_im)
    taus = jnp.arange(nt + 1, dtype=F32)
    pw = jnp.exp((lam * dt)[..., None] * taus).reshape(2, npair, S5_ST, nt + 1)
    ri = lambda x: jnp.stack([jnp.real(x), jnp.imag(x)], axis=1)
    pwk = jnp.stack([pw[0], pw[1, ..., ::-1]])
    pwk = jnp.pad(ri(pwk), ((0, 0), (0, 0), (0, 0), (0, 0), (0, S5_ST - (nt + 1))))
    pwt = jnp.pad(ri(pw).transpose(0, 1, 2, 4, 3), ((0, 0), (0, 0), (0, 0), (0, S5_PWT_ROWS - (nt + 1)), (0, 0)))
    ct = lax.complex(c_re, c_im).transpose(0, 1, 3, 2).reshape(2, npair, S5_ST, S5_GC)
    ct = jnp.pad(ri(ct), ((0, 0), (0, 0), (0, 0), (0, 0), (0, S5_ST - S5_GC)))
    bb = b_bar.reshape(2, npair, 2, S5_P, S5_GC)
    eye2 = jnp.eye(2, dtype=F32)
    b2t = (bb.transpose(0, 1, 2, 4, 3)[:, :, :, :, None, :] * eye2[None, None, :, None, :, None])
    b2t = ri(b2t.reshape(2, npair, S5_PAIR_CH, S5_ST))
    d2 = jnp.tile(d_skip.reshape(npair, 1, S5_PAIR_CH), (1, 1, nt))
    m2, win2, wout2 = _s5_prep(pwk, pwt, ct, b2t, d2)
    a_blk = pw[..., nt]
    a2 = jnp.stack([jnp.real(a_blk[0]), jnp.imag(a_blk[0]), jnp.real(a_blk[1]), jnp.imag(a_blk[1])], axis=1)
    return m2, win2, wout2, a2


def _mix_body(x_ref, hf_ref, hb_ref, o_ref, y_ref, nw_ref, wglu_ref, wout_ref, out_ref):
    h = hf_ref[...] + hb_ref[...]
    parts = []
    for hd in range(N_HEADS):
        hh = h[:, hd * HEAD_DIM:(hd + 1) * HEAD_DIM]
        mu = jnp.mean(hh, axis=-1, keepdims=True)
        var = jnp.mean(jnp.square(hh - mu), axis=-1, keepdims=True)
        parts.append((hh - mu) * lax.rsqrt(var + EPS))
    hn = jnp.concatenate(parts, axis=1)
    h_m = hn * nw_ref[...] * _sigmoid(o_ref[...])
    y = y_ref[...]
    gelu = 0.5 * y * (1.0 + jnp.tanh(math.sqrt(2.0 / math.pi) * (y + 0.044715 * (y * y * y))))
    ab = jnp.dot(gelu.astype(BF16), wglu_ref[...], preferred_element_type=F32)
    h_s = ab[:, :D_S5] * _sigmoid(ab[:, D_S5:])
    mix = jnp.dot(h_m.astype(BF16), wout_ref[0:D_MLSTM, :], preferred_element_type=F32)
    mix += jnp.dot(h_s.astype(BF16), wout_ref[D_MLSTM:, :], preferred_element_type=F32)
    out_ref[...] = x_ref[...] + mix


def _mix(x, hf, hb, o_in, y, norm_w, w_glu, w_out):
    L = x.shape[0]
    tm = MIX_TM
    assert L % tm == 0
    whole = pl.BlockSpec(memory_space=pltpu.VMEM)
    row = lambda n: pl.BlockSpec((tm, n), lambda i: (i, 0))
    return pl.pallas_call(
        _mix_body,
        grid=(L // tm,),
        in_specs=[row(D_MODEL), row(D_MLSTM), row(D_MLSTM), row(D_MLSTM), row(D_S5), whole, whole, whole],
        out_specs=row(D_MODEL),
        out_shape=jax.ShapeDtypeStruct((L, D_MODEL), F32),
        compiler_params=pltpu.CompilerParams(
            dimension_semantics=("parallel",), vmem_limit_bytes=VMEM_LIMIT_BYTES),
        name="mix",
    )(x, hf, hb, o_in, y, norm_w, w_glu, w_out)


def _encode(x, p):
    x = _ffn(x, p["norm_ffn1"], p["ffn1_w_gate"], p["ffn1_w_up"], p["ffn1_w_down"], p["norm_final"],
             final_norm=False)
    q, kt, v, o_in, u, rb, col = _in_proj(x, p["norm_mix"], p["w_all"], p["w_u"], p["w_gr"], p["gate_bias_r"],
                                          p["conv_w"], p["conv_b"])
    hf, hb = _mlstm(q, kt, v, rb, col)
    y = _s5(u, p["s5_m"], p["s5_win"], p["s5_wout"], p["s5_a"])
    x = _mix(x, hf, hb, o_in, y, p["mlstm_norm_w"], p["s5_w_glu"], p["w_out"])
    return _ffn(x, p["norm_ffn2"], p["ffn2_w_gate"], p["ffn2_w_up"], p["ffn2_w_down"], p["norm_final"],
                final_norm=True)


def _prepare(norm_ffn1, ffn1_w_gate, ffn1_w_up, ffn1_w_down, norm_mix, w_in, conv_w, conv_b, b_igate, b_fgate,
             mlstm_norm_w, s5_a_re, s5_a_im, s5_log_dt, s5_b_re, s5_b_im, s5_c_re, s5_c_im, s5_d, s5_w_glu,
             w_out, norm_ffn2, ffn2_w_gate, ffn2_w_up, ffn2_w_down, norm_final):
    l = 0
    w = w_in[l]
    g0 = 4 * D_MLSTM
    w_g = w[:, g0:g0 + N_GATES]
    gate_bias = jnp.concatenate([b_igate[l].reshape(-1), b_fgate[l].reshape(-1)])
    m2, win2, wout2, a2 = _s5_weights(s5_a_re[l], s5_a_im[l], s5_log_dt[l], s5_b_re[l], s5_b_im[l],
                                      s5_c_re[l], s5_c_im[l], s5_d[l])
    row = lambda a: a.reshape(1, -1).astype(F32)
    return {
        "norm_ffn1": row(norm_ffn1[l]), "norm_ffn2": row(norm_ffn2[l]), "norm_final": row(norm_final),
        "ffn1_w_gate": ffn1_w_gate[l].astype(BF16), "ffn1_w_up": ffn1_w_up[l].astype(BF16),
        "ffn1_w_down": ffn1_w_down[l].astype(BF16),
        "ffn2_w_gate": ffn2_w_gate[l].astype(BF16), "ffn2_w_up": ffn2_w_up[l].astype(BF16),
        "ffn2_w_down": ffn2_w_down[l].astype(BF16),
        "norm_mix": row(norm_mix[l]),
        "w_all": w.astype(BF16),
        "w_u": w[:, g0 + N_GATES:].astype(BF16),
        "w_gr": w_g.T.astype(BF16),
        "conv_w": conv_w[l].astype(F32), "conv_b": row(conv_b[l]),
        "gate_bias_r": gate_bias.reshape(N_GATES, 1),
        "mlstm_norm_w": row(mlstm_norm_w[l]),
        "s5_m": m2, "s5_win": win2, "s5_wout": wout2, "s5_a": a2,
        "s5_w_glu": s5_w_glu[l].astype(BF16), "w_out": w_out[l].astype(BF16),
    }


def kernel(x_prompt, x_sample, norm_ffn1, ffn1_w_gate, ffn1_w_up, ffn1_w_down, norm_mix, w_in, conv_w, conv_b, b_igate, b_fgate, mlstm_norm_w, s5_a_re, s5_a_im, s5_log_dt, s5_b_re, s5_b_im, s5_c_re, s5_c_im, s5_d, s5_w_glu, w_out, norm_ffn2, ffn2_w_gate, ffn2_w_up, ffn2_w_down, norm_final):
    assert norm_ffn1.shape[0] == 1 and x_prompt.shape[0] == 1 and x_sample.shape[0] == 1
    p = _prepare(norm_ffn1, ffn1_w_gate, ffn1_w_up, ffn1_w_down, norm_mix, w_in, conv_w, conv_b, b_igate,
                 b_fgate, mlstm_norm_w, s5_a_re, s5_a_im, s5_log_dt, s5_b_re, s5_b_im, s5_c_re, s5_c_im, s5_d,
                 s5_w_glu, w_out, norm_ffn2, ffn2_w_gate, ffn2_w_up, ffn2_w_down, norm_final)
    y_prompt = _encode(x_prompt[0], p)[None]
    y_sample = _encode(x_sample[0], p)[None]
    return (y_prompt, y_sample)
```

```python
import functools
import math

import jax
import jax.numpy as jnp
from jax import lax
from jax.experimental import pallas as pl
from jax.experimental.pallas import tpu as pltpu

F32 = jnp.float32
BF16 = jnp.bfloat16

D_MODEL = 2048
D_MLSTM = 1024
D_S5 = 1024
N_HEADS = 8
HEAD_DIM = 128
CHUNK = 128
MLSTM_SUB = 4
N_GATES = 32
GATE_COL_LANES = 128
S5_GROUPS = 64
S5_GC = 16
S5_P = 64
S5_BLK = 16
S5_PAIRS = S5_GROUPS // 2
D_FF = 5632
EPS = 1e-6
M_INIT = -1e30

VMEM_LIMIT_BYTES = 56 * 1024 * 1024

FFN_TM = 1024
FFN_TF = 512
PROJ_TM = 256
PROJ_HALO = 8
MIX_TM = 512


def _sigmoid(x):
    return 1.0 / (1.0 + jnp.exp(-x))


def _rmsnorm(x, w):
    return x * lax.rsqrt(jnp.mean(x * x, axis=-1, keepdims=True) + EPS) * w


def _ffn_body(x_ref, nw_ref, wg_ref, wu_ref, wd_ref, nf_ref, o_ref, xn_ref, *, final_norm):
    j = pl.program_id(1)

    @pl.when(j == 0)
    def _():
        x = x_ref[...]
        xn_ref[...] = _rmsnorm(x, nw_ref[...]).astype(BF16)
        o_ref[...] = x

    xn = xn_ref[...]
    g = jnp.dot(xn, wg_ref[...], preferred_element_type=F32)
    u = jnp.dot(xn, wu_ref[...], preferred_element_type=F32)
    h = (0.5 * g * _sigmoid(g)) * u
    o_ref[...] += jnp.dot(h.astype(BF16), wd_ref[...], preferred_element_type=F32)

    if final_norm:
        @pl.when(j == pl.num_programs(1) - 1)
        def _():
            o_ref[...] = _rmsnorm(o_ref[...], nf_ref[...])


def _ffn(x, norm_w, w_gate, w_up, w_down, norm_final, *, final_norm):
    L = x.shape[0]
    assert L % FFN_TM == 0 and D_FF % FFN_TF == 0
    return pl.pallas_call(
        functools.partial(_ffn_body, final_norm=final_norm),
        grid=(L // FFN_TM, D_FF // FFN_TF),
        in_specs=[
            pl.BlockSpec((FFN_TM, D_MODEL), lambda i, j: (i, 0)),
            pl.BlockSpec((1, D_MODEL), lambda i, j: (0, 0)),
            pl.BlockSpec((D_MODEL, FFN_TF), lambda i, j: (0, j)),
            pl.BlockSpec((D_MODEL, FFN_TF), lambda i, j: (0, j)),
            pl.BlockSpec((FFN_TF, D_MODEL), lambda i, j: (j, 0)),
            pl.BlockSpec((1, D_MODEL), lambda i, j: (0, 0)),
        ],
        out_specs=pl.BlockSpec((FFN_TM, D_MODEL), lambda i, j: (i, 0)),
        out_shape=jax.ShapeDtypeStruct((L, D_MODEL), F32),
        scratch_shapes=[pltpu.VMEM((FFN_TM, D_MODEL), BF16)],
        compiler_params=pltpu.CompilerParams(
            dimension_semantics=("parallel", "arbitrary"), vmem_limit_bytes=VMEM_LIMIT_BYTES),
        name="ffn_final" if final_norm else "ffn",
    )(x, norm_w, w_gate, w_up, w_down, norm_final)


QK_COLS = 2 * D_MLSTM
QK_CB = 512


def _chunk_scan(x, op, identity, backward):
    n = x.shape[-1]
    pos = lax.broadcasted_iota(jnp.int32, x.shape, x.ndim - 1) % CHUNK
    sh = 1
    while sh < CHUNK:
        if backward:
            x = op(x, jnp.where(pos < CHUNK - sh, pltpu.roll(x, n - sh, x.ndim - 1), identity))
        else:
            x = op(x, jnp.where(pos >= sh, pltpu.roll(x, sh, x.ndim - 1), identity))
        sh *= 2
    return x


def _in_proj_body(xp_ref, x_ref, xnx_ref, nw_ref, w_ref, wu_ref, wgr_ref, gb_ref,
                  cw_ref, cb_ref, q_ref, kt_ref, v_ref, o_ref, u_ref, rb_ref, col_ref):
    i = pl.program_id(0)
    tm = x_ref.shape[0]
    nw = nw_ref[...]
    xn = _rmsnorm(x_ref[...], nw)
    xn_prev = jnp.where(i == 0, 0.0, _rmsnorm(xp_ref[...], nw))
    xn_next = jnp.where(i == pl.num_programs(0) - 1, 0.0, _rmsnorm(xnx_ref[...], nw))
    xn_b = xn.astype(BF16)
    xe_b = jnp.concatenate([xn_prev, xn, xn_next], axis=0).astype(BF16)
    rows = tm + 2 * PROJ_HALO

    g = lax.dot_general(wgr_ref[...], xn_b, (((1,), (1,)), ((), ())), preferred_element_type=F32) + gb_ref[...]
    H = N_HEADS
    f_pre = g[2 * H:]
    lf = jnp.minimum(f_pre, 0.0) - jnp.log1p(jnp.exp(-jnp.abs(f_pre)))
    b_f = _chunk_scan(lf[:H], jnp.add, 0.0, False)
    b_b = _chunk_scan(lf[H:], jnp.add, 0.0, True)
    r_f = g[:H] - b_f
    r_b = g[H:2 * H] - b_b
    cm_f = _chunk_scan(r_f, jnp.maximum, -jnp.inf, False)
    cm_b = _chunk_scan(r_b, jnp.maximum, -jnp.inf, True)

    for c in range(QK_COLS // QK_CB):
        cs = slice(c * QK_CB, (c + 1) * QK_CB)
        z = jnp.dot(xe_b, w_ref[:, cs], preferred_element_type=F32)
        z_m1 = pltpu.roll(z, 1, 0)[PROJ_HALO:PROJ_HALO + tm]
        z_0 = z[PROJ_HALO:PROJ_HALO + tm]
        z_p1 = pltpu.roll(z, rows - 1, 0)[PROJ_HALO:PROJ_HALO + tm]
        y = z_m1 * cw_ref[0:1, cs] + z_0 * cw_ref[1:2, cs] + z_p1 * cw_ref[2:3, cs] + cb_ref[:, cs]
        y = y * _sigmoid(y)
        if c * QK_CB < D_MLSTM:
            q_ref[:, cs] = (y * (HEAD_DIM ** -0.5)).astype(q_ref.dtype)
        else:
            kt_ref[c * QK_CB - D_MLSTM:(c + 1) * QK_CB - D_MLSTM, :] = y.T.astype(kt_ref.dtype)

    v_ref[...] = jnp.dot(xn_b, w_ref[:, QK_COLS:QK_COLS + D_MLSTM], preferred_element_type=F32).astype(v_ref.dtype)
    o_ref[...] = jnp.dot(xn_b, w_ref[:, QK_COLS + D_MLSTM:QK_COLS + 2 * D_MLSTM], preferred_element_type=F32)
    u_ref[...] = jnp.dot(xn_b, wu_ref[...], preferred_element_type=F32)
    rb_ref[...] = jnp.concatenate([r_f, r_b, b_f, b_b], axis=0)
    pad = jnp.zeros((GATE_COL_LANES - 4 * H, tm), F32)
    col_ref[...] = jnp.concatenate([b_f, b_b, cm_f, cm_b, pad], axis=0).T


def _in_proj(x, norm_w, w_all, w_u, w_gr, gate_bias, conv_w, conv_b):
    L = x.shape[0]
    tm = PROJ_TM
    assert L % tm == 0 and tm % CHUNK == 0
    hb = tm // PROJ_HALO
    nblk8 = L // PROJ_HALO
    whole = pl.BlockSpec(memory_space=pltpu.VMEM)
    return pl.pallas_call(
        _in_proj_body,
        grid=(L // tm,),
        in_specs=[
            pl.BlockSpec((PROJ_HALO, D_MODEL), lambda i: (jnp.maximum(i * hb - 1, 0), 0)),
            pl.BlockSpec((tm, D_MODEL), lambda i: (i, 0)),
            pl.BlockSpec((PROJ_HALO, D_MODEL), lambda i: (jnp.minimum((i + 1) * hb, nblk8 - 1), 0)),
            whole, whole, whole, whole, whole, whole, whole,
        ],
        out_specs=[
            pl.BlockSpec((tm, D_MLSTM), lambda i: (i, 0)),
            pl.BlockSpec((D_MLSTM, tm), lambda i: (0, i)),
            pl.BlockSpec((tm, D_MLSTM), lambda i: (i, 0)),
            pl.BlockSpec((tm, D_MLSTM), lambda i: (i, 0)),
            pl.BlockSpec((tm, D_S5), lambda i: (i, 0)),
            pl.BlockSpec((N_GATES, tm), lambda i: (0, i)),
            pl.BlockSpec((tm, GATE_COL_LANES), lambda i: (i, 0)),
        ],
        out_shape=[
            jax.ShapeDtypeStruct((L, D_MLSTM), BF16),
            jax.ShapeDtypeStruct((D_MLSTM, L), BF16),
            jax.ShapeDtypeStruct((L, D_MLSTM), BF16),
            jax.ShapeDtypeStruct((L, D_MLSTM), F32),
            jax.ShapeDtypeStruct((L, D_S5), F32),
            jax.ShapeDtypeStruct((N_GATES, L), F32),
            jax.ShapeDtypeStruct((L, GATE_COL_LANES), F32),
        ],
        compiler_params=pltpu.CompilerParams(
            dimension_semantics=("parallel",), vmem_limit_bytes=VMEM_LIMIT_BYTES),
        name="in_proj",
    )(x, x, x, norm_w, w_all, w_u, w_gr, gate_bias, conv_w, conv_b)


def _bf16_split3(x):
    hi = x.astype(BF16)
    r1 = x - hi.astype(F32)
    mid = r1.astype(BF16)
    lo = (r1 - mid.astype(F32)).astype(BF16)
    return hi, mid, lo


def _mlstm_direction(q_ref, kt_ref, v_ref, rb_ref, col_ref, h_ref, ct_ref, m_ref, ml_ref, d, sub):
    T = CHUNK
    H = N_HEADS
    backward = d == 1
    tt = slice(sub * T, (sub + 1) * T)
    rr = lax.broadcasted_iota(jnp.int32, (T, T), 0)
    cc = lax.broadcasted_iota(jnp.int32, (T, T), 1)
    mask = (cc >= rr) if backward else (cc <= rr)
    last = 0 if backward else T - 1

    r_rows = rb_ref[d * H:(d + 1) * H, tt]
    b_rows = rb_ref[(2 + d) * H:(3 + d) * H, tt]
    b_cols = col_ref[tt, d * H:(d + 1) * H]
    cm_cols = col_ref[tt, (2 + d) * H:(3 + d) * H]
    m_prev = m_ref[d]
    m_prev_l = ml_ref[d]
    m_cols = jnp.maximum(cm_cols, m_prev_l)
    clamp_cols = -(b_cols + m_cols)
    m_last = jnp.maximum(jnp.max(r_rows, axis=1, keepdims=True), m_prev)
    b_tot = jnp.broadcast_to(b_rows[:, last:last + 1], (H, T))
    sc_rows = jnp.exp(m_prev - m_last)
    wkk_rows = jnp.exp(r_rows - m_last)
    ones = jnp.ones((T, HEAD_DIM), BF16)

    for h in range(H):
        ci = d * H + h
        hs = slice(h * HEAD_DIM, (h + 1) * HEAD_DIM)
        q = q_ref[tt, hs]
        kt = kt_ref[hs, tt]
        vaug = jnp.concatenate([v_ref[tt, hs], ones], axis=1)
        ct_prev = ct_ref[ci]
        m_col = jnp.broadcast_to(m_cols[:, h:h + 1], (T, T))
        w = jnp.exp(jnp.where(mask, r_rows[h:h + 1, :] - m_col, -jnp.inf))
        sq = jnp.dot(q, jnp.concatenate([kt, ct_prev.astype(BF16)], axis=1), preferred_element_type=F32)
        s = sq[:, :T] * w
        qc = sq[:, T:]
        kw = (kt.astype(F32) * wkk_rows[h:h + 1, :]).astype(BF16)
        both = jnp.dot(jnp.concatenate([s.astype(BF16), kw], axis=0), vaug, preferred_element_type=F32)
        sv = both[:T]
        upd = both[T:]
        s_inter = jnp.exp(m_prev[h:h + 1, :] - m_col)
        num = sv[:, :HEAD_DIM] + s_inter * qc[:, :HEAD_DIM]
        den = sv[:, HEAD_DIM:] + s_inter * qc[:, HEAD_DIM:]
        floor = jnp.exp(jnp.broadcast_to(clamp_cols[:, h:h + 1], (T, HEAD_DIM)))
        h_ref[tt, hs] = num / jnp.maximum(jnp.abs(den), floor)

        sc = sc_rows[h:h + 1, :]
        ct_ref[ci] = jnp.concatenate([sc, sc], axis=1) * ct_prev + upd

    m_ref[d] = b_tot + m_last
    ml_ref[d] = b_cols[last:last + 1, :] + m_cols[last:last + 1, :]


def _mlstm_body(qf_ref, ktf_ref, vf_ref, rbf_ref, colf_ref, qb_ref, ktb_ref, vb_ref, rbb_ref, colb_ref,
                hf_ref, hb_ref, ct_ref, m_ref, ml_ref):
    @pl.when(pl.program_id(0) == 0)
    def _():
        ct_ref[...] = jnp.zeros_like(ct_ref)
        m_ref[...] = jnp.full_like(m_ref, M_INIT)
        ml_ref[...] = jnp.full_like(ml_ref, M_INIT)

    for sub in range(MLSTM_SUB):
        _mlstm_direction(qf_ref, ktf_ref, vf_ref, rbf_ref, colf_ref, hf_ref, ct_ref, m_ref, ml_ref, 0, sub)
        _mlstm_direction(qb_ref, ktb_ref, vb_ref, rbb_ref, colb_ref, hb_ref, ct_ref, m_ref, ml_ref, 1,
                         MLSTM_SUB - 1 - sub)


def _mlstm(q, kt, v, rb, col):
    L = q.shape[0]
    T = MLSTM_SUB * CHUNK
    assert L % T == 0 and CHUNK == HEAD_DIM
    nc = L // T
    fwd = lambda c: (c, 0)
    bwd = lambda c: (nc - 1 - c, 0)
    fwd_r = lambda c: (0, c)
    bwd_r = lambda c: (0, nc - 1 - c)
    return pl.pallas_call(
        _mlstm_body,
        grid=(nc,),
        in_specs=[
            pl.BlockSpec((T, D_MLSTM), fwd), pl.BlockSpec((D_MLSTM, T), fwd_r), pl.BlockSpec((T, D_MLSTM), fwd),
            pl.BlockSpec((N_GATES, T), fwd_r), pl.BlockSpec((T, GATE_COL_LANES), fwd),
            pl.BlockSpec((T, D_MLSTM), bwd), pl.BlockSpec((D_MLSTM, T), bwd_r), pl.BlockSpec((T, D_MLSTM), bwd),
            pl.BlockSpec((N_GATES, T), bwd_r), pl.BlockSpec((T, GATE_COL_LANES), bwd),
        ],
        out_specs=[pl.BlockSpec((T, D_MLSTM), fwd), pl.BlockSpec((T, D_MLSTM), bwd)],
        out_shape=[jax.ShapeDtypeStruct((L, D_MLSTM), F32), jax.ShapeDtypeStruct((L, D_MLSTM), F32)],
        scratch_shapes=[
            pltpu.VMEM((2 * N_HEADS, HEAD_DIM, 2 * HEAD_DIM), F32),
            pltpu.VMEM((2, N_HEADS, CHUNK), F32),
            pltpu.VMEM((2, 1, N_HEADS), F32),
        ],
        compiler_params=pltpu.CompilerParams(
            dimension_semantics=("arbitrary",), vmem_limit_bytes=VMEM_LIMIT_BYTES),
        name="mlstm",
    )(q, kt, v, rb, col, q, kt, v, rb, col)


S5_PAIR_CH = 2 * S5_GC
S5_ROW = S5_BLK * S5_PAIR_CH
S5_ST = 2 * S5_P
S5_TILE = 8
S5_LANE_PAIRS = 4
S5_TT = 4096
S5_NBT = S5_TT // S5_BLK


def _s5_scan(a, sin_ref, nb):
    R = S5_TILE
    ntile = nb // R
    row = lax.broadcasted_iota(jnp.int32, (R, S5_ST), 0)
    zero = jnp.zeros((1, S5_ST), F32)

    def cmul(x, y):
        return x[0] * y[0] - x[1] * y[1], x[0] * y[1] + x[1] * y[0]

    def bcast(x):
        return tuple(jnp.broadcast_to(t, (R, S5_ST)) for t in x)

    def tables(ar, ai, backward):
        pw = {1: (ar, ai)}
        for e in range(2, R + 1):
            pw[e] = cmul(pw[e // 2], pw[e - e // 2])
        steps = []
        for sh in (1, 2, 4):
            keep = (row < R - sh) if backward else (row >= sh)
            steps.append(tuple(jnp.where(keep, t, 0.0) for t in bcast(pw[sh])))
        order = range(R, 0, -1) if backward else range(1, R + 1)
        carry_pw = tuple(jnp.concatenate([pw[e][j] for e in order], axis=0) for j in (0, 1))
        return steps, carry_pw

    def scan_tile(x, carry, steps, cpw, backward):
        for sh, am in zip((1, 2, 4), steps):
            rs = (R - sh) if backward else sh
            x = tuple(p + q for p, q in zip(x, cmul(am, (pltpu.roll(x[0], rs, 0), pltpu.roll(x[1], rs, 0)))))
        cb = bcast(carry)
        x = tuple(p + q for p, q in zip(x, cmul(cpw, cb)))
        edge, rs = (R - 1, R - 1) if backward else (0, 1)
        enter = tuple(jnp.where(row == edge, c, pltpu.roll(t, rs, 0)) for t, c in zip(x, cb))
        last = 0 if backward else R - 1
        return enter, (x[0][last:last + 1], x[1][last:last + 1])

    tabs = []
    for p in range(S5_LANE_PAIRS):
        ap = a[p]
        tabs.append((tables(ap[0:1], ap[1:2], False), tables(ap[2:3], ap[3:4], True)))

    def step(i, carry):
        rf = pl.multiple_of(i * R, R)
        rb = pl.multiple_of((ntile - 1 - i) * R, R)
        out = []
        for p in range(S5_LANE_PAIRS):
            (steps_f, cpw_f), (steps_b, cpw_b) = tabs[p]
            cf, cb = carry[p]
            xf = (sin_ref[p, pl.ds(rf, R), 0:S5_ST], sin_ref[p, pl.ds(rf, R), S5_ST:2 * S5_ST])
            xb = (sin_ref[p, pl.ds(rb, R), 2 * S5_ST:3 * S5_ST], sin_ref[p, pl.ds(rb, R), 3 * S5_ST:4 * S5_ST])
            ef, cf = scan_tile(xf, cf, steps_f, cpw_f, False)
            eb, cb = scan_tile(xb, cb, steps_b, cpw_b, True)
            sin_ref[p, pl.ds(rf, R), 0:S5_ST] = ef[0]
            sin_ref[p, pl.ds(rf, R), S5_ST:2 * S5_ST] = ef[1]
            sin_ref[p, pl.ds(rb, R), 2 * S5_ST:3 * S5_ST] = eb[0]
            sin_ref[p, pl.ds(rb, R), 3 * S5_ST:4 * S5_ST] = eb[1]
            out.append((cf, cb))
        return tuple(out)

    init = tuple(((zero, zero), (zero, zero)) for _ in range(S5_LANE_PAIRS))
    lax.fori_loop(0, ntile, step, init)


def _s5_body(u_ref, m_ref, win_ref, wout_ref, a_ref, y_ref, u2_ref, sin_ref, y2_ref):
    phase = pl.program_id(1)
    t = pl.program_id(2)
    nb = u2_ref.shape[1]
    r0 = pl.multiple_of(t * S5_NBT, S5_NBT)
    rows = pl.ds(r0, S5_NBT)

    n_slot = S5_LANE_PAIRS
    lt = S5_LANE_PAIRS * S5_PAIR_CH
    slot = lax.broadcasted_iota(jnp.int32, (S5_NBT, lt), 1) // S5_PAIR_CH

    def pick(parts, first):
        out = parts[n_slot - 1]
        for i in range(n_slot - 2, -1, -1):
            out = jnp.where(slot == (first + i) % n_slot, parts[i], out)
        return out

    @pl.when(phase == 0)
    def _():
        for q in range(S5_BLK // n_slot):
            rot = []
            for i in range(n_slot):
                tok = u_ref[pl.ds(n_slot * q + i, S5_NBT, stride=S5_BLK), :]
                rot.append(pltpu.roll(tok, i * S5_PAIR_CH, 1) if i else tok)
            for p in range(S5_LANE_PAIRS):
                u2_ref[p, rows, q * lt:(q + 1) * lt] = pick(rot, p).astype(BF16)
        for p in range(S5_LANE_PAIRS):
            sin_ref[p, rows, :] = jnp.dot(u2_ref[p, rows, :], win_ref[p], preferred_element_type=F32)

    @pl.when((phase == 0) & (t == pl.num_programs(2) - 1))
    def _():
        _s5_scan(a_ref[...], sin_ref, nb)

    @pl.when(phase == 1)
    def _():
        for p in range(S5_LANE_PAIRS):
            y2_ref[p] = (jnp.dot(u2_ref[p, rows, :], m_ref[p], preferred_element_type=F32)
                         + jnp.dot(sin_ref[p, rows, :].astype(BF16), wout_ref[p], preferred_element_type=F32))
        for tkn in range(S5_BLK):
            q, i = divmod(tkn, n_slot)
            merged = pick([y2_ref[p, :, q * lt:(q + 1) * lt] for p in range(S5_LANE_PAIRS)], i)
            out = pltpu.roll(merged, ((n_slot - i) % n_slot) * S5_PAIR_CH, 1) if i else merged
            y_ref[pl.ds(tkn, S5_NBT, stride=S5_BLK), :] = out


def _s5(u, m2, win2, wout2, a2):
    L = u.shape[0]
    assert L % S5_TT == 0 and D_S5 == S5_PAIRS * S5_PAIR_CH
    nb = L // S5_BLK
    nq = S5_PAIRS // S5_LANE_PAIRS
    lane_tile = S5_LANE_PAIRS * S5_PAIR_CH
    wspec = pl.BlockSpec((S5_LANE_PAIRS, S5_ROW, S5_ROW), lambda q, ph, t: (q, 0, 0))
    return pl.pallas_call(
        _s5_body,
        grid=(nq, 2, L // S5_TT),
        in_specs=[
            pl.BlockSpec((S5_TT, lane_tile), lambda q, ph, t: (t, q)),
            wspec, wspec, wspec,
            pl.BlockSpec((S5_LANE_PAIRS, 4, S5_ST), lambda q, ph, t: (q, 0, 0)),
        ],
        out_specs=pl.BlockSpec((S5_TT, lane_tile), lambda q, ph, t: (t * ph, q)),
        out_shape=jax.ShapeDtypeStruct((L, D_S5), F32),
        scratch_shapes=[
            pltpu.VMEM((S5_LANE_PAIRS, nb, S5_ROW), BF16),
            pltpu.VMEM((S5_LANE_PAIRS, nb, 4 * S5_ST), F32),
            pltpu.VMEM((S5_LANE_PAIRS, S5_NBT, S5_ROW), F32),
        ],
        compiler_params=pltpu.CompilerParams(
            dimension_semantics=("parallel", "arbitrary", "arbitrary"), vmem_limit_bytes=VMEM_LIMIT_BYTES),
        name="s5",
    )(u, m2, win2, wout2, a2)


S5_EXP = (S5_BLK + 1) * S5_PAIR_CH
S5_EXP_PAD = 640
S5_PWT_ROWS = 24


def _s5_prep_body(pwk_ref, pwt_ref, ct_ref, b2t_ref, d_ref, m_ref, win_ref, wout_ref):
    nt, w = S5_BLK, S5_PAIR_CH
    lane = lax.broadcasted_iota(jnp.int32, (S5_ST, S5_EXP_PAD), 1)
    row = lax.broadcasted_iota(jnp.int32, (S5_ST, S5_EXP_PAD), 0)
    sel_slot = (lane // w == row).astype(BF16)
    sel_chan = ((lane % S5_GC == row) & (row < S5_GC)).astype(BF16)
    same_group = (row // S5_P == (lane // S5_GC) % 2) & (lane < S5_EXP)

    def expand(x, sel):
        return sum(jnp.dot(p, sel, preferred_element_type=F32) for p in _bf16_split3(x))

    def split_dot(a, x):
        a_hi, a_lo, _ = _bf16_split3(a)
        x_hi, x_lo, _ = _bf16_split3(x)
        return (jnp.dot(a_hi, x_hi, preferred_element_type=F32) + jnp.dot(a_hi, x_lo, preferred_element_type=F32)
                + jnp.dot(a_lo, x_hi, preferred_element_type=F32))

    lane_m = lax.broadcasted_iota(jnp.int32, (w, S5_ROW), 1)
    row_m = lax.broadcasted_iota(jnp.int32, (w, S5_ROW), 0)
    krow, wout_rows, win_rows = [], [], []
    for z in range(2):
        pr, pi = expand(pwk_ref[z, 0, 0], sel_slot), expand(pwk_ref[z, 1, 0], sel_slot)
        cr, ci = expand(ct_ref[z, 0, 0], sel_chan), expand(ct_ref[z, 1, 0], sel_chan)
        xr = jnp.where(same_group, pr * cr - pi * ci, 0.0)
        xi = jnp.where(same_group, pr * ci + pi * cr, 0.0)
        lo = w if z == 0 else 0
        wout_rows += [xr[:, lo:lo + S5_ROW], -xi[:, lo:lo + S5_ROW]]
        k_all = split_dot(b2t_ref[z, 0, 0], xr) - split_dot(b2t_ref[z, 1, 0], xi)
        lo = 0 if z == 0 else w
        krow.append(k_all[:, lo:lo + S5_ROW])
        br, bi = b2t_ref[z, 0, 0], b2t_ref[z, 1, 0]
        blocks = []
        for s in range(nt):
            e = nt - 1 - s if z == 0 else s
            qr, qi = pwt_ref[z, 0, 0, e:e + 1, :], pwt_ref[z, 1, 0, e:e + 1, :]
            blocks.append((br * qr - bi * qi, br * qi + bi * qr))
        win_rows.append(blocks)

    kf, kb = krow
    d_diag = jnp.where((lane_m % w) == row_m, d_ref[0], 0.0)
    m_rows = []
    for s in range(nt):
        f = jnp.where(lane_m >= w * s, pltpu.roll(kf, w * s, 1), 0.0) if s else kf
        sh = w * (nt - 1 - s)
        b = jnp.where(lane_m < S5_ROW - sh, pltpu.roll(kb, S5_ROW - sh, 1), 0.0) if sh else kb
        dd = jnp.where(lane_m // w == s, d_diag, 0.0)
        m_rows.append(f + b + dd)

    n_slot = S5_LANE_PAIRS
    lt = n_slot * w

    pp = pl.program_id(1)

    def order_cols(x):
        return jnp.concatenate([pltpu.roll(x[:, q * lt:(q + 1) * lt], pp * w, 1) for q in range(S5_ROW // lt)], axis=1)

    for k, x in enumerate(wout_rows):
        wout_ref[0, k * S5_ST:(k + 1) * S5_ST, :] = order_cols(x).astype(BF16)
    for s in range(nt):
        r0 = pl.multiple_of((n_slot * (s // n_slot) + (s % n_slot + pp) % n_slot) * w, w)
        m_ref[0, pl.ds(r0, w), :] = order_cols(m_rows[s]).astype(BF16)
        for z in range(2):
            wr, wi = win_rows[z][s]
            win_ref[0, pl.ds(r0, w), (2 * z) * S5_ST:(2 * z + 1) * S5_ST] = wr.astype(BF16)
            win_ref[0, pl.ds(r0, w), (2 * z + 1) * S5_ST:(2 * z + 2) * S5_ST] = wi.astype(BF16)


def _s5_prep(pwk, pwt, ct, b2t, d2):
    npair = d2.shape[0]
    n = S5_LANE_PAIRS
    spec = lambda r, c: pl.BlockSpec((2, 2, 1, r, c), lambda g, p: (0, 0, g * n + p, 0, 0))
    out = pl.BlockSpec((1, S5_ROW, S5_ROW), lambda g, p: (g * n + p, 0, 0))
    shape = jax.ShapeDtypeStruct((npair, S5_ROW, S5_ROW), BF16)
    return pl.pallas_call(
        _s5_prep_body,
        grid=(npair // n, n),
        in_specs=[spec(S5_ST, S5_ST), spec(S5_PWT_ROWS, S5_ST), spec(S5_ST, S5_ST), spec(S5_PAIR_CH, S5_ST),
                  pl.BlockSpec((1, 1, S5_ROW), lambda g, p: (g * n + p, 0, 0))],
        out_specs=[out, out, out],
        out_shape=[shape, shape, shape],
        compiler_params=pltpu.CompilerParams(
            dimension_semantics=("parallel", "arbitrary"), vmem_limit_bytes=VMEM_LIMIT_BYTES),
        name="s5_prep",
    )(pwk, pwt, ct, b2t, d2)


def _s5_weights(a_re, a_im, log_dt, b_re, b_im, c_re, c_im, d_skip):
    nt, npair = S5_BLK, S5_PAIRS
    lam = lax.complex(a_re, a_im)
    dt = jnp.exp(log_dt)[..., None]
    lam_bar = jnp.exp(lam * dt)
    b_bar = ((lam_bar - 1.0) / lam)[..., None] * lax.complex(b_re, b_im)
    taus = jnp.arange(nt + 1, dtype=F32)
    pw = jnp.exp((lam * dt)[..., None] * taus).reshape(2, npair, S5_ST, nt + 1)
    ri = lambda x: jnp.stack([jnp.real(x), jnp.imag(x)], axis=1)
    pwk = jnp.stack([pw[0], pw[1, ..., ::-1]])
    pwk = jnp.pad(ri(pwk), ((0, 0), (0, 0), (0, 0), (0, 0), (0, S5_ST - (nt + 1))))
    pwt = jnp.pad(ri(pw).transpose(0, 1, 2, 4, 3), ((0, 0), (0, 0), (0, 0), (0, S5_PWT_ROWS - (nt + 1)), (0, 0)))
    ct = lax.complex(c_re, c_im).transpose(0, 1, 3, 2).reshape(2, npair, S5_ST, S5_GC)
    ct = jnp.pad(ri(ct), ((0, 0), (0, 0), (0, 0), (0, 0), (0, S5_ST - S5_GC)))
    bb = b_bar.reshape(2, npair, 2, S5_P, S5_GC)
    eye2 = jnp.eye(2, dtype=F32)
    b2t = (bb.transpose(0, 1, 2, 4, 3)[:, :, :, :, None, :] * eye2[None, None, :, None, :, None])
    b2t = ri(b2t.reshape(2, npair, S5_PAIR_CH, S5_ST))
    d2 = jnp.tile(d_skip.reshape(npair, 1, S5_PAIR_CH), (1, 1, nt))
    m2, win2, wout2 = _s5_prep(pwk, pwt, ct, b2t, d2)
    a_blk = pw[..., nt]
    a2 = jnp.stack([jnp.real(a_blk[0]), jnp.imag(a_blk[0]), jnp.real(a_blk[1]), jnp.imag(a_blk[1])], axis=1)
    return m2, win2, wout2, a2


def _mix_body(x_ref, hf_ref, hb_ref, o_ref, y_ref, nw_ref, wglu_ref, wout_ref, out_ref):
    h = hf_ref[...] + hb_ref[...]
    parts = []
    for hd in range(N_HEADS):
        hh = h[:, hd * HEAD_DIM:(hd + 1) * HEAD_DIM]
        mu = jnp.mean(hh, axis=-1, keepdims=True)
        var = jnp.mean(jnp.square(hh - mu), axis=-1, keepdims=True)
        parts.append((hh - mu) * lax.rsqrt(var + EPS))
    hn = jnp.concatenate(parts, axis=1)
    h_m = hn * nw_ref[...] * _sigmoid(o_ref[...])
    y = y_ref[...]
    gelu = 0.5 * y * (1.0 + jnp.tanh(math.sqrt(2.0 / math.pi) * (y + 0.044715 * (y * y * y))))
    ab = jnp.dot(gelu.astype(BF16), wglu_ref[...], preferred_element_type=F32)
    h_s = ab[:, :D_S5] * _sigmoid(ab[:, D_S5:])
    mix = jnp.dot(h_m.astype(BF16), wout_ref[0:D_MLSTM, :], preferred_element_type=F32)
    mix += jnp.dot(h_s.astype(BF16), wout_ref[D_MLSTM:, :], preferred_element_type=F32)
    out_ref[...] = x_ref[...] + mix


def _mix(x, hf, hb, o_in, y, norm_w, w_glu, w_out):
    L = x.shape[0]
    tm = MIX_TM
    assert L % tm == 0
    whole = pl.BlockSpec(memory_space=pltpu.VMEM)
    row = lambda n: pl.BlockSpec((tm, n), lambda i: (i, 0))
    return pl.pallas_call(
        _mix_body,
        grid=(L // tm,),
        in_specs=[row(D_MODEL), row(D_MLSTM), row(D_MLSTM), row(D_MLSTM), row(D_S5), whole, whole, whole],
        out_specs=row(D_MODEL),
        out_shape=jax.ShapeDtypeStruct((L, D_MODEL), F32),
        compiler_params=pltpu.CompilerParams(
            dimension_semantics=("parallel",), vmem_limit_bytes=VMEM_LIMIT_BYTES),
        name="mix",
    )(x, hf, hb, o_in, y, norm_w, w_glu, w_out)


def _encode(x, p):
    x = _ffn(x, p["norm_ffn1"], p["ffn1_w_gate"], p["ffn1_w_up"], p["ffn1_w_down"], p["norm_final"],
             final_norm=False)
    q, kt, v, o_in, u, rb, col = _in_proj(x, p["norm_mix"], p["w_all"], p["w_u"], p["w_gr"], p["gate_bias_r"],
                                          p["conv_w"], p["conv_b"])
    hf, hb = _mlstm(q, kt, v, rb, col)
    y = _s5(u, p["s5_m"], p["s5_win"], p["s5_wout"], p["s5_a"])
    x = _mix(x, hf, hb, o_in, y, p["mlstm_norm_w"], p["s5_w_glu"], p["w_out"])
    return _ffn(x, p["norm_ffn2"], p["ffn2_w_gate"], p["ffn2_w_up"], p["ffn2_w_down"], p["norm_final"],
                final_norm=True)


def _prepare(norm_ffn1, ffn1_w_gate, ffn1_w_up, ffn1_w_down, norm_mix, w_in, conv_w, conv_b, b_igate, b_fgate,
             mlstm_norm_w, s5_a_re, s5_a_im, s5_log_dt, s5_b_re, s5_b_im, s5_c_re, s5_c_im, s5_d, s5_w_glu,
             w_out, norm_ffn2, ffn2_w_gate, ffn2_w_up, ffn2_w_down, norm_final):
    l = 0
    w = w_in[l]
    g0 = 4 * D_MLSTM
    w_g = w[:, g0:g0 + N_GATES]
    gate_bias = jnp.concatenate([b_igate[l].reshape(-1), b_fgate[l].reshape(-1)])
    m2, win2, wout2, a2 = _s5_weights(s5_a_re[l], s5_a_im[l], s5_log_dt[l], s5_b_re[l], s5_b_im[l],
                                      s5_c_re[l], s5_c_im[l], s5_d[l])
    row = lambda a: a.reshape(1, -1).astype(F32)
    return {
        "norm_ffn1": row(norm_ffn1[l]), "norm_ffn2": row(norm_ffn2[l]), "norm_final": row(norm_final),
        "ffn1_w_gate": ffn1_w_gate[l].astype(BF16), "ffn1_w_up": ffn1_w_up[l].astype(BF16),
        "ffn1_w_down": ffn1_w_down[l].astype(BF16),
        "ffn2_w_gate": ffn2_w_gate[l].astype(BF16), "ffn2_w_up": ffn2_w_up[l].astype(BF16),
        "ffn2_w_down": ffn2_w_down[l].astype(BF16),
        "norm_mix": row(norm_mix[l]),
        "w_all": w.astype(BF16),
        "w_u": w[:, g0 + N_GATES:].astype(BF16),
        "w_gr": w_g.T.astype(BF16),
        "conv_w": conv_w[l].astype(F32), "conv_b": row(conv_b[l]),
        "gate_bias_r": gate_bias.reshape(N_GATES, 1),
        "mlstm_norm_w": row(mlstm_norm_w[l]),
        "s5_m": m2, "s5_win": win2, "s5_wout": wout2, "s5_a": a2,
        "s5_w_glu": s5_w_glu[l].astype(BF16), "w_out": w_out[l].astype(BF16),
    }


def kernel(x_prompt, x_sample, norm_ffn1, ffn1_w_gate, ffn1_w_up, ffn1_w_down, norm_mix, w_in, conv_w, conv_b, b_igate, b_fgate, mlstm_norm_w, s5_a_re, s5_a_im, s5_log_dt, s5_b_re, s5_b_im, s5_c_re, s5_c_im, s5_d, s5_w_glu, w_out, norm_ffn2, ffn2_w_gate, ffn2_w_up, ffn2_w_down, norm_final):
    assert norm_ffn1.shape[0] == 1 and x_prompt.shape[0] == 1 and x_sample.shape[0] == 1
    p = _prepare(norm_ffn1, ffn1_w_gate, ffn1_w_up, ffn1_w_down, norm_mix, w_in, conv_w, conv_b, b_igate,
                 b_fgate, mlstm_norm_w, s5_a_re, s5_a_im, s5_log_dt, s5_b_re, s5_b_im, s5_c_re, s5_c_im, s5_d,
                 s5_w_glu, w_out, norm_ffn2, ffn2_w_gate, ffn2_w_up, ffn2_w_down, norm_final)
    y_prompt = _encode(x_prompt[0], p)[None]
    y_sample = _encode(x_sample[0], p)[None]
    return (y_prompt, y_sample)
```

```python
import functools
import math

import jax
import jax.numpy as jnp
from jax import lax
from jax.experimental import pallas as pl
from jax.experimental.pallas import tpu as pltpu

F32 = jnp.float32
BF16 = jnp.bfloat16

D_MODEL = 2048
D_MLSTM = 1024
D_S5 = 1024
N_HEADS = 8
HEAD_DIM = 128
CHUNK = 128
MLSTM_SUB = 4
N_GATES = 32
GATE_COL_LANES = 128
S5_GROUPS = 64
S5_GC = 16
S5_P = 64
S5_BLK = 16
S5_PAIRS = S5_GROUPS // 2
D_FF = 5632
EPS = 1e-6
M_INIT = -1e30

VMEM_LIMIT_BYTES = 56 * 1024 * 1024

FFN_TM = 1024
FFN_TF = 512
PROJ_TM = 256
PROJ_HALO = 8
MIX_TM = 512


def _sigmoid(x):
    return 1.0 / (1.0 + jnp.exp(-x))


def _rmsnorm(x, w):
    return x * lax.rsqrt(jnp.mean(x * x, axis=-1, keepdims=True) + EPS) * w


def _ffn_body(x_ref, nw_ref, wg_ref, wu_ref, wd_ref, nf_ref, o_ref, xn_ref, *, final_norm):
    j = pl.program_id(1)

    @pl.when(j == 0)
    def _():
        x = x_ref[...]
        xn_ref[...] = _rmsnorm(x, nw_ref[...]).astype(BF16)
        o_ref[...] = x

    xn = xn_ref[...]
    g = jnp.dot(xn, wg_ref[...], preferred_element_type=F32)
    u = jnp.dot(xn, wu_ref[...], preferred_element_type=F32)
    h = (0.5 * g * _sigmoid(g)) * u
    o_ref[...] += jnp.dot(h.astype(BF16), wd_ref[...], preferred_element_type=F32)

    if final_norm:
        @pl.when(j == pl.num_programs(1) - 1)
        def _():
            o_ref[...] = _rmsnorm(o_ref[...], nf_ref[...])


def _ffn(x, norm_w, w_gate, w_up, w_down, norm_final, *, final_norm):
    L = x.shape[0]
    assert L % FFN_TM == 0 and D_FF % FFN_TF == 0
    return pl.pallas_call(
        functools.partial(_ffn_body, final_norm=final_norm),
        grid=(L // FFN_TM, D_FF // FFN_TF),
        in_specs=[
            pl.BlockSpec((FFN_TM, D_MODEL), lambda i, j: (i, 0)),
            pl.BlockSpec((1, D_MODEL), lambda i, j: (0, 0)),
            pl.BlockSpec((D_MODEL, FFN_TF), lambda i, j: (0, j)),
            pl.BlockSpec((D_MODEL, FFN_TF), lambda i, j: (0, j)),
            pl.BlockSpec((FFN_TF, D_MODEL), lambda i, j: (j, 0)),
            pl.BlockSpec((1, D_MODEL), lambda i, j: (0, 0)),
        ],
        out_specs=pl.BlockSpec((FFN_TM, D_MODEL), lambda i, j: (i, 0)),
        out_shape=jax.ShapeDtypeStruct((L, D_MODEL), F32),
        scratch_shapes=[pltpu.VMEM((FFN_TM, D_MODEL), BF16)],
        compiler_params=pltpu.CompilerParams(
            dimension_semantics=("parallel", "arbitrary"), vmem_limit_bytes=VMEM_LIMIT_BYTES),
        name="ffn_final" if final_norm else "ffn",
    )(x, norm_w, w_gate, w_up, w_down, norm_final)


QK_COLS = 2 * D_MLSTM
QK_CB = 512


def _chunk_scan(x, op, identity, backward):
    n = x.shape[-1]
    pos = lax.broadcasted_iota(jnp.int32, x.shape, x.ndim - 1) % CHUNK
    sh = 1
    while sh < CHUNK:
        if backward:
            x = op(x, jnp.where(pos < CHUNK - sh, pltpu.roll(x, n - sh, x.ndim - 1), identity))
        else:
            x = op(x, jnp.where(pos >= sh, pltpu.roll(x, sh, x.ndim - 1), identity))
        sh *= 2
    return x


def _in_proj_body(xp_ref, x_ref, xnx_ref, nw_ref, w_ref, wu_ref, gb_ref,
                  cw_ref, cb_ref, q_ref, kt_ref, v_ref, o_ref, u_ref, rb_ref, col_ref):
    i = pl.program_id(0)
    tm = x_ref.shape[0]
    nw = nw_ref[...]
    xn = _rmsnorm(x_ref[...], nw)
    xn_prev = jnp.where(i == 0, 0.0, _rmsnorm(xp_ref[...], nw))
    xn_next = jnp.where(i == pl.num_programs(0) - 1, 0.0, _rmsnorm(xnx_ref[...], nw))
    xn_b = xn.astype(BF16)
    xe_b = jnp.concatenate([xn_prev, xn, xn_next], axis=0).astype(BF16)
    rows = tm + 2 * PROJ_HALO

    w_gates = w_ref[:, 2 * QK_COLS:2 * QK_COLS + N_GATES]
    g = lax.dot_general(w_gates, xn_b, (((0,), (1,)), ((), ())), preferred_element_type=F32) + gb_ref[...]
    H = N_HEADS
    f_pre = g[2 * H:]
    lf = jnp.minimum(f_pre, 0.0) - jnp.log1p(jnp.exp(-jnp.abs(f_pre)))
    b_f = _chunk_scan(lf[:H], jnp.add, 0.0, False)
    b_b = _chunk_scan(lf[H:], jnp.add, 0.0, True)
    r_f = g[:H] - b_f
    r_b = g[H:2 * H] - b_b
    cm_f = _chunk_scan(r_f, jnp.maximum, -jnp.inf, False)
    cm_b = _chunk_scan(r_b, jnp.maximum, -jnp.inf, True)

    for c in range(QK_COLS // QK_CB):
        cs = slice(c * QK_CB, (c + 1) * QK_CB)
        z = jnp.dot(xe_b, w_ref[:, cs], preferred_element_type=F32)
        z_m1 = pltpu.roll(z, 1, 0)[PROJ_HALO:PROJ_HALO + tm]
        z_0 = z[PROJ_HALO:PROJ_HALO + tm]
        z_p1 = pltpu.roll(z, rows - 1, 0)[PROJ_HALO:PROJ_HALO + tm]
        y = z_m1 * cw_ref[0:1, cs] + z_0 * cw_ref[1:2, cs] + z_p1 * cw_ref[2:3, cs] + cb_ref[:, cs]
        y = y * _sigmoid(y)
        if c * QK_CB < D_MLSTM:
            q_ref[:, cs] = (y * (HEAD_DIM ** -0.5)).astype(q_ref.dtype)
        else:
            kt_ref[c * QK_CB - D_MLSTM:(c + 1) * QK_CB - D_MLSTM, :] = y.T.astype(kt_ref.dtype)

    v_ref[...] = jnp.dot(xn_b, w_ref[:, QK_COLS:QK_COLS + D_MLSTM], preferred_element_type=F32).astype(v_ref.dtype)
    o_ref[...] = jnp.dot(xn_b, w_ref[:, QK_COLS + D_MLSTM:QK_COLS + 2 * D_MLSTM], preferred_element_type=F32)
    u_ref[...] = jnp.dot(xn_b, wu_ref[...], preferred_element_type=F32)
    rb_ref[...] = jnp.concatenate([r_f, r_b, b_f, b_b], axis=0)
    pad = jnp.zeros((GATE_COL_LANES - 4 * H, tm), F32)
    col_ref[...] = jnp.concatenate([b_f, b_b, cm_f, cm_b, pad], axis=0).T


def _in_proj(x, norm_w, w_all, w_u, gate_bias, conv_w, conv_b):
    L = x.shape[0]
    tm = PROJ_TM
    assert L % tm == 0 and tm % CHUNK == 0
    hb = tm // PROJ_HALO
    nblk8 = L // PROJ_HALO
    whole = pl.BlockSpec(memory_space=pltpu.VMEM)
    return pl.pallas_call(
        _in_proj_body,
        grid=(L // tm,),
        in_specs=[
            pl.BlockSpec((PROJ_HALO, D_MODEL), lambda i: (jnp.maximum(i * hb - 1, 0), 0)),
            pl.BlockSpec((tm, D_MODEL), lambda i: (i, 0)),
            pl.BlockSpec((PROJ_HALO, D_MODEL), lambda i: (jnp.minimum((i + 1) * hb, nblk8 - 1), 0)),
            whole, whole, whole, whole, whole, whole,
        ],
        out_specs=[
            pl.BlockSpec((tm, D_MLSTM), lambda i: (i, 0)),
            pl.BlockSpec((D_MLSTM, tm), lambda i: (0, i)),
            pl.BlockSpec((tm, D_MLSTM), lambda i: (i, 0)),
            pl.BlockSpec((tm, D_MLSTM), lambda i: (i, 0)),
            pl.BlockSpec((tm, D_S5), lambda i: (i, 0)),
            pl.BlockSpec((N_GATES, tm), lambda i: (0, i)),
            pl.BlockSpec((tm, GATE_COL_LANES), lambda i: (i, 0)),
        ],
        out_shape=[
            jax.ShapeDtypeStruct((L, D_MLSTM), BF16),
            jax.ShapeDtypeStruct((D_MLSTM, L), BF16),
            jax.ShapeDtypeStruct((L, D_MLSTM), BF16),
            jax.ShapeDtypeStruct((L, D_MLSTM), F32),
            jax.ShapeDtypeStruct((L, D_S5), F32),
            jax.ShapeDtypeStruct((N_GATES, L), F32),
            jax.ShapeDtypeStruct((L, GATE_COL_LANES), F32),
        ],
        compiler_params=pltpu.CompilerParams(
            dimension_semantics=("parallel",), vmem_limit_bytes=VMEM_LIMIT_BYTES),
        name="in_proj",
    )(x, x, x, norm_w, w_all, w_u, gate_bias, conv_w, conv_b)


def _bf16_split3(x):
    hi = x.astype(BF16)
    r1 = x - hi.astype(F32)
    mid = r1.astype(BF16)
    lo = (r1 - mid.astype(F32)).astype(BF16)
    return hi, mid, lo


def _mlstm_direction(q_ref, kt_ref, v_ref, rb_ref, col_ref, h_ref, ct_ref, m_ref, ml_ref, d, sub):
    T = CHUNK
    H = N_HEADS
    backward = d == 1
    tt = slice(sub * T, (sub + 1) * T)
    rr = lax.broadcasted_iota(jnp.int32, (T, T), 0)
    cc = lax.broadcasted_iota(jnp.int32, (T, T), 1)
    mask = (cc >= rr) if backward else (cc <= rr)
    last = 0 if backward else T - 1

    r_rows = rb_ref[d * H:(d + 1) * H, tt]
    b_rows = rb_ref[(2 + d) * H:(3 + d) * H, tt]
    b_cols = col_ref[tt, d * H:(d + 1) * H]
    cm_cols = col_ref[tt, (2 + d) * H:(3 + d) * H]
    m_prev = m_ref[d]
    m_prev_l = ml_ref[d]
    m_cols = jnp.maximum(cm_cols, m_prev_l)
    clamp_cols = -(b_cols + m_cols)
    m_last = jnp.maximum(jnp.max(r_rows, axis=1, keepdims=True), m_prev)
    b_tot = jnp.broadcast_to(b_rows[:, last:last + 1], (H, T))
    sc_rows = jnp.exp(m_prev - m_last)
    wkk_rows = jnp.exp(r_rows - m_last)
    ones = jnp.ones((T, HEAD_DIM), BF16)

    for h in range(H):
        ci = d * H + h
        hs = slice(h * HEAD_DIM, (h + 1) * HEAD_DIM)
        q = q_ref[tt, hs]
        kt = kt_ref[hs, tt]
        vaug = jnp.concatenate([v_ref[tt, hs], ones], axis=1)
        ct_prev = ct_ref[ci]
        m_col = jnp.broadcast_to(m_cols[:, h:h + 1], (T, T))
        w = jnp.exp(jnp.where(mask, r_rows[h:h + 1, :] - m_col, -jnp.inf))
        sq = jnp.dot(q, jnp.concatenate([kt, ct_prev.astype(BF16)], axis=1), preferred_element_type=F32)
        s = sq[:, :T] * w
        qc = sq[:, T:]
        kw = (kt.astype(F32) * wkk_rows[h:h + 1, :]).astype(BF16)
        both = jnp.dot(jnp.concatenate([s.astype(BF16), kw], axis=0), vaug, preferred_element_type=F32)
        sv = both[:T]
        upd = both[T:]
        s_inter = jnp.exp(m_prev[h:h + 1, :] - m_col)
        num = sv[:, :HEAD_DIM] + s_inter * qc[:, :HEAD_DIM]
        den = sv[:, HEAD_DIM:] + s_inter * qc[:, HEAD_DIM:]
        floor = jnp.exp(jnp.broadcast_to(clamp_cols[:, h:h + 1], (T, HEAD_DIM)))
        h_ref[tt, hs] = num / jnp.maximum(jnp.abs(den), floor)

        sc = sc_rows[h:h + 1, :]
        ct_ref[ci] = jnp.concatenate([sc, sc], axis=1) * ct_prev + upd

    m_ref[d] = b_tot + m_last
    ml_ref[d] = b_cols[last:last + 1, :] + m_cols[last:last + 1, :]


def _mlstm_body(qf_ref, ktf_ref, vf_ref, rbf_ref, colf_ref, qb_ref, ktb_ref, vb_ref, rbb_ref, colb_ref,
                hf_ref, hb_ref, ct_ref, m_ref, ml_ref):
    @pl.when(pl.program_id(0) == 0)
    def _():
        ct_ref[...] = jnp.zeros_like(ct_ref)
        m_ref[...] = jnp.full_like(m_ref, M_INIT)
        ml_ref[...] = jnp.full_like(ml_ref, M_INIT)

    for sub in range(MLSTM_SUB):
        _mlstm_direction(qf_ref, ktf_ref, vf_ref, rbf_ref, colf_ref, hf_ref, ct_ref, m_ref, ml_ref, 0, sub)
        _mlstm_direction(qb_ref, ktb_ref, vb_ref, rbb_ref, colb_ref, hb_ref, ct_ref, m_ref, ml_ref, 1,
                         MLSTM_SUB - 1 - sub)


def _mlstm(q, kt, v, rb, col):
    L = q.shape[0]
    T = MLSTM_SUB * CHUNK
    assert L % T == 0 and CHUNK == HEAD_DIM
    nc = L // T
    fwd = lambda c: (c, 0)
    bwd = lambda c: (nc - 1 - c, 0)
    fwd_r = lambda c: (0, c)
    bwd_r = lambda c: (0, nc - 1 - c)
    return pl.pallas_call(
        _mlstm_body,
        grid=(nc,),
        in_specs=[
            pl.BlockSpec((T, D_MLSTM), fwd), pl.BlockSpec((D_MLSTM, T), fwd_r), pl.BlockSpec((T, D_MLSTM), fwd),
            pl.BlockSpec((N_GATES, T), fwd_r), pl.BlockSpec((T, GATE_COL_LANES), fwd),
            pl.BlockSpec((T, D_MLSTM), bwd), pl.BlockSpec((D_MLSTM, T), bwd_r), pl.BlockSpec((T, D_MLSTM), bwd),
            pl.BlockSpec((N_GATES, T), bwd_r), pl.BlockSpec((T, GATE_COL_LANES), bwd),
        ],
        out_specs=[pl.BlockSpec((T, D_MLSTM), fwd), pl.BlockSpec((T, D_MLSTM), bwd)],
        out_shape=[jax.ShapeDtypeStruct((L, D_MLSTM), F32), jax.ShapeDtypeStruct((L, D_MLSTM), F32)],
        scratch_shapes=[
            pltpu.VMEM((2 * N_HEADS, HEAD_DIM, 2 * HEAD_DIM), F32),
            pltpu.VMEM((2, N_HEADS, CHUNK), F32),
            pltpu.VMEM((2, 1, N_HEADS), F32),
        ],
        compiler_params=pltpu.CompilerParams(
            dimension_semantics=("arbitrary",), vmem_limit_bytes=VMEM_LIMIT_BYTES),
        name="mlstm",
    )(q, kt, v, rb, col, q, kt, v, rb, col)


S5_PAIR_CH = 2 * S5_GC
S5_ROW = S5_BLK * S5_PAIR_CH
S5_ST = 2 * S5_P
S5_TILE = 8
S5_LANE_PAIRS = 4
S5_TT = 4096
S5_NBT = S5_TT // S5_BLK


def _s5_scan(a, sin_ref, nb):
    R = S5_TILE
    ntile = nb // R
    row = lax.broadcasted_iota(jnp.int32, (R, S5_ST), 0)
    zero = jnp.zeros((1, S5_ST), F32)

    def cmul(x, y):
        return x[0] * y[0] - x[1] * y[1], x[0] * y[1] + x[1] * y[0]

    def bcast(x):
        return tuple(jnp.broadcast_to(t, (R, S5_ST)) for t in x)

    def tables(ar, ai, backward):
        pw = {1: (ar, ai)}
        for e in range(2, R + 1):
            pw[e] = cmul(pw[e // 2], pw[e - e // 2])
        steps = []
        for sh in (1, 2, 4):
            keep = (row < R - sh) if backward else (row >= sh)
            steps.append(tuple(jnp.where(keep, t, 0.0) for t in bcast(pw[sh])))
        order = range(R, 0, -1) if backward else range(1, R + 1)
        carry_pw = tuple(jnp.concatenate([pw[e][j] for e in order], axis=0) for j in (0, 1))
        return steps, carry_pw

    def scan_tile(x, carry, steps, cpw, backward):
        for sh, am in zip((1, 2, 4), steps):
            rs = (R - sh) if backward else sh
            x = tuple(p + q for p, q in zip(x, cmul(am, (pltpu.roll(x[0], rs, 0), pltpu.roll(x[1], rs, 0)))))
        cb = bcast(carry)
        x = tuple(p + q for p, q in zip(x, cmul(cpw, cb)))
        edge, rs = (R - 1, R - 1) if backward else (0, 1)
        enter = tuple(jnp.where(row == edge, c, pltpu.roll(t, rs, 0)) for t, c in zip(x, cb))
        last = 0 if backward else R - 1
        return enter, (x[0][last:last + 1], x[1][last:last + 1])

    tabs = []
    for p in range(S5_LANE_PAIRS):
        ap = a[p]
        tabs.append((tables(ap[0:1], ap[1:2], False), tables(ap[2:3], ap[3:4], True)))

    def step(i, carry):
        rf = pl.multiple_of(i * R, R)
        rb = pl.multiple_of((ntile - 1 - i) * R, R)
        out = []
        for p in range(S5_LANE_PAIRS):
            (steps_f, cpw_f), (steps_b, cpw_b) = tabs[p]
            cf, cb = carry[p]
            xf = (sin_ref[p, pl.ds(rf, R), 0:S5_ST], sin_ref[p, pl.ds(rf, R), S5_ST:2 * S5_ST])
            xb = (sin_ref[p, pl.ds(rb, R), 2 * S5_ST:3 * S5_ST], sin_ref[p, pl.ds(rb, R), 3 * S5_ST:4 * S5_ST])
            ef, cf = scan_tile(xf, cf, steps_f, cpw_f, False)
            eb, cb = scan_tile(xb, cb, steps_b, cpw_b, True)
            sin_ref[p, pl.ds(rf, R), 0:S5_ST] = ef[0]
            sin_ref[p, pl.ds(rf, R), S5_ST:2 * S5_ST] = ef[1]
            sin_ref[p, pl.ds(rb, R), 2 * S5_ST:3 * S5_ST] = eb[0]
            sin_ref[p, pl.ds(rb, R), 3 * S5_ST:4 * S5_ST] = eb[1]
            out.append((cf, cb))
        return tuple(out)

    init = tuple(((zero, zero), (zero, zero)) for _ in range(S5_LANE_PAIRS))
    lax.fori_loop(0, ntile, step, init)


def _s5_body(u_ref, m_ref, win_ref, wout_ref, a_ref, y_ref, u2_ref, sin_ref, y2_ref):
    phase = pl.program_id(1)
    t = pl.program_id(2)
    nb = u2_ref.shape[1]
    r0 = pl.multiple_of(t * S5_NBT, S5_NBT)
    rows = pl.ds(r0, S5_NBT)

    n_slot = S5_LANE_PAIRS
    lt = S5_LANE_PAIRS * S5_PAIR_CH
    slot = lax.broadcasted_iota(jnp.int32, (S5_NBT, lt), 1) // S5_PAIR_CH

    def pick(parts, first):
        out = parts[n_slot - 1]
        for i in range(n_slot - 2, -1, -1):
            out = jnp.where(slot == (first + i) % n_slot, parts[i], out)
        return out

    @pl.when(phase == 0)
    def _():
        for q in range(S5_BLK // n_slot):
            rot = []
            for i in range(n_slot):
                tok = u_ref[pl.ds(n_slot * q + i, S5_NBT, stride=S5_BLK), :]
                rot.append(pltpu.roll(tok, i * S5_PAIR_CH, 1) if i else tok)
            for p in range(S5_LANE_PAIRS):
                u2_ref[p, rows, q * lt:(q + 1) * lt] = pick(rot, p).astype(BF16)
        for p in range(S5_LANE_PAIRS):
            sin_ref[p, rows, :] = jnp.dot(u2_ref[p, rows, :], win_ref[p], preferred_element_type=F32)

    @pl.when((phase == 0) & (t == pl.num_programs(2) - 1))
    def _():
        _s5_scan(a_ref[...], sin_ref, nb)

    @pl.when(phase == 1)
    def _():
        for p in range(S5_LANE_PAIRS):
            y2_ref[p] = (jnp.dot(u2_ref[p, rows, :], m_ref[p], preferred_element_type=F32)
                         + jnp.dot(sin_ref[p, rows, :].astype(BF16), wout_ref[p], preferred_element_type=F32))
        for tkn in range(S5_BLK):
            q, i = divmod(tkn, n_slot)
            merged = pick([y2_ref[p, :, q * lt:(q + 1) * lt] for p in range(S5_LANE_PAIRS)], i)
            out = pltpu.roll(merged, ((n_slot - i) % n_slot) * S5_PAIR_CH, 1) if i else merged
            y_ref[pl.ds(tkn, S5_NBT, stride=S5_BLK), :] = out


def _s5(u, m2, win2, wout2, a2):
    L = u.shape[0]
    assert L % S5_TT == 0 and D_S5 == S5_PAIRS * S5_PAIR_CH
    nb = L // S5_BLK
    nq = S5_PAIRS // S5_LANE_PAIRS
    lane_tile = S5_LANE_PAIRS * S5_PAIR_CH
    wspec = pl.BlockSpec((S5_LANE_PAIRS, S5_ROW, S5_ROW), lambda q, ph, t: (q, 0, 0))
    return pl.pallas_call(
        _s5_body,
        grid=(nq, 2, L // S5_TT),
        in_specs=[
            pl.BlockSpec((S5_TT, lane_tile), lambda q, ph, t: (t, q)),
            wspec, wspec, wspec,
            pl.BlockSpec((S5_LANE_PAIRS, 4, S5_ST), lambda q, ph, t: (q, 0, 0)),
        ],
        out_specs=pl.BlockSpec((S5_TT, lane_tile), lambda q, ph, t: (t * ph, q)),
        out_shape=jax.ShapeDtypeStruct((L, D_S5), F32),
        scratch_shapes=[
            pltpu.VMEM((S5_LANE_PAIRS, nb, S5_ROW), BF16),
            pltpu.VMEM((S5_LANE_PAIRS, nb, 4 * S5_ST), F32),
            pltpu.VMEM((S5_LANE_PAIRS, S5_NBT, S5_ROW), F32),
        ],
        compiler_params=pltpu.CompilerParams(
            dimension_semantics=("parallel", "arbitrary", "arbitrary"), vmem_limit_bytes=VMEM_LIMIT_BYTES),
        name="s5",
    )(u, m2, win2, wout2, a2)


S5_EXP = (S5_BLK + 1) * S5_PAIR_CH
S5_EXP_PAD = 640
S5_PWT_ROWS = 24


def _s5_prep_body(pwk_ref, pwt_ref, ct_ref, b2t_ref, d_ref, m_ref, win_ref, wout_ref):
    nt, w = S5_BLK, S5_PAIR_CH
    lane = lax.broadcasted_iota(jnp.int32, (S5_ST, S5_EXP_PAD), 1)
    row = lax.broadcasted_iota(jnp.int32, (S5_ST, S5_EXP_PAD), 0)
    sel_slot = (lane // w == row).astype(BF16)
    sel_chan = ((lane % S5_GC == row) & (row < S5_GC)).astype(BF16)
    same_group = (row // S5_P == (lane // S5_GC) % 2) & (lane < S5_EXP)

    def expand(x, sel):
        return sum(jnp.dot(p, sel, preferred_element_type=F32) for p in _bf16_split3(x))

    def split_dot(a, x):
        a_hi, a_lo, _ = _bf16_split3(a)
        x_hi, x_lo, _ = _bf16_split3(x)
        return (jnp.dot(a_hi, x_hi, preferred_element_type=F32) + jnp.dot(a_hi, x_lo, preferred_element_type=F32)
                + jnp.dot(a_lo, x_hi, preferred_element_type=F32))

    lane_m = lax.broadcasted_iota(jnp.int32, (w, S5_ROW), 1)
    row_m = lax.broadcasted_iota(jnp.int32, (w, S5_ROW), 0)
    krow, wout_rows, win_rows = [], [], []
    for z in range(2):
        pr, pi = expand(pwk_ref[z, 0, 0], sel_slot), expand(pwk_ref[z, 1, 0], sel_slot)
        cr, ci = expand(ct_ref[z, 0, 0], sel_chan), expand(ct_ref[z, 1, 0], sel_chan)
        xr = jnp.where(same_group, pr * cr - pi * ci, 0.0)
        xi = jnp.where(same_group, pr * ci + pi * cr, 0.0)
        lo = w if z == 0 else 0
        wout_rows += [xr[:, lo:lo + S5_ROW], -xi[:, lo:lo + S5_ROW]]
        k_all = split_dot(b2t_ref[z, 0, 0], xr) - split_dot(b2t_ref[z, 1, 0], xi)
        lo = 0 if z == 0 else w
        krow.append(k_all[:, lo:lo + S5_ROW])
        br, bi = b2t_ref[z, 0, 0], b2t_ref[z, 1, 0]
        blocks = []
        for s in range(nt):
            e = nt - 1 - s if z == 0 else s
            qr, qi = pwt_ref[z, 0, 0, e:e + 1, :], pwt_ref[z, 1, 0, e:e + 1, :]
            blocks.append((br * qr - bi * qi, br * qi + bi * qr))
        win_rows.append(blocks)

    kf, kb = krow
    d_diag = jnp.where((lane_m % w) == row_m, d_ref[0], 0.0)
    m_rows = []
    for s in range(nt):
        f = jnp.where(lane_m >= w * s, pltpu.roll(kf, w * s, 1), 0.0) if s else kf
        sh = w * (nt - 1 - s)
        b = jnp.where(lane_m < S5_ROW - sh, pltpu.roll(kb, S5_ROW - sh, 1), 0.0) if sh else kb
        dd = jnp.where(lane_m // w == s, d_diag, 0.0)
        m_rows.append(f + b + dd)

    n_slot = S5_LANE_PAIRS
    lt = n_slot * w

    pp = pl.program_id(1)

    def order_cols(x):
        return jnp.concatenate([pltpu.roll(x[:, q * lt:(q + 1) * lt], pp * w, 1) for q in range(S5_ROW // lt)], axis=1)

    for k, x in enumerate(wout_rows):
        wout_ref[0, k * S5_ST:(k + 1) * S5_ST, :] = order_cols(x).astype(BF16)
    for s in range(nt):
        r0 = pl.multiple_of((n_slot * (s // n_slot) + (s % n_slot + pp) % n_slot) * w, w)
        m_ref[0, pl.ds(r0, w), :] = order_cols(m_rows[s]).astype(BF16)
        for z in range(2):
            wr, wi = win_rows[z][s]
            win_ref[0, pl.ds(r0, w), (2 * z) * S5_ST:(2 * z + 1) * S5_ST] = wr.astype(BF16)
            win_ref[0, pl.ds(r0, w), (2 * z + 1) * S5_ST:(2 * z + 2) * S5_ST] = wi.astype(BF16)


def _s5_prep(pwk, pwt, ct, b2t, d2):
    npair = d2.shape[0]
    n = S5_LANE_PAIRS
    spec = lambda r, c: pl.BlockSpec((2, 2, 1, r, c), lambda g, p: (0, 0, g * n + p, 0, 0))
    out = pl.BlockSpec((1, S5_ROW, S5_ROW), lambda g, p: (g * n + p, 0, 0))
    shape = jax.ShapeDtypeStruct((npair, S5_ROW, S5_ROW), BF16)
    return pl.pallas_call(
        _s5_prep_body,
        grid=(npair // n, n),
        in_specs=[spec(S5_ST, S5_ST), spec(S5_PWT_ROWS, S5_ST), spec(S5_ST, S5_ST), spec(S5_PAIR_CH, S5_ST),
                  pl.BlockSpec((1, 1, S5_ROW), lambda g, p: (g * n + p, 0, 0))],
        out_specs=[out, out, out],
        out_shape=[shape, shape, shape],
        compiler_params=pltpu.CompilerParams(
            dimension_semantics=("parallel", "arbitrary"), vmem_limit_bytes=VMEM_LIMIT_BYTES),
        name="s5_prep",
    )(pwk, pwt, ct, b2t, d2)


def _s5_weights(a_re, a_im, log_dt, b_re, b_im, c_re, c_im, d_skip):
    nt, npair = S5_BLK, S5_PAIRS
    lam = lax.complex(a_re, a_im)
    dt = jnp.exp(log_dt)[..., None]
    lam_bar = jnp.exp(lam * dt)
    b_bar = ((lam_bar - 1.0) / lam)[..., None] * lax.complex(b_re, b_im)
    taus = jnp.arange(nt + 1, dtype=F32)
    pw = jnp.exp((lam * dt)[..., None] * taus).reshape(2, npair, S5_ST, nt + 1)
    ri = lambda x: jnp.stack([jnp.real(x), jnp.imag(x)], axis=1)
    pwk = jnp.stack([pw[0], pw[1, ..., ::-1]])
    pwk = jnp.pad(ri(pwk), ((0, 0), (0, 0), (0, 0), (0, 0), (0, S5_ST - (nt + 1))))
    pwt = jnp.pad(ri(pw).transpose(0, 1, 2, 4, 3), ((0, 0), (0, 0), (0, 0), (0, S5_PWT_ROWS - (nt + 1)), (0, 0)))
    ct = lax.complex(c_re, c_im).transpose(0, 1, 3, 2).reshape(2, npair, S5_ST, S5_GC)
    ct = jnp.pad(ri(ct), ((0, 0), (0, 0), (0, 0), (0, 0), (0, S5_ST - S5_GC)))
    bb = b_bar.reshape(2, npair, 2, S5_P, S5_GC)
    eye2 = jnp.eye(2, dtype=F32)
    b2t = (bb.transpose(0, 1, 2, 4, 3)[:, :, :, :, None, :] * eye2[None, None, :, None, :, None])
    b2t = ri(b2t.reshape(2, npair, S5_PAIR_CH, S5_ST))
    d2 = jnp.tile(d_skip.reshape(npair, 1, S5_PAIR_CH), (1, 1, nt))
    m2, win2, wout2 = _s5_prep(pwk, pwt, ct, b2t, d2)
    a_blk = pw[..., nt]
    a2 = jnp.stack([jnp.real(a_blk[0]), jnp.imag(a_blk[0]), jnp.real(a_blk[1]), jnp.imag(a_blk[1])], axis=1)
    return m2, win2, wout2, a2


def _mix_body(x_ref, hf_ref, hb_ref, o_ref, y_ref, nw_ref, wglu_ref, wout_ref, out_ref):
    h = hf_ref[...] + hb_ref[...]
    parts = []
    for hd in range(N_HEADS):
        hh = h[:, hd * HEAD_DIM:(hd + 1) * HEAD_DIM]
        mu = jnp.mean(hh, axis=-1, keepdims=True)
        var = jnp.mean(jnp.square(hh - mu), axis=-1, keepdims=True)
        parts.append((hh - mu) * lax.rsqrt(var + EPS))
    hn = jnp.concatenate(parts, axis=1)
    h_m = hn * nw_ref[...] * _sigmoid(o_ref[...])
    y = y_ref[...]
    gelu = 0.5 * y * (1.0 + jnp.tanh(math.sqrt(2.0 / math.pi) * (y + 0.044715 * (y * y * y))))
    ab = jnp.dot(gelu.astype(BF16), wglu_ref[...], preferred_element_type=F32)
    h_s = ab[:, :D_S5] * _sigmoid(ab[:, D_S5:])
    mix = jnp.dot(h_m.astype(BF16), wout_ref[0:D_MLSTM, :], preferred_element_type=F32)
    mix += jnp.dot(h_s.astype(BF16), wout_ref[D_MLSTM:, :], preferred_element_type=F32)
    out_ref[...] = x_ref[...] + mix


def _mix(x, hf, hb, o_in, y, norm_w, w_glu, w_out):
    L = x.shape[0]
    tm = MIX_TM
    assert L % tm == 0
    whole = pl.BlockSpec(memory_space=pltpu.VMEM)
    row = lambda n: pl.BlockSpec((tm, n), lambda i: (i, 0))
    return pl.pallas_call(
        _mix_body,
        grid=(L // tm,),
        in_specs=[row(D_MODEL), row(D_MLSTM), row(D_MLSTM), row(D_MLSTM), row(D_S5), whole, whole, whole],
        out_specs=row(D_MODEL),
        out_shape=jax.ShapeDtypeStruct((L, D_MODEL), F32),
        compiler_params=pltpu.CompilerParams(
            dimension_semantics=("parallel",), vmem_limit_bytes=VMEM_LIMIT_BYTES),
        name="mix",
    )(x, hf, hb, o_in, y, norm_w, w_glu, w_out)


def _encode(x, p):
    x = _ffn(x, p["norm_ffn1"], p["ffn1_w_gate"], p["ffn1_w_up"], p["ffn1_w_down"], p["norm_final"],
             final_norm=False)
    q, kt, v, o_in, u, rb, col = _in_proj(x, p["norm_mix"], p["w_all"], p["w_u"], p["gate_bias_r"],
                                          p["conv_w"], p["conv_b"])
    hf, hb = _mlstm(q, kt, v, rb, col)
    y = _s5(u, p["s5_m"], p["s5_win"], p["s5_wout"], p["s5_a"])
    x = _mix(x, hf, hb, o_in, y, p["mlstm_norm_w"], p["s5_w_glu"], p["w_out"])
    return _ffn(x, p["norm_ffn2"], p["ffn2_w_gate"], p["ffn2_w_up"], p["ffn2_w_down"], p["norm_final"],
                final_norm=True)


def _prepare(norm_ffn1, ffn1_w_gate, ffn1_w_up, ffn1_w_down, norm_mix, w_in, conv_w, conv_b, b_igate, b_fgate,
             mlstm_norm_w, s5_a_re, s5_a_im, s5_log_dt, s5_b_re, s5_b_im, s5_c_re, s5_c_im, s5_d, s5_w_glu,
             w_out, norm_ffn2, ffn2_w_gate, ffn2_w_up, ffn2_w_down, norm_final):
    l = 0
    w = w_in[l]
    g0 = 4 * D_MLSTM
    gate_bias = jnp.concatenate([b_igate[l].reshape(-1), b_fgate[l].reshape(-1)])
    m2, win2, wout2, a2 = _s5_weights(s5_a_re[l], s5_a_im[l], s5_log_dt[l], s5_b_re[l], s5_b_im[l],
                                      s5_c_re[l], s5_c_im[l], s5_d[l])
    row = lambda a: a.reshape(1, -1).astype(F32)
    return {
        "norm_ffn1": row(norm_ffn1[l]), "norm_ffn2": row(norm_ffn2[l]), "norm_final": row(norm_final),
        "ffn1_w_gate": ffn1_w_gate[l].astype(BF16), "ffn1_w_up": ffn1_w_up[l].astype(BF16),
        "ffn1_w_down": ffn1_w_down[l].astype(BF16),
        "ffn2_w_gate": ffn2_w_gate[l].astype(BF16), "ffn2_w_up": ffn2_w_up[l].astype(BF16),
        "ffn2_w_down": ffn2_w_down[l].astype(BF16),
        "norm_mix": row(norm_mix[l]),
        "w_all": w.astype(BF16),
        "w_u": w[:, g0 + N_GATES:].astype(BF16),
        "conv_w": conv_w[l].astype(F32), "conv_b": row(conv_b[l]),
        "gate_bias_r": gate_bias.reshape(N_GATES, 1),
        "mlstm_norm_w": row(mlstm_norm_w[l]),
        "s5_m": m2, "s5_win": win2, "s5_wout": wout2, "s5_a": a2,
        "s5_w_glu": s5_w_glu[l].astype(BF16), "w_out": w_out[l].astype(BF16),
    }


def kernel(x_prompt, x_sample, norm_ffn1, ffn1_w_gate, ffn1_w_up, ffn1_w_down, norm_mix, w_in, conv_w, conv_b, b_igate, b_fgate, mlstm_norm_w, s5_a_re, s5_a_im, s5_log_dt, s5_b_re, s5_b_im, s5_c_re, s5_c_im, s5_d, s5_w_glu, w_out, norm_ffn2, ffn2_w_gate, ffn2_w_up, ffn2_w_down, norm_final):
    assert norm_ffn1.shape[0] == 1 and x_prompt.shape[0] == 1 and x_sample.shape[0] == 1
    p = _prepare(norm_ffn1, ffn1_w_gate, ffn1_w_up, ffn1_w_down, norm_mix, w_in, conv_w, conv_b, b_igate,
                 b_fgate, mlstm_norm_w, s5_a_re, s5_a_im, s5_log_dt, s5_b_re, s5_b_im, s5_c_re, s5_c_im, s5_d,
                 s5_w_glu, w_out, norm_ffn2, ffn2_w_gate, ffn2_w_up, ffn2_w_down, norm_final)
    y_prompt = _encode(x_prompt[0], p)[None]
    y_sample = _encode(x_sample[0], p)[None]
    return (y_prompt, y_sample)
```

```python
import functools
import math

import jax
import jax.numpy as jnp
from jax import lax
from jax.experimental import pallas as pl
from jax.experimental.pallas import tpu as pltpu

F32 = jnp.float32
BF16 = jnp.bfloat16

D_MODEL = 2048
D_MLSTM = 1024
D_S5 = 1024
N_HEADS = 8
HEAD_DIM = 128
CHUNK = 128
MLSTM_SUB = 4
N_GATES = 32
GATE_COL_LANES = 128
S5_GROUPS = 64
S5_GC = 16
S5_P = 64
S5_BLK = 16
S5_PAIRS = S5_GROUPS // 2
D_FF = 5632
EPS = 1e-6
M_INIT = -1e30

VMEM_LIMIT_BYTES = 56 * 1024 * 1024

FFN_TM = 1024
FFN_TF = 512
PROJ_TM = 256
PROJ_HALO = 8
MIX_TM = 512


def _sigmoid(x):
    return 1.0 / (1.0 + jnp.exp(-x))


def _rmsnorm(x, w):
    return x * lax.rsqrt(jnp.mean(x * x, axis=-1, keepdims=True) + EPS) * w


def _ffn_body(x_ref, nw_ref, wg_ref, wu_ref, wd_ref, nf_ref, o_ref, xn_ref, *, final_norm):
    j = pl.program_id(1)

    @pl.when(j == 0)
    def _():
        x = x_ref[...]
        xn_ref[...] = _rmsnorm(x, nw_ref[...]).astype(BF16)
        o_ref[...] = x

    xn = xn_ref[...]
    g = jnp.dot(xn, wg_ref[...], preferred_element_type=F32)
    u = jnp.dot(xn, wu_ref[...], preferred_element_type=F32)
    h = (0.5 * g * _sigmoid(g)) * u
    o_ref[...] += jnp.dot(h.astype(BF16), wd_ref[...], preferred_element_type=F32)

    if final_norm:
        @pl.when(j == pl.num_programs(1) - 1)
        def _():
            o_ref[...] = _rmsnorm(o_ref[...], nf_ref[...])


def _ffn(x, norm_w, w_gate, w_up, w_down, norm_final, *, final_norm):
    L = x.shape[0]
    assert L % FFN_TM == 0 and D_FF % FFN_TF == 0
    return pl.pallas_call(
        functools.partial(_ffn_body, final_norm=final_norm),
        grid=(L // FFN_TM, D_FF // FFN_TF),
        in_specs=[
            pl.BlockSpec((FFN_TM, D_MODEL), lambda i, j: (i, 0)),
            pl.BlockSpec((1, D_MODEL), lambda i, j: (0, 0)),
            pl.BlockSpec((D_MODEL, FFN_TF), lambda i, j: (0, j)),
            pl.BlockSpec((D_MODEL, FFN_TF), lambda i, j: (0, j)),
            pl.BlockSpec((FFN_TF, D_MODEL), lambda i, j: (j, 0)),
            pl.BlockSpec((1, D_MODEL), lambda i, j: (0, 0)),
        ],
        out_specs=pl.BlockSpec((FFN_TM, D_MODEL), lambda i, j: (i, 0)),
        out_shape=jax.ShapeDtypeStruct((L, D_MODEL), F32),
        scratch_shapes=[pltpu.VMEM((FFN_TM, D_MODEL), BF16)],
        compiler_params=pltpu.CompilerParams(
            dimension_semantics=("parallel", "arbitrary"), vmem_limit_bytes=VMEM_LIMIT_BYTES),
        name="ffn_final" if final_norm else "ffn",
    )(x, norm_w, w_gate, w_up, w_down, norm_final)


QK_COLS = 2 * D_MLSTM
QK_CB = 512


def _chunk_scan(x, op, identity, backward):
    n = x.shape[-1]
    pos = lax.broadcasted_iota(jnp.int32, x.shape, x.ndim - 1) % CHUNK
    sh = 1
    while sh < CHUNK:
        if backward:
            x = op(x, jnp.where(pos < CHUNK - sh, pltpu.roll(x, n - sh, x.ndim - 1), identity))
        else:
            x = op(x, jnp.where(pos >= sh, pltpu.roll(x, sh, x.ndim - 1), identity))
        sh *= 2
    return x


def _in_proj_body(xp_ref, x_ref, xnx_ref, nw_ref, w_ref, wu_ref, gb_ref,
                  cw_ref, cb_ref, q_ref, kt_ref, v_ref, o_ref, u_ref, rb_ref, col_ref):
    i = pl.program_id(0)
    tm = x_ref.shape[0]
    nw = nw_ref[...]
    xn = _rmsnorm(x_ref[...], nw)
    xn_prev = jnp.where(i == 0, 0.0, _rmsnorm(xp_ref[...], nw))
    xn_next = jnp.where(i == pl.num_programs(0) - 1, 0.0, _rmsnorm(xnx_ref[...], nw))
    xn_b = xn.astype(BF16)
    xe_b = jnp.concatenate([xn_prev, xn, xn_next], axis=0).astype(BF16)
    rows = tm + 2 * PROJ_HALO

    w_gates = w_ref[:, 2 * QK_COLS:2 * QK_COLS + N_GATES]
    g = lax.dot_general(w_gates, xn_b, (((0,), (1,)), ((), ())), preferred_element_type=F32) + gb_ref[...]
    H = N_HEADS
    f_pre = g[2 * H:]
    lf = jnp.minimum(f_pre, 0.0) - jnp.log1p(jnp.exp(-jnp.abs(f_pre)))
    b_f = _chunk_scan(lf[:H], jnp.add, 0.0, False)
    b_b = _chunk_scan(lf[H:], jnp.add, 0.0, True)
    r_f = g[:H] - b_f
    r_b = g[H:2 * H] - b_b
    cm_f = _chunk_scan(r_f, jnp.maximum, -jnp.inf, False)
    cm_b = _chunk_scan(r_b, jnp.maximum, -jnp.inf, True)

    for c in range(QK_COLS // QK_CB):
        cs = slice(c * QK_CB, (c + 1) * QK_CB)
        z = jnp.dot(xe_b, w_ref[:, cs], preferred_element_type=F32)
        z_m1 = pltpu.roll(z, 1, 0)[PROJ_HALO:PROJ_HALO + tm]
        z_0 = z[PROJ_HALO:PROJ_HALO + tm]
        z_p1 = pltpu.roll(z, rows - 1, 0)[PROJ_HALO:PROJ_HALO + tm]
        y = z_m1 * cw_ref[0:1, cs] + z_0 * cw_ref[1:2, cs] + z_p1 * cw_ref[2:3, cs] + cb_ref[:, cs]
        y = y * _sigmoid(y)
        if c * QK_CB < D_MLSTM:
            q_ref[:, cs] = (y * (HEAD_DIM ** -0.5)).astype(q_ref.dtype)
        else:
            kt_ref[0, c * QK_CB - D_MLSTM:(c + 1) * QK_CB - D_MLSTM, :] = y.T.astype(kt_ref.dtype)

    v_ref[...] = jnp.dot(xn_b, w_ref[:, QK_COLS:QK_COLS + D_MLSTM], preferred_element_type=F32).astype(v_ref.dtype)
    o_ref[...] = jnp.dot(xn_b, w_ref[:, QK_COLS + D_MLSTM:QK_COLS + 2 * D_MLSTM], preferred_element_type=F32)
    u_ref[...] = jnp.dot(xn_b, wu_ref[...], preferred_element_type=F32)
    rb_ref[...] = jnp.concatenate([r_f, r_b, b_f, b_b], axis=0)
    pad = jnp.zeros((GATE_COL_LANES - 4 * H, tm), F32)
    col_ref[...] = jnp.concatenate([b_f, b_b, cm_f, cm_b, pad], axis=0).T


def _in_proj(x, norm_w, w_all, w_u, gate_bias, conv_w, conv_b):
    L = x.shape[0]
    tm = PROJ_TM
    assert L % tm == 0 and tm % CHUNK == 0
    hb = tm // PROJ_HALO
    nblk8 = L // PROJ_HALO
    whole = pl.BlockSpec(memory_space=pltpu.VMEM)
    return pl.pallas_call(
        _in_proj_body,
        grid=(L // tm,),
        in_specs=[
            pl.BlockSpec((PROJ_HALO, D_MODEL), lambda i: (jnp.maximum(i * hb - 1, 0), 0)),
            pl.BlockSpec((tm, D_MODEL), lambda i: (i, 0)),
            pl.BlockSpec((PROJ_HALO, D_MODEL), lambda i: (jnp.minimum((i + 1) * hb, nblk8 - 1), 0)),
            whole, whole, whole, whole, whole, whole,
        ],
        out_specs=[
            pl.BlockSpec((tm, D_MLSTM), lambda i: (i, 0)),
            pl.BlockSpec((1, D_MLSTM, tm), lambda i: (i, 0, 0)),
            pl.BlockSpec((tm, D_MLSTM), lambda i: (i, 0)),
            pl.BlockSpec((tm, D_MLSTM), lambda i: (i, 0)),
            pl.BlockSpec((tm, D_S5), lambda i: (i, 0)),
            pl.BlockSpec((N_GATES, tm), lambda i: (0, i)),
            pl.BlockSpec((tm, GATE_COL_LANES), lambda i: (i, 0)),
        ],
        out_shape=[
            jax.ShapeDtypeStruct((L, D_MLSTM), BF16),
            jax.ShapeDtypeStruct((L // tm, D_MLSTM, tm), BF16),
            jax.ShapeDtypeStruct((L, D_MLSTM), BF16),
            jax.ShapeDtypeStruct((L, D_MLSTM), F32),
            jax.ShapeDtypeStruct((L, D_S5), F32),
            jax.ShapeDtypeStruct((N_GATES, L), F32),
            jax.ShapeDtypeStruct((L, GATE_COL_LANES), F32),
        ],
        compiler_params=pltpu.CompilerParams(
            dimension_semantics=("parallel",), vmem_limit_bytes=VMEM_LIMIT_BYTES),
        name="in_proj",
    )(x, x, x, norm_w, w_all, w_u, gate_bias, conv_w, conv_b)


def _bf16_split3(x):
    hi = x.astype(BF16)
    r1 = x - hi.astype(F32)
    mid = r1.astype(BF16)
    lo = (r1 - mid.astype(F32)).astype(BF16)
    return hi, mid, lo


def _mlstm_direction(q_ref, kt_ref, v_ref, rb_ref, col_ref, h_ref, ct_ref, m_ref, ml_ref, d, sub):
    T = CHUNK
    H = N_HEADS
    backward = d == 1
    tt = slice(sub * T, (sub + 1) * T)
    rr = lax.broadcasted_iota(jnp.int32, (T, T), 0)
    cc = lax.broadcasted_iota(jnp.int32, (T, T), 1)
    mask = (cc >= rr) if backward else (cc <= rr)
    last = 0 if backward else T - 1

    r_rows = rb_ref[d * H:(d + 1) * H, tt]
    b_rows = rb_ref[(2 + d) * H:(3 + d) * H, tt]
    b_cols = col_ref[tt, d * H:(d + 1) * H]
    cm_cols = col_ref[tt, (2 + d) * H:(3 + d) * H]
    m_prev = m_ref[d]
    m_prev_l = ml_ref[d]
    m_cols = jnp.maximum(cm_cols, m_prev_l)
    clamp_cols = -(b_cols + m_cols)
    m_last = jnp.maximum(jnp.max(r_rows, axis=1, keepdims=True), m_prev)
    b_tot = jnp.broadcast_to(b_rows[:, last:last + 1], (H, T))
    sc_rows = jnp.exp(m_prev - m_last)
    wkk_rows = jnp.exp(r_rows - m_last)
    ones = jnp.ones((T, HEAD_DIM), BF16)

    for h in range(H):
        ci = d * H + h
        hs = slice(h * HEAD_DIM, (h + 1) * HEAD_DIM)
        q = q_ref[tt, hs]
        kt = kt_ref[sub * T // PROJ_TM, hs, sub * T % PROJ_TM:sub * T % PROJ_TM + T]
        vaug = jnp.concatenate([v_ref[tt, hs], ones], axis=1)
        ct_prev = ct_ref[ci]
        m_col = jnp.broadcast_to(m_cols[:, h:h + 1], (T, T))
        w = jnp.exp(jnp.where(mask, r_rows[h:h + 1, :] - m_col, -jnp.inf))
        sq = jnp.dot(q, jnp.concatenate([kt, ct_prev.astype(BF16)], axis=1), preferred_element_type=F32)
        s = sq[:, :T] * w
        qc = sq[:, T:]
        kw = (kt.astype(F32) * wkk_rows[h:h + 1, :]).astype(BF16)
        both = jnp.dot(jnp.concatenate([s.astype(BF16), kw], axis=0), vaug, preferred_element_type=F32)
        sv = both[:T]
        upd = both[T:]
        s_inter = jnp.exp(m_prev[h:h + 1, :] - m_col)
        num = sv[:, :HEAD_DIM] + s_inter * qc[:, :HEAD_DIM]
        den = sv[:, HEAD_DIM:] + s_inter * qc[:, HEAD_DIM:]
        floor = jnp.exp(jnp.broadcast_to(clamp_cols[:, h:h + 1], (T, HEAD_DIM)))
        h_ref[tt, hs] = num / jnp.maximum(jnp.abs(den), floor)

        sc = sc_rows[h:h + 1, :]
        ct_ref[ci] = jnp.concatenate([sc, sc], axis=1) * ct_prev + upd

    m_ref[d] = b_tot + m_last
    ml_ref[d] = b_cols[last:last + 1, :] + m_cols[last:last + 1, :]


def _mlstm_body(qf_ref, ktf_ref, vf_ref, rbf_ref, colf_ref, qb_ref, ktb_ref, vb_ref, rbb_ref, colb_ref,
                hf_ref, hb_ref, ct_ref, m_ref, ml_ref):
    @pl.when(pl.program_id(0) == 0)
    def _():
        ct_ref[...] = jnp.zeros_like(ct_ref)
        m_ref[...] = jnp.full_like(m_ref, M_INIT)
        ml_ref[...] = jnp.full_like(ml_ref, M_INIT)

    for sub in range(MLSTM_SUB):
        _mlstm_direction(qf_ref, ktf_ref, vf_ref, rbf_ref, colf_ref, hf_ref, ct_ref, m_ref, ml_ref, 0, sub)
        _mlstm_direction(qb_ref, ktb_ref, vb_ref, rbb_ref, colb_ref, hb_ref, ct_ref, m_ref, ml_ref, 1,
                         MLSTM_SUB - 1 - sub)


def _mlstm(q, kt, v, rb, col):
    L = q.shape[0]
    T = MLSTM_SUB * CHUNK
    assert L % T == 0 and CHUNK == HEAD_DIM and T % PROJ_TM == 0 and PROJ_TM % CHUNK == 0
    nc = L // T
    kt_blk = (T // PROJ_TM, D_MLSTM, PROJ_TM)
    fwd = lambda c: (c, 0)
    bwd = lambda c: (nc - 1 - c, 0)
    fwd_r = lambda c: (0, c)
    bwd_r = lambda c: (0, nc - 1 - c)
    return pl.pallas_call(
        _mlstm_body,
        grid=(nc,),
        in_specs=[
            pl.BlockSpec((T, D_MLSTM), fwd), pl.BlockSpec(kt_blk, lambda c: (c, 0, 0)), pl.BlockSpec((T, D_MLSTM), fwd),
            pl.BlockSpec((N_GATES, T), fwd_r), pl.BlockSpec((T, GATE_COL_LANES), fwd),
            pl.BlockSpec((T, D_MLSTM), bwd), pl.BlockSpec(kt_blk, lambda c: (nc - 1 - c, 0, 0)),
            pl.BlockSpec((T, D_MLSTM), bwd),
            pl.BlockSpec((N_GATES, T), bwd_r), pl.BlockSpec((T, GATE_COL_LANES), bwd),
        ],
        out_specs=[pl.BlockSpec((T, D_MLSTM), fwd), pl.BlockSpec((T, D_MLSTM), bwd)],
        out_shape=[jax.ShapeDtypeStruct((L, D_MLSTM), F32), jax.ShapeDtypeStruct((L, D_MLSTM), F32)],
        scratch_shapes=[
            pltpu.VMEM((2 * N_HEADS, HEAD_DIM, 2 * HEAD_DIM), F32),
            pltpu.VMEM((2, N_HEADS, CHUNK), F32),
            pltpu.VMEM((2, 1, N_HEADS), F32),
        ],
        compiler_params=pltpu.CompilerParams(
            dimension_semantics=("arbitrary",), vmem_limit_bytes=VMEM_LIMIT_BYTES),
        name="mlstm",
    )(q, kt, v, rb, col, q, kt, v, rb, col)


S5_PAIR_CH = 2 * S5_GC
S5_ROW = S5_BLK * S5_PAIR_CH
S5_ST = 2 * S5_P
S5_TILE = 8
S5_LANE_PAIRS = 4
S5_TT = 4096
S5_NBT = S5_TT // S5_BLK


def _s5_scan(a, sin_ref, nb):
    R = S5_TILE
    ntile = nb // R
    row = lax.broadcasted_iota(jnp.int32, (R, S5_ST), 0)
    zero = jnp.zeros((1, S5_ST), F32)

    def cmul(x, y):
        return x[0] * y[0] - x[1] * y[1], x[0] * y[1] + x[1] * y[0]

    def bcast(x):
        return tuple(jnp.broadcast_to(t, (R, S5_ST)) for t in x)

    def tables(ar, ai, backward):
        pw = {1: (ar, ai)}
        for e in range(2, R + 1):
            pw[e] = cmul(pw[e // 2], pw[e - e // 2])
        steps = []
        for sh in (1, 2, 4):
            keep = (row < R - sh) if backward else (row >= sh)
            steps.append(tuple(jnp.where(keep, t, 0.0) for t in bcast(pw[sh])))
        order = range(R, 0, -1) if backward else range(1, R + 1)
        carry_pw = tuple(jnp.concatenate([pw[e][j] for e in order], axis=0) for j in (0, 1))
        return steps, carry_pw

    def scan_tile(x, carry, steps, cpw, backward):
        for sh, am in zip((1, 2, 4), steps):
            rs = (R - sh) if backward else sh
            x = tuple(p + q for p, q in zip(x, cmul(am, (pltpu.roll(x[0], rs, 0), pltpu.roll(x[1], rs, 0)))))
        cb = bcast(carry)
        x = tuple(p + q for p, q in zip(x, cmul(cpw, cb)))
        edge, rs = (R - 1, R - 1) if backward else (0, 1)
        enter = tuple(jnp.where(row == edge, c, pltpu.roll(t, rs, 0)) for t, c in zip(x, cb))
        last = 0 if backward else R - 1
        return enter, (x[0][last:last + 1], x[1][last:last + 1])

    tabs = []
    for p in range(S5_LANE_PAIRS):
        ap = a[p]
        tabs.append((tables(ap[0:1], ap[1:2], False), tables(ap[2:3], ap[3:4], True)))

    def step(i, carry):
        rf = pl.multiple_of(i * R, R)
        rb = pl.multiple_of((ntile - 1 - i) * R, R)
        out = []
        for p in range(S5_LANE_PAIRS):
            (steps_f, cpw_f), (steps_b, cpw_b) = tabs[p]
            cf, cb = carry[p]
            xf = (sin_ref[p, pl.ds(rf, R), 0:S5_ST], sin_ref[p, pl.ds(rf, R), S5_ST:2 * S5_ST])
            xb = (sin_ref[p, pl.ds(rb, R), 2 * S5_ST:3 * S5_ST], sin_ref[p, pl.ds(rb, R), 3 * S5_ST:4 * S5_ST])
            ef, cf = scan_tile(xf, cf, steps_f, cpw_f, False)
            eb, cb = scan_tile(xb, cb, steps_b, cpw_b, True)
            sin_ref[p, pl.ds(rf, R), 0:S5_ST] = ef[0]
            sin_ref[p, pl.ds(rf, R), S5_ST:2 * S5_ST] = ef[1]
            sin_ref[p, pl.ds(rb, R), 2 * S5_ST:3 * S5_ST] = eb[0]
            sin_ref[p, pl.ds(rb, R), 3 * S5_ST:4 * S5_ST] = eb[1]
            out.append((cf, cb))
        return tuple(out)

    init = tuple(((zero, zero), (zero, zero)) for _ in range(S5_LANE_PAIRS))
    lax.fori_loop(0, ntile, step, init)


def _s5_body(u_ref, m_ref, win_ref, wout_ref, a_ref, y_ref, u2_ref, sin_ref, y2_ref):
    phase = pl.program_id(1)
    t = pl.program_id(2)
    nb = u2_ref.shape[1]
    r0 = pl.multiple_of(t * S5_NBT, S5_NBT)
    rows = pl.ds(r0, S5_NBT)

    n_slot = S5_LANE_PAIRS
    lt = S5_LANE_PAIRS * S5_PAIR_CH
    slot = lax.broadcasted_iota(jnp.int32, (S5_NBT, lt), 1) // S5_PAIR_CH

    def pick(parts, first):
        out = parts[n_slot - 1]
        for i in range(n_slot - 2, -1, -1):
            out = jnp.where(slot == (first + i) % n_slot, parts[i], out)
        return out

    @pl.when(phase == 0)
    def _():
        for q in range(S5_BLK // n_slot):
            rot = []
            for i in range(n_slot):
                tok = u_ref[pl.ds(n_slot * q + i, S5_NBT, stride=S5_BLK), :]
                rot.append(pltpu.roll(tok, i * S5_PAIR_CH, 1) if i else tok)
            for p in range(S5_LANE_PAIRS):
                u2_ref[p, rows, q * lt:(q + 1) * lt] = pick(rot, p).astype(BF16)
        for p in range(S5_LANE_PAIRS):
            sin_ref[p, rows, :] = jnp.dot(u2_ref[p, rows, :], win_ref[p], preferred_element_type=F32)

    @pl.when((phase == 0) & (t == pl.num_programs(2) - 1))
    def _():
        _s5_scan(a_ref[...], sin_ref, nb)

    @pl.when(phase == 1)
    def _():
        for p in range(S5_LANE_PAIRS):
            y2_ref[p] = (jnp.dot(u2_ref[p, rows, :], m_ref[p], preferred_element_type=F32)
                         + jnp.dot(sin_ref[p, rows, :].astype(BF16), wout_ref[p], preferred_element_type=F32))
        for tkn in range(S5_BLK):
            q, i = divmod(tkn, n_slot)
            merged = pick([y2_ref[p, :, q * lt:(q + 1) * lt] for p in range(S5_LANE_PAIRS)], i)
            out = pltpu.roll(merged, ((n_slot - i) % n_slot) * S5_PAIR_CH, 1) if i else merged
            y_ref[pl.ds(tkn, S5_NBT, stride=S5_BLK), :] = out


def _s5(u, m2, win2, wout2, a2):
    L = u.shape[0]
    assert L % S5_TT == 0 and D_S5 == S5_PAIRS * S5_PAIR_CH
    nb = L // S5_BLK
    nq = S5_PAIRS // S5_LANE_PAIRS
    lane_tile = S5_LANE_PAIRS * S5_PAIR_CH
    wspec = pl.BlockSpec((S5_LANE_PAIRS, S5_ROW, S5_ROW), lambda q, ph, t: (q, 0, 0))
    return pl.pallas_call(
        _s5_body,
        grid=(nq, 2, L // S5_TT),
        in_specs=[
            pl.BlockSpec((S5_TT, lane_tile), lambda q, ph, t: (t, q)),
            wspec, wspec, wspec,
            pl.BlockSpec((S5_LANE_PAIRS, 4, S5_ST), lambda q, ph, t: (q, 0, 0)),
        ],
        out_specs=pl.BlockSpec((S5_TT, lane_tile), lambda q, ph, t: (t * ph, q)),
        out_shape=jax.ShapeDtypeStruct((L, D_S5), F32),
        scratch_shapes=[
            pltpu.VMEM((S5_LANE_PAIRS, nb, S5_ROW), BF16),
            pltpu.VMEM((S5_LANE_PAIRS, nb, 4 * S5_ST), F32),
            pltpu.VMEM((S5_LANE_PAIRS, S5_NBT, S5_ROW), F32),
        ],
        compiler_params=pltpu.CompilerParams(
            dimension_semantics=("parallel", "arbitrary", "arbitrary"), vmem_limit_bytes=VMEM_LIMIT_BYTES),
        name="s5",
    )(u, m2, win2, wout2, a2)


S5_EXP = (S5_BLK + 1) * S5_PAIR_CH
S5_EXP_PAD = 640
S5_PWT_ROWS = 24


def _s5_prep_body(pwk_ref, pwt_ref, ct_ref, b2t_ref, d_ref, m_ref, win_ref, wout_ref):
    nt, w = S5_BLK, S5_PAIR_CH
    lane = lax.broadcasted_iota(jnp.int32, (S5_ST, S5_EXP_PAD), 1)
    row = lax.broadcasted_iota(jnp.int32, (S5_ST, S5_EXP_PAD), 0)
    sel_slot = (lane // w == row).astype(BF16)
    sel_chan = ((lane % S5_GC == row) & (row < S5_GC)).astype(BF16)
    same_group = (row // S5_P == (lane // S5_GC) % 2) & (lane < S5_EXP)

    def expand(x, sel):
        return sum(jnp.dot(p, sel, preferred_element_type=F32) for p in _bf16_split3(x))

    def split_dot(a, x):
        a_hi, a_lo, _ = _bf16_split3(a)
        x_hi, x_lo, _ = _bf16_split3(x)
        return (jnp.dot(a_hi, x_hi, preferred_element_type=F32) + jnp.dot(a_hi, x_lo, preferred_element_type=F32)
                + jnp.dot(a_lo, x_hi, preferred_element_type=F32))

    lane_m = lax.broadcasted_iota(jnp.int32, (w, S5_ROW), 1)
    row_m = lax.broadcasted_iota(jnp.int32, (w, S5_ROW), 0)
    krow, wout_rows, win_rows = [], [], []
    for z in range(2):
        pr, pi = expand(pwk_ref[z, 0, 0], sel_slot), expand(pwk_ref[z, 1, 0], sel_slot)
        cr, ci = expand(ct_ref[z, 0, 0], sel_chan), expand(ct_ref[z, 1, 0], sel_chan)
        xr = jnp.where(same_group, pr * cr - pi * ci, 0.0)
        xi = jnp.where(same_group, pr * ci + pi * cr, 0.0)
        lo = w if z == 0 else 0
        wout_rows += [xr[:, lo:lo + S5_ROW], -xi[:, lo:lo + S5_ROW]]
        k_all = split_dot(b2t_ref[z, 0, 0], xr) - split_dot(b2t_ref[z, 1, 0], xi)
        lo = 0 if z == 0 else w
        krow.append(k_all[:, lo:lo + S5_ROW])
        br, bi = b2t_ref[z, 0, 0], b2t_ref[z, 1, 0]
        blocks = []
        for s in range(nt):
            e = nt - 1 - s if z == 0 else s
            qr, qi = pwt_ref[z, 0, 0, e:e + 1, :], pwt_ref[z, 1, 0, e:e + 1, :]
            blocks.append((br * qr - bi * qi, br * qi + bi * qr))
        win_rows.append(blocks)

    kf, kb = krow
    d_diag = jnp.where((lane_m % w) == row_m, d_ref[0], 0.0)
    m_rows = []
    for s in range(nt):
        f = jnp.where(lane_m >= w * s, pltpu.roll(kf, w * s, 1), 0.0) if s else kf
        sh = w * (nt - 1 - s)
        b = jnp.where(lane_m < S5_ROW - sh, pltpu.roll(kb, S5_ROW - sh, 1), 0.0) if sh else kb
        dd = jnp.where(lane_m // w == s, d_diag, 0.0)
        m_rows.append(f + b + dd)

    n_slot = S5_LANE_PAIRS
    lt = n_slot * w

    pp = pl.program_id(1)

    def order_cols(x):
        return jnp.concatenate([pltpu.roll(x[:, q * lt:(q + 1) * lt], pp * w, 1) for q in range(S5_ROW // lt)], axis=1)

    for k, x in enumerate(wout_rows):
        wout_ref[0, k * S5_ST:(k + 1) * S5_ST, :] = order_cols(x).astype(BF16)
    for s in range(nt):
        r0 = pl.multiple_of((n_slot * (s // n_slot) + (s % n_slot + pp) % n_slot) * w, w)
        m_ref[0, pl.ds(r0, w), :] = order_cols(m_rows[s]).astype(BF16)
        for z in range(2):
            wr, wi = win_rows[z][s]
            win_ref[0, pl.ds(r0, w), (2 * z) * S5_ST:(2 * z + 1) * S5_ST] = wr.astype(BF16)
            win_ref[0, pl.ds(r0, w), (2 * z + 1) * S5_ST:(2 * z + 2) * S5_ST] = wi.astype(BF16)


def _s5_prep(pwk, pwt, ct, b2t, d2):
    npair = d2.shape[0]
    n = S5_LANE_PAIRS
    spec = lambda r, c: pl.BlockSpec((2, 2, 1, r, c), lambda g, p: (0, 0, g * n + p, 0, 0))
    out = pl.BlockSpec((1, S5_ROW, S5_ROW), lambda g, p: (g * n + p, 0, 0))
    shape = jax.ShapeDtypeStruct((npair, S5_ROW, S5_ROW), BF16)
    return pl.pallas_call(
        _s5_prep_body,
        grid=(npair // n, n),
        in_specs=[spec(S5_ST, S5_ST), spec(S5_PWT_ROWS, S5_ST), spec(S5_ST, S5_ST), spec(S5_PAIR_CH, S5_ST),
                  pl.BlockSpec((1, 1, S5_ROW), lambda g, p: (g * n + p, 0, 0))],
        out_specs=[out, out, out],
        out_shape=[shape, shape, shape],
        compiler_params=pltpu.CompilerParams(
            dimension_semantics=("parallel", "arbitrary"), vmem_limit_bytes=VMEM_LIMIT_BYTES),
        name="s5_prep",
    )(pwk, pwt, ct, b2t, d2)


def _s5_weights(a_re, a_im, log_dt, b_re, b_im, c_re, c_im, d_skip):
    nt, npair = S5_BLK, S5_PAIRS
    lam = lax.complex(a_re, a_im)
    dt = jnp.exp(log_dt)[..., None]
    lam_bar = jnp.exp(lam * dt)
    b_bar = ((lam_bar - 1.0) / lam)[..., None] * lax.complex(b_re, b_im)
    taus = jnp.arange(nt + 1, dtype=F32)
    pw = jnp.exp((lam * dt)[..., None] * taus).reshape(2, npair, S5_ST, nt + 1)
    ri = lambda x: jnp.stack([jnp.real(x), jnp.imag(x)], axis=1)
    pwk = jnp.stack([pw[0], pw[1, ..., ::-1]])
    pwk = jnp.pad(ri(pwk), ((0, 0), (0, 0), (0, 0), (0, 0), (0, S5_ST - (nt + 1))))
    pwt = jnp.pad(ri(pw).transpose(0, 1, 2, 4, 3), ((0, 0), (0, 0), (0, 0), (0, S5_PWT_ROWS - (nt + 1)), (0, 0)))
    ct = lax.complex(c_re, c_im).transpose(0, 1, 3, 2).reshape(2, npair, S5_ST, S5_GC)
    ct = jnp.pad(ri(ct), ((0, 0), (0, 0), (0, 0), (0, 0), (0, S5_ST - S5_GC)))
    bb = b_bar.reshape(2, npair, 2, S5_P, S5_GC)
    eye2 = jnp.eye(2, dtype=F32)
    b2t = (bb.transpose(0, 1, 2, 4, 3)[:, :, :, :, None, :] * eye2[None, None, :, None, :, None])
    b2t = ri(b2t.reshape(2, npair, S5_PAIR_CH, S5_ST))
    d2 = jnp.tile(d_skip.reshape(npair, 1, S5_PAIR_CH), (1, 1, nt))
    m2, win2, wout2 = _s5_prep(pwk, pwt, ct, b2t, d2)
    a_blk = pw[..., nt]
    a2 = jnp.stack([jnp.real(a_blk[0]), jnp.imag(a_blk[0]), jnp.real(a_blk[1]), jnp.imag(a_blk[1])], axis=1)
    return m2, win2, wout2, a2


def _mix_body(x_ref, hf_ref, hb_ref, o_ref, y_ref, nw_ref, wglu_ref, wout_ref, out_ref):
    h = hf_ref[...] + hb_ref[...]
    parts = []
    for hd in range(N_HEADS):
        hh = h[:, hd * HEAD_DIM:(hd + 1) * HEAD_DIM]
        mu = jnp.mean(hh, axis=-1, keepdims=True)
        var = jnp.mean(jnp.square(hh - mu), axis=-1, keepdims=True)
        parts.append((hh - mu) * lax.rsqrt(var + EPS))
    hn = jnp.concatenate(parts, axis=1)
    h_m = hn * nw_ref[...] * _sigmoid(o_ref[...])
    y = y_ref[...]
    gelu = 0.5 * y * (1.0 + jnp.tanh(math.sqrt(2.0 / math.pi) * (y + 0.044715 * (y * y * y))))
    ab = jnp.dot(gelu.astype(BF16), wglu_ref[...], preferred_element_type=F32)
    h_s = ab[:, :D_S5] * _sigmoid(ab[:, D_S5:])
    mix = jnp.dot(h_m.astype(BF16), wout_ref[0:D_MLSTM, :], preferred_element_type=F32)
    mix += jnp.dot(h_s.astype(BF16), wout_ref[D_MLSTM:, :], preferred_element_type=F32)
    out_ref[...] = x_ref[...] + mix


def _mix(x, hf, hb, o_in, y, norm_w, w_glu, w_out):
    L = x.shape[0]
    tm = MIX_TM
    assert L % tm == 0
    whole = pl.BlockSpec(memory_space=pltpu.VMEM)
    row = lambda n: pl.BlockSpec((tm, n), lambda i: (i, 0))
    return pl.pallas_call(
        _mix_body,
        grid=(L // tm,),
        in_specs=[row(D_MODEL), row(D_MLSTM), row(D_MLSTM), row(D_MLSTM), row(D_S5), whole, whole, whole],
        out_specs=row(D_MODEL),
        out_shape=jax.ShapeDtypeStruct((L, D_MODEL), F32),
        compiler_params=pltpu.CompilerParams(
            dimension_semantics=("parallel",), vmem_limit_bytes=VMEM_LIMIT_BYTES),
        name="mix",
    )(x, hf, hb, o_in, y, norm_w, w_glu, w_out)


def _encode(x, p):
    x = _ffn(x, p["norm_ffn1"], p["ffn1_w_gate"], p["ffn1_w_up"], p["ffn1_w_down"], p["norm_final"],
             final_norm=False)
    q, kt, v, o_in, u, rb, col = _in_proj(x, p["norm_mix"], p["w_all"], p["w_u"], p["gate_bias_r"],
                                          p["conv_w"], p["conv_b"])
    hf, hb = _mlstm(q, kt, v, rb, col)
    y = _s5(u, p["s5_m"], p["s5_win"], p["s5_wout"], p["s5_a"])
    x = _mix(x, hf, hb, o_in, y, p["mlstm_norm_w"], p["s5_w_glu"], p["w_out"])
    return _ffn(x, p["norm_ffn2"], p["ffn2_w_gate"], p["ffn2_w_up"], p["ffn2_w_down"], p["norm_final"],
                final_norm=True)


def _prepare(norm_ffn1, ffn1_w_gate, ffn1_w_up, ffn1_w_down, norm_mix, w_in, conv_w, conv_b, b_igate, b_fgate,
             mlstm_norm_w, s5_a_re, s5_a_im, s5_log_dt, s5_b_re, s5_b_im, s5_c_re, s5_c_im, s5_d, s5_w_glu,
             w_out, norm_ffn2, ffn2_w_gate, ffn2_w_up, ffn2_w_down, norm_final):
    l = 0
    w = w_in[l]
    g0 = 4 * D_MLSTM
    gate_bias = jnp.concatenate([b_igate[l].reshape(-1), b_fgate[l].reshape(-1)])
    m2, win2, wout2, a2 = _s5_weights(s5_a_re[l], s5_a_im[l], s5_log_dt[l], s5_b_re[l], s5_b_im[l],
                                      s5_c_re[l], s5_c_im[l], s5_d[l])
    row = lambda a: a.reshape(1, -1).astype(F32)
    return {
        "norm_ffn1": row(norm_ffn1[l]), "norm_ffn2": row(norm_ffn2[l]), "norm_final": row(norm_final),
        "ffn1_w_gate": ffn1_w_gate[l].astype(BF16), "ffn1_w_up": ffn1_w_up[l].astype(BF16),
        "ffn1_w_down": ffn1_w_down[l].astype(BF16),
        "ffn2_w_gate": ffn2_w_gate[l].astype(BF16), "ffn2_w_up": ffn2_w_up[l].astype(BF16),
        "ffn2_w_down": ffn2_w_down[l].astype(BF16),
        "norm_mix": row(norm_mix[l]),
        "w_all": w.astype(BF16),
        "w_u": w[:, g0 + N_GATES:].astype(BF16),
        "conv_w": conv_w[l].astype(F32), "conv_b": row(conv_b[l]),
        "gate_bias_r": gate_bias.reshape(N_GATES, 1),
        "mlstm_norm_w": row(mlstm_norm_w[l]),
        "s5_m": m2, "s5_win": win2, "s5_wout": wout2, "s5_a": a2,
        "s5_w_glu": s5_w_glu[l].astype(BF16), "w_out": w_out[l].astype(BF16),
    }


def kernel(x_prompt, x_sample, norm_ffn1, ffn1_w_gate, ffn1_w_up, ffn1_w_down, norm_mix, w_in, conv_w, conv_b, b_igate, b_fgate, mlstm_norm_w, s5_a_re, s5_a_im, s5_log_dt, s5_b_re, s5_b_im, s5_c_re, s5_c_im, s5_d, s5_w_glu, w_out, norm_ffn2, ffn2_w_gate, ffn2_w_up, ffn2_w_down, norm_final):
    assert norm_ffn1.shape[0] == 1 and x_prompt.shape[0] == 1 and x_sample.shape[0] == 1
    p = _prepare(norm_ffn1, ffn1_w_gate, ffn1_w_up, ffn1_w_down, norm_mix, w_in, conv_w, conv_b, b_igate,
                 b_fgate, mlstm_norm_w, s5_a_re, s5_a_im, s5_log_dt, s5_b_re, s5_b_im, s5_c_re, s5_c_im, s5_d,
                 s5_w_glu, w_out, norm_ffn2, ffn2_w_gate, ffn2_w_up, ffn2_w_down, norm_final)
    y_prompt = _encode(x_prompt[0], p)[None]
    y_sample = _encode(x_sample[0], p)[None]
    return (y_prompt, y_sample)
```

```python
import functools
import math

import jax
import jax.numpy as jnp
from jax import lax
from jax.experimental import pallas as pl
from jax.experimental.pallas import tpu as pltpu

F32 = jnp.float32
BF16 = jnp.bfloat16

D_MODEL = 2048
D_MLSTM = 1024
D_S5 = 1024
N_HEADS = 8
HEAD_DIM = 128
CHUNK = 128
MLSTM_SUB = 4
N_GATES = 32
GATE_COL_LANES = 128
S5_GROUPS = 64
S5_GC = 16
S5_P = 64
S5_BLK = 16
S5_PAIRS = S5_GROUPS // 2
D_FF = 5632
EPS = 1e-6
M_INIT = -1e30

VMEM_LIMIT_BYTES = 56 * 1024 * 1024

FFN_TM = 1024
FFN_TF = 512
PROJ_TM = 256
PROJ_HALO = 8
MIX_TM = 512


def _sigmoid(x):
    return 1.0 / (1.0 + jnp.exp(-x))


def _rmsnorm(x, w):
    return x * lax.rsqrt(jnp.mean(x * x, axis=-1, keepdims=True) + EPS) * w


def _ffn_body(x_ref, nw_ref, wg_ref, wu_ref, wd_ref, nf_ref, o_ref, xn_ref, *, final_norm):
    j = pl.program_id(1)

    @pl.when(j == 0)
    def _():
        x = x_ref[...]
        xn_ref[...] = _rmsnorm(x, nw_ref[...]).astype(BF16)
        o_ref[...] = x

    xn = xn_ref[...]
    g = jnp.dot(xn, wg_ref[...], preferred_element_type=F32)
    u = jnp.dot(xn, wu_ref[...], preferred_element_type=F32)
    h = (0.5 * g * _sigmoid(g)) * u
    o_ref[...] += jnp.dot(h.astype(BF16), wd_ref[...], preferred_element_type=F32)

    if final_norm:
        @pl.when(j == pl.num_programs(1) - 1)
        def _():
            o_ref[...] = _rmsnorm(o_ref[...], nf_ref[...])


def _ffn(x, norm_w, w_gate, w_up, w_down, norm_final, *, final_norm):
    L = x.shape[0]
    assert L % FFN_TM == 0 and D_FF % FFN_TF == 0
    return pl.pallas_call(
        functools.partial(_ffn_body, final_norm=final_norm),
        grid=(L // FFN_TM, D_FF // FFN_TF),
        in_specs=[
            pl.BlockSpec((FFN_TM, D_MODEL), lambda i, j: (i, 0)),
            pl.BlockSpec((1, D_MODEL), lambda i, j: (0, 0)),
            pl.BlockSpec((D_MODEL, FFN_TF), lambda i, j: (0, j)),
            pl.BlockSpec((D_MODEL, FFN_TF), lambda i, j: (0, j)),
            pl.BlockSpec((FFN_TF, D_MODEL), lambda i, j: (j, 0)),
            pl.BlockSpec((1, D_MODEL), lambda i, j: (0, 0)),
        ],
        out_specs=pl.BlockSpec((FFN_TM, D_MODEL), lambda i, j: (i, 0)),
        out_shape=jax.ShapeDtypeStruct((L, D_MODEL), F32),
        scratch_shapes=[pltpu.VMEM((FFN_TM, D_MODEL), BF16)],
        compiler_params=pltpu.CompilerParams(
            dimension_semantics=("parallel", "arbitrary"), vmem_limit_bytes=VMEM_LIMIT_BYTES),
        name="ffn_final" if final_norm else "ffn",
    )(x, norm_w, w_gate, w_up, w_down, norm_final)


QK_COLS = 2 * D_MLSTM
QK_CB = 512


def _chunk_scan(x, op, identity, backward):
    n = x.shape[-1]
    pos = lax.broadcasted_iota(jnp.int32, x.shape, x.ndim - 1) % CHUNK
    sh = 1
    while sh < CHUNK:
        if backward:
            x = op(x, jnp.where(pos < CHUNK - sh, pltpu.roll(x, n - sh, x.ndim - 1), identity))
        else:
            x = op(x, jnp.where(pos >= sh, pltpu.roll(x, sh, x.ndim - 1), identity))
        sh *= 2
    return x


def _in_proj_body(xp_ref, x_ref, xnx_ref, nw_ref, w_ref, wu_ref, gb_ref,
                  cw_ref, cb_ref, q_ref, kt_ref, v_ref, o_ref, u_ref, rb_ref, col_ref):
    i = pl.program_id(0)
    tm = x_ref.shape[0]
    nw = nw_ref[...]
    xn = _rmsnorm(x_ref[...], nw)
    xn_prev = jnp.where(i == 0, 0.0, _rmsnorm(xp_ref[...], nw))
    xn_next = jnp.where(i == pl.num_programs(0) - 1, 0.0, _rmsnorm(xnx_ref[...], nw))
    xn_b = xn.astype(BF16)
    xe_b = jnp.concatenate([xn_prev, xn, xn_next], axis=0).astype(BF16)
    rows = tm + 2 * PROJ_HALO

    w_gates = w_ref[:, 2 * QK_COLS:2 * QK_COLS + N_GATES]
    g = lax.dot_general(w_gates, xn_b, (((0,), (1,)), ((), ())), preferred_element_type=F32) + gb_ref[...]
    H = N_HEADS
    f_pre = g[2 * H:]
    lf = jnp.minimum(f_pre, 0.0) - jnp.log1p(jnp.exp(-jnp.abs(f_pre)))
    b_f = _chunk_scan(lf[:H], jnp.add, 0.0, False)
    b_b = _chunk_scan(lf[H:], jnp.add, 0.0, True)
    r_f = g[:H] - b_f
    r_b = g[H:2 * H] - b_b
    cm_f = _chunk_scan(r_f, jnp.maximum, -jnp.inf, False)
    cm_b = _chunk_scan(r_b, jnp.maximum, -jnp.inf, True)

    for c in range(QK_COLS // QK_CB):
        cs = slice(c * QK_CB, (c + 1) * QK_CB)
        z = jnp.dot(xe_b, w_ref[:, cs], preferred_element_type=F32)
        z_m1 = pltpu.roll(z, 1, 0)[PROJ_HALO:PROJ_HALO + tm]
        z_0 = z[PROJ_HALO:PROJ_HALO + tm]
        z_p1 = pltpu.roll(z, rows - 1, 0)[PROJ_HALO:PROJ_HALO + tm]
        y = z_m1 * cw_ref[0:1, cs] + z_0 * cw_ref[1:2, cs] + z_p1 * cw_ref[2:3, cs] + cb_ref[:, cs]
        y = y * _sigmoid(y)
        if c * QK_CB < D_MLSTM:
            q_ref[:, cs] = (y * (HEAD_DIM ** -0.5)).astype(q_ref.dtype)
        else:
            kt_ref[c * QK_CB - D_MLSTM:(c + 1) * QK_CB - D_MLSTM, :] = y.T.astype(kt_ref.dtype)

    v_ref[...] = jnp.dot(xn_b, w_ref[:, QK_COLS:QK_COLS + D_MLSTM], preferred_element_type=F32).astype(v_ref.dtype)
    o_ref[...] = jnp.dot(xn_b, w_ref[:, QK_COLS + D_MLSTM:QK_COLS + 2 * D_MLSTM], preferred_element_type=F32)
    u_ref[...] = jnp.dot(xn_b, wu_ref[...], preferred_element_type=F32)
    rb_ref[...] = jnp.concatenate([r_f, r_b, b_f, b_b], axis=0)
    pad = jnp.zeros((GATE_COL_LANES - 4 * H, tm), F32)
    col_ref[...] = jnp.concatenate([b_f, b_b, cm_f, cm_b, pad], axis=0).T


def _in_proj(x, norm_w, w_all, w_u, gate_bias, conv_w, conv_b):
    L = x.shape[0]
    tm = PROJ_TM
    assert L % tm == 0 and tm % CHUNK == 0
    hb = tm // PROJ_HALO
    nblk8 = L // PROJ_HALO
    whole = pl.BlockSpec(memory_space=pltpu.VMEM)
    return pl.pallas_call(
        _in_proj_body,
        grid=(L // tm,),
        in_specs=[
            pl.BlockSpec((PROJ_HALO, D_MODEL), lambda i: (jnp.maximum(i * hb - 1, 0), 0)),
            pl.BlockSpec((tm, D_MODEL), lambda i: (i, 0)),
            pl.BlockSpec((PROJ_HALO, D_MODEL), lambda i: (jnp.minimum((i + 1) * hb, nblk8 - 1), 0)),
            whole, whole, whole, whole, whole, whole,
        ],
        out_specs=[
            pl.BlockSpec((tm, D_MLSTM), lambda i: (i, 0)),
            pl.BlockSpec((D_MLSTM, tm), lambda i: (0, i)),
            pl.BlockSpec((tm, D_MLSTM), lambda i: (i, 0)),
            pl.BlockSpec((tm, D_MLSTM), lambda i: (i, 0)),
            pl.BlockSpec((tm, D_S5), lambda i: (i, 0)),
            pl.BlockSpec((N_GATES, tm), lambda i: (0, i)),
            pl.BlockSpec((tm, GATE_COL_LANES), lambda i: (i, 0)),
        ],
        out_shape=[
            jax.ShapeDtypeStruct((L, D_MLSTM), BF16),
            jax.ShapeDtypeStruct((D_MLSTM, L), BF16),
            jax.ShapeDtypeStruct((L, D_MLSTM), BF16),
            jax.ShapeDtypeStruct((L, D_MLSTM), F32),
            jax.ShapeDtypeStruct((L, D_S5), F32),
            jax.ShapeDtypeStruct((N_GATES, L), F32),
            jax.ShapeDtypeStruct((L, GATE_COL_LANES), F32),
        ],
        compiler_params=pltpu.CompilerParams(
            dimension_semantics=("parallel",), vmem_limit_bytes=VMEM_LIMIT_BYTES),
        name="in_proj",
    )(x, x, x, norm_w, w_all, w_u, gate_bias, conv_w, conv_b)


def _bf16_split3(x):
    hi = x.astype(BF16)
    r1 = x - hi.astype(F32)
    mid = r1.astype(BF16)
    lo = (r1 - mid.astype(F32)).astype(BF16)
    return hi, mid, lo


def _mlstm_direction(q_ref, kt_ref, v_ref, rb_ref, col_ref, h_ref, ct_ref, m_ref, ml_ref, d, sub):
    T = CHUNK
    H = N_HEADS
    backward = d == 1
    tt = slice(sub * T, (sub + 1) * T)
    rr = lax.broadcasted_iota(jnp.int32, (T, T), 0)
    cc = lax.broadcasted_iota(jnp.int32, (T, T), 1)
    mask = (cc >= rr) if backward else (cc <= rr)
    last = 0 if backward else T - 1

    r_rows = rb_ref[d * H:(d + 1) * H, tt]
    b_rows = rb_ref[(2 + d) * H:(3 + d) * H, tt]
    b_cols = col_ref[tt, d * H:(d + 1) * H]
    cm_cols = col_ref[tt, (2 + d) * H:(3 + d) * H]
    m_prev = m_ref[d]
    m_prev_l = ml_ref[d]
    m_cols = jnp.maximum(cm_cols, m_prev_l)
    clamp_cols = -(b_cols + m_cols)
    m_last = jnp.maximum(jnp.max(r_rows, axis=1, keepdims=True), m_prev)
    b_tot = jnp.broadcast_to(b_rows[:, last:last + 1], (H, T))
    sc_rows = jnp.exp(m_prev - m_last)
    wkk_rows = jnp.exp(r_rows - m_last)
    ones = jnp.ones((T, HEAD_DIM), BF16)

    for h in range(H):
        ci = d * H + h
        hs = slice(h * HEAD_DIM, (h + 1) * HEAD_DIM)
        q = q_ref[tt, hs]
        kt = kt_ref[hs, tt]
        vaug = jnp.concatenate([v_ref[tt, hs], ones], axis=1)
        ct_prev = ct_ref[ci]
        m_col = jnp.broadcast_to(m_cols[:, h:h + 1], (T, T))
        w = jnp.exp(jnp.where(mask, r_rows[h:h + 1, :] - m_col, -jnp.inf))
        sq = jnp.dot(q, jnp.concatenate([kt, ct_prev.astype(BF16)], axis=1), preferred_element_type=F32)
        s = sq[:, :T] * w
        qc = sq[:, T:]
        kw = (kt.astype(F32) * wkk_rows[h:h + 1, :]).astype(BF16)
        both = jnp.dot(jnp.concatenate([s.astype(BF16), kw], axis=0), vaug, preferred_element_type=F32)
        sv = both[:T]
        upd = both[T:]
        s_inter = jnp.exp(m_prev[h:h + 1, :] - m_col)
        num = sv[:, :HEAD_DIM] + s_inter * qc[:, :HEAD_DIM]
        den = sv[:, HEAD_DIM:] + s_inter * qc[:, HEAD_DIM:]
        floor = jnp.exp(jnp.broadcast_to(clamp_cols[:, h:h + 1], (T, HEAD_DIM)))
        h_ref[tt, hs] = num / jnp.maximum(jnp.abs(den), floor)

        sc = sc_rows[h:h + 1, :]
        ct_ref[ci] = jnp.concatenate([sc, sc], axis=1) * ct_prev + upd

    m_ref[d] = b_tot + m_last
    ml_ref[d] = b_cols[last:last + 1, :] + m_cols[last:last + 1, :]


def _mlstm_body(qf_ref, ktf_ref, vf_ref, rbf_ref, colf_ref, qb_ref, ktb_ref, vb_ref, rbb_ref, colb_ref,
                hf_ref, hb_ref, ct_ref, m_ref, ml_ref):
    @pl.when(pl.program_id(0) == 0)
    def _():
        ct_ref[...] = jnp.zeros_like(ct_ref)
        m_ref[...] = jnp.full_like(m_ref, M_INIT)
        ml_ref[...] = jnp.full_like(ml_ref, M_INIT)

    for sub in range(MLSTM_SUB):
        _mlstm_direction(qf_ref, ktf_ref, vf_ref, rbf_ref, colf_ref, hf_ref, ct_ref, m_ref, ml_ref, 0, sub)
        _mlstm_direction(qb_ref, ktb_ref, vb_ref, rbb_ref, colb_ref, hb_ref, ct_ref, m_ref, ml_ref, 1,
                         MLSTM_SUB - 1 - sub)


def _mlstm(q, kt, v, rb, col):
    L = q.shape[0]
    T = MLSTM_SUB * CHUNK
    assert L % T == 0 and CHUNK == HEAD_DIM
    nc = L // T
    fwd = lambda c: (c, 0)
    bwd = lambda c: (nc - 1 - c, 0)
    fwd_r = lambda c: (0, c)
    bwd_r = lambda c: (0, nc - 1 - c)
    return pl.pallas_call(
        _mlstm_body,
        grid=(nc,),
        in_specs=[
            pl.BlockSpec((T, D_MLSTM), fwd), pl.BlockSpec((D_MLSTM, T), fwd_r), pl.BlockSpec((T, D_MLSTM), fwd),
            pl.BlockSpec((N_GATES, T), fwd_r), pl.BlockSpec((T, GATE_COL_LANES), fwd),
            pl.BlockSpec((T, D_MLSTM), bwd), pl.BlockSpec((D_MLSTM, T), bwd_r), pl.BlockSpec((T, D_MLSTM), bwd),
            pl.BlockSpec((N_GATES, T), bwd_r), pl.BlockSpec((T, GATE_COL_LANES), bwd),
        ],
        out_specs=[pl.BlockSpec((T, D_MLSTM), fwd), pl.BlockSpec((T, D_MLSTM), bwd)],
        out_shape=[jax.ShapeDtypeStruct((L, D_MLSTM), F32), jax.ShapeDtypeStruct((L, D_MLSTM), F32)],
        scratch_shapes=[
            pltpu.VMEM((2 * N_HEADS, HEAD_DIM, 2 * HEAD_DIM), F32),
            pltpu.VMEM((2, N_HEADS, CHUNK), F32),
            pltpu.VMEM((2, 1, N_HEADS), F32),
        ],
        compiler_params=pltpu.CompilerParams(
            dimension_semantics=("arbitrary",), vmem_limit_bytes=VMEM_LIMIT_BYTES),
        name="mlstm",
    )(q, kt, v, rb, col, q, kt, v, rb, col)


S5_PAIR_CH = 2 * S5_GC
S5_ROW = S5_BLK * S5_PAIR_CH
S5_ST = 2 * S5_P
S5_TILE = 8
S5_LANE_PAIRS = 4
S5_TT = 4096
S5_NBT = S5_TT // S5_BLK


def _s5_scan(a, sin_ref, nb):
    R = S5_TILE
    ntile = nb // R
    row = lax.broadcasted_iota(jnp.int32, (R, S5_ST), 0)
    zero = jnp.zeros((1, S5_ST), F32)

    def cmul(x, y):
        return x[0] * y[0] - x[1] * y[1], x[0] * y[1] + x[1] * y[0]

    def bcast(x):
        return tuple(jnp.broadcast_to(t, (R, S5_ST)) for t in x)

    def tables(ar, ai, backward):
        pw = {1: (ar, ai)}
        for e in range(2, R + 1):
            pw[e] = cmul(pw[e // 2], pw[e - e // 2])
        steps = []
        for sh in (1, 2, 4):
            keep = (row < R - sh) if backward else (row >= sh)
            steps.append(tuple(jnp.where(keep, t, 0.0) for t in bcast(pw[sh])))
        order = range(R, 0, -1) if backward else range(1, R + 1)
        carry_pw = tuple(jnp.concatenate([pw[e][j] for e in order], axis=0) for j in (0, 1))
        return steps, carry_pw

    def scan_tile(x, carry, steps, cpw, backward):
        for sh, am in zip((1, 2, 4), steps):
            rs = (R - sh) if backward else sh
            x = tuple(p + q for p, q in zip(x, cmul(am, (pltpu.roll(x[0], rs, 0), pltpu.roll(x[1], rs, 0)))))
        cb = bcast(carry)
        x = tuple(p + q for p, q in zip(x, cmul(cpw, cb)))
        edge, rs = (R - 1, R - 1) if backward else (0, 1)
        enter = tuple(jnp.where(row == edge, c, pltpu.roll(t, rs, 0)) for t, c in zip(x, cb))
        last = 0 if backward else R - 1
        return enter, (x[0][last:last + 1], x[1][last:last + 1])

    tabs = []
    for p in range(S5_LANE_PAIRS):
        ap = a[p]
        tabs.append((tables(ap[0:1], ap[1:2], False), tables(ap[2:3], ap[3:4], True)))

    def step(i, carry):
        rf = pl.multiple_of(i * R, R)
        rb = pl.multiple_of((ntile - 1 - i) * R, R)
        out = []
        for p in range(S5_LANE_PAIRS):
            (steps_f, cpw_f), (steps_b, cpw_b) = tabs[p]
            cf, cb = carry[p]
            xf = (sin_ref[p, pl.ds(rf, R), 0:S5_ST], sin_ref[p, pl.ds(rf, R), S5_ST:2 * S5_ST])
            xb = (sin_ref[p, pl.ds(rb, R), 2 * S5_ST:3 * S5_ST], sin_ref[p, pl.ds(rb, R), 3 * S5_ST:4 * S5_ST])
            ef, cf = scan_tile(xf, cf, steps_f, cpw_f, False)
            eb, cb = scan_tile(xb, cb, steps_b, cpw_b, True)
            sin_ref[p, pl.ds(rf, R), 0:S5_ST] = ef[0]
            sin_ref[p, pl.ds(rf, R), S5_ST:2 * S5_ST] = ef[1]
            sin_ref[p, pl.ds(rb, R), 2 * S5_ST:3 * S5_ST] = eb[0]
            sin_ref[p, pl.ds(rb, R), 3 * S5_ST:4 * S5_ST] = eb[1]
            out.append((cf, cb))
        return tuple(out)

    init = tuple(((zero, zero), (zero, zero)) for _ in range(S5_LANE_PAIRS))
    lax.fori_loop(0, ntile, step, init)


def _s5_body(u_ref, m_ref, win_ref, wout_ref, a_ref, y_ref, u2_ref, sin_ref, y2_ref):
    phase = pl.program_id(1)
    t = pl.program_id(2)
    nb = u2_ref.shape[1]
    r0 = pl.multiple_of(t * S5_NBT, S5_NBT)
    rows = pl.ds(r0, S5_NBT)

    n_slot = S5_LANE_PAIRS
    lt = S5_LANE_PAIRS * S5_PAIR_CH
    slot = lax.broadcasted_iota(jnp.int32, (S5_NBT, lt), 1) // S5_PAIR_CH

    def pick(parts, first):
        out = parts[n_slot - 1]
        for i in range(n_slot - 2, -1, -1):
            out = jnp.where(slot == (first + i) % n_slot, parts[i], out)
        return out

    @pl.when(phase == 0)
    def _():
        for q in range(S5_BLK // n_slot):
            rot = []
            for i in range(n_slot):
                tok = u_ref[pl.ds(n_slot * q + i, S5_NBT, stride=S5_BLK), :]
                rot.append(pltpu.roll(tok, i * S5_PAIR_CH, 1) if i else tok)
            for p in range(S5_LANE_PAIRS):
                u2_ref[p, rows, q * lt:(q + 1) * lt] = pick(rot, p).astype(BF16)
        for p in range(S5_LANE_PAIRS):
            sin_ref[p, rows, :] = jnp.dot(u2_ref[p, rows, :], win_ref[p], preferred_element_type=F32)

    @pl.when((phase == 0) & (t == pl.num_programs(2) - 1))
    def _():
        _s5_scan(a_ref[...], sin_ref, nb)

    @pl.when(phase == 1)
    def _():
        for p in range(S5_LANE_PAIRS):
            y2_ref[p] = (jnp.dot(u2_ref[p, rows, :], m_ref[p], preferred_element_type=F32)
                         + jnp.dot(sin_ref[p, rows, :].astype(BF16), wout_ref[p], preferred_element_type=F32))
        for tkn in range(S5_BLK):
            q, i = divmod(tkn, n_slot)
            merged = pick([y2_ref[p, :, q * lt:(q + 1) * lt] for p in range(S5_LANE_PAIRS)], i)
            out = pltpu.roll(merged, ((n_slot - i) % n_slot) * S5_PAIR_CH, 1) if i else merged
            y_ref[pl.ds(tkn, S5_NBT, stride=S5_BLK), :] = out


def _s5(u, m2, win2, wout2, a2):
    L = u.shape[0]
    assert L % S5_TT == 0 and D_S5 == S5_PAIRS * S5_PAIR_CH
    nb = L // S5_BLK
    nq = S5_PAIRS // S5_LANE_PAIRS
    lane_tile = S5_LANE_PAIRS * S5_PAIR_CH
    wspec = pl.BlockSpec((S5_LANE_PAIRS, S5_ROW, S5_ROW), lambda q, ph, t: (q, 0, 0))
    return pl.pallas_call(
        _s5_body,
        grid=(nq, 2, L // S5_TT),
        in_specs=[
            pl.BlockSpec((S5_TT, lane_tile), lambda q, ph, t: (t, q)),
            wspec, wspec, wspec,
            pl.BlockSpec((S5_LANE_PAIRS, 4, S5_ST), lambda q, ph, t: (q, 0, 0)),
        ],
        out_specs=pl.BlockSpec((S5_TT, lane_tile), lambda q, ph, t: (t * ph, q)),
        out_shape=jax.ShapeDtypeStruct((L, D_S5), F32),
        scratch_shapes=[
            pltpu.VMEM((S5_LANE_PAIRS, nb, S5_ROW), BF16),
            pltpu.VMEM((S5_LANE_PAIRS, nb, 4 * S5_ST), F32),
            pltpu.VMEM((S5_LANE_PAIRS, S5_NBT, S5_ROW), F32),
        ],
        compiler_params=pltpu.CompilerParams(
            dimension_semantics=("parallel", "arbitrary", "arbitrary"), vmem_limit_bytes=VMEM_LIMIT_BYTES),
        name="s5",
    )(u, m2, win2, wout2, a2)


S5_EXP = (S5_BLK + 1) * S5_PAIR_CH
S5_EXP_PAD = 640
S5_PWT_ROWS = 24


def _s5_prep_body(pwk_ref, pwt_ref, ct_ref, b2t_ref, d_ref, m_ref, win_ref, wout_ref):
    nt, w = S5_BLK, S5_PAIR_CH
    lane = lax.broadcasted_iota(jnp.int32, (S5_ST, S5_EXP_PAD), 1)
    row = lax.broadcasted_iota(jnp.int32, (S5_ST, S5_EXP_PAD), 0)
    sel_slot = (lane // w == row).astype(BF16)
    sel_chan = ((lane % S5_GC == row) & (row < S5_GC)).astype(BF16)
    same_group = (row // S5_P == (lane // S5_GC) % 2) & (lane < S5_EXP)

    def expand(x, sel):
        return sum(jnp.dot(p, sel, preferred_element_type=F32) for p in _bf16_split3(x))

    def split_dot(a, x):
        a_hi, a_lo, _ = _bf16_split3(a)
        x_hi, x_lo, _ = _bf16_split3(x)
        return (jnp.dot(a_hi, x_hi, preferred_element_type=F32) + jnp.dot(a_hi, x_lo, preferred_element_type=F32)
                + jnp.dot(a_lo, x_hi, preferred_element_type=F32))

    lane_m = lax.broadcasted_iota(jnp.int32, (w, S5_ROW), 1)
    row_m = lax.broadcasted_iota(jnp.int32, (w, S5_ROW), 0)
    krow, wout_rows, win_rows = [], [], []
    for z in range(2):
        pr, pi = expand(pwk_ref[z, 0, 0], sel_slot), expand(pwk_ref[z, 1, 0], sel_slot)
        cr, ci = expand(ct_ref[z, 0, 0], sel_chan), expand(ct_ref[z, 1, 0], sel_chan)
        xr = jnp.where(same_group, pr * cr - pi * ci, 0.0)
        xi = jnp.where(same_group, pr * ci + pi * cr, 0.0)
        lo = w if z == 0 else 0
        wout_rows += [xr[:, lo:lo + S5_ROW], -xi[:, lo:lo + S5_ROW]]
        k_all = split_dot(b2t_ref[z, 0, 0], xr) - split_dot(b2t_ref[z, 1, 0], xi)
        lo = 0 if z == 0 else w
        krow.append(k_all[:, lo:lo + S5_ROW])
        br, bi = b2t_ref[z, 0, 0], b2t_ref[z, 1, 0]
        blocks = []
        for s in range(nt):
            e = nt - 1 - s if z == 0 else s
            qr, qi = pwt_ref[z, 0, 0, e:e + 1, :], pwt_ref[z, 1, 0, e:e + 1, :]
            blocks.append((br * qr - bi * qi, br * qi + bi * qr))
        win_rows.append(blocks)

    kf, kb = krow
    d_diag = jnp.where((lane_m % w) == row_m, d_ref[0], 0.0)
    m_rows = []
    for s in range(nt):
        f = jnp.where(lane_m >= w * s, pltpu.roll(kf, w * s, 1), 0.0) if s else kf
        sh = w * (nt - 1 - s)
        b = jnp.where(lane_m < S5_ROW - sh, pltpu.roll(kb, S5_ROW - sh, 1), 0.0) if sh else kb
        dd = jnp.where(lane_m // w == s, d_diag, 0.0)
        m_rows.append(f + b + dd)

    n_slot = S5_LANE_PAIRS
    lt = n_slot * w

    pp = pl.program_id(1)

    def order_cols(x):
        return jnp.concatenate([pltpu.roll(x[:, q * lt:(q + 1) * lt], pp * w, 1) for q in range(S5_ROW // lt)], axis=1)

    for k, x in enumerate(wout_rows):
        wout_ref[0, k * S5_ST:(k + 1) * S5_ST, :] = order_cols(x).astype(BF16)
    for s in range(nt):
        r0 = pl.multiple_of((n_slot * (s // n_slot) + (s % n_slot + pp) % n_slot) * w, w)
        m_ref[0, pl.ds(r0, w), :] = order_cols(m_rows[s]).astype(BF16)
        for z in range(2):
            wr, wi = win_rows[z][s]
            win_ref[0, pl.ds(r0, w), (2 * z) * S5_ST:(2 * z + 1) * S5_ST] = wr.astype(BF16)
            win_ref[0, pl.ds(r0, w), (2 * z + 1) * S5_ST:(2 * z + 2) * S5_ST] = wi.astype(BF16)


def _s5_prep(pwk, pwt, ct, b2t, d2):
    npair = d2.shape[0]
    n = S5_LANE_PAIRS
    spec = lambda r, c: pl.BlockSpec((2, 2, 1, r, c), lambda g, p: (0, 0, g * n + p, 0, 0))
    out = pl.BlockSpec((1, S5_ROW, S5_ROW), lambda g, p: (g * n + p, 0, 0))
    shape = jax.ShapeDtypeStruct((npair, S5_ROW, S5_ROW), BF16)
    return pl.pallas_call(
        _s5_prep_body,
        grid=(npair // n, n),
        in_specs=[spec(S5_ST, S5_ST), spec(S5_PWT_ROWS, S5_ST), spec(S5_ST, S5_ST), spec(S5_PAIR_CH, S5_ST),
                  pl.BlockSpec((1, 1, S5_ROW), lambda g, p: (g * n + p, 0, 0))],
        out_specs=[out, out, out],
        out_shape=[shape, shape, shape],
        compiler_params=pltpu.CompilerParams(
            dimension_semantics=("parallel", "arbitrary"), vmem_limit_bytes=VMEM_LIMIT_BYTES),
        name="s5_prep",
    )(pwk, pwt, ct, b2t, d2)


def _s5_weights(a_re, a_im, log_dt, b_re, b_im, c_re, c_im, d_skip):
    nt, npair = S5_BLK, S5_PAIRS
    lam = lax.complex(a_re, a_im)
    dt = jnp.exp(log_dt)[..., None]
    lam_bar = jnp.exp(lam * dt)
    b_bar = ((lam_bar - 1.0) / lam)[..., None] * lax.complex(b_re, b_im)
    taus = jnp.arange(nt + 1, dtype=F32)
    pw = jnp.exp((lam * dt)[..., None] * taus).reshape(2, npair, S5_ST, nt + 1)
    ri = lambda x: jnp.stack([jnp.real(x), jnp.imag(x)], axis=1)
    pwk = jnp.stack([pw[0], pw[1, ..., ::-1]])
    pwk = jnp.pad(ri(pwk), ((0, 0), (0, 0), (0, 0), (0, 0), (0, S5_ST - (nt + 1))))
    pwt = jnp.pad(ri(pw).transpose(0, 1, 2, 4, 3), ((0, 0), (0, 0), (0, 0), (0, S5_PWT_ROWS - (nt + 1)), (0, 0)))
    ct = lax.complex(c_re, c_im).transpose(0, 1, 3, 2).reshape(2, npair, S5_ST, S5_GC)
    ct = jnp.pad(ri(ct), ((0, 0), (0, 0), (0, 0), (0, 0), (0, S5_ST - S5_GC)))
    bb = b_bar.reshape(2, npair, 2, S5_P, S5_GC)
    eye2 = jnp.eye(2, dtype=F32)
    b2t = (bb.transpose(0, 1, 2, 4, 3)[:, :, :, :, None, :] * eye2[None, None, :, None, :, None])
    b2t = ri(b2t.reshape(2, npair, S5_PAIR_CH, S5_ST))
    d2 = jnp.tile(d_skip.reshape(npair, 1, S5_PAIR_CH), (1, 1, nt))
    m2, win2, wout2 = _s5_prep(pwk, pwt, ct, b2t, d2)
    a_blk = pw[..., nt]
    a2 = jnp.stack([jnp.real(a_blk[0]), jnp.imag(a_blk[0]), jnp.real(a_blk[1]), jnp.imag(a_blk[1])], axis=1)
    return m2, win2, wout2, a2


MIX_CB = 256


def _mix_body(x_ref, hf_ref, hb_ref, o_ref, y_ref, nw_ref, wglu_ref, wout_ref, out_ref, hm_ref, g_ref, hs_ref):
    for hd in range(N_HEADS):
        hs = slice(hd * HEAD_DIM, (hd + 1) * HEAD_DIM)
        hh = hf_ref[:, hs] + hb_ref[:, hs]
        mu = jnp.mean(hh, axis=-1, keepdims=True)
        var = jnp.mean(jnp.square(hh - mu), axis=-1, keepdims=True)
        hn = (hh - mu) * lax.rsqrt(var + EPS)
        hm_ref[:, hs] = (hn * nw_ref[:, hs] * _sigmoid(o_ref[:, hs])).astype(BF16)
    for c in range(D_S5 // MIX_CB):
        cs = slice(c * MIX_CB, (c + 1) * MIX_CB)
        y = y_ref[:, cs]
        g_ref[:, cs] = (0.5 * y * (1.0 + jnp.tanh(math.sqrt(2.0 / math.pi) * (y + 0.044715 * (y * y * y))))).astype(BF16)
    for c in range(D_S5 // MIX_CB):
        cs = slice(c * MIX_CB, (c + 1) * MIX_CB)
        gs = slice(D_S5 + c * MIX_CB, D_S5 + (c + 1) * MIX_CB)
        val = jnp.dot(g_ref[...], wglu_ref[:, cs], preferred_element_type=F32)
        gate = jnp.dot(g_ref[...], wglu_ref[:, gs], preferred_element_type=F32)
        hs_ref[:, cs] = (val * _sigmoid(gate)).astype(BF16)
    for c in range(D_MODEL // MIX_CB):
        cs = slice(c * MIX_CB, (c + 1) * MIX_CB)
        out_ref[:, cs] = (x_ref[:, cs] + jnp.dot(hm_ref[...], wout_ref[0:D_MLSTM, cs], preferred_element_type=F32)
                          + jnp.dot(hs_ref[...], wout_ref[D_MLSTM:, cs], preferred_element_type=F32))


def _mix(x, hf, hb, o_in, y, norm_w, w_glu, w_out):
    L = x.shape[0]
    tm = MIX_TM
    assert L % tm == 0
    whole = pl.BlockSpec(memory_space=pltpu.VMEM)
    row = lambda n: pl.BlockSpec((tm, n), lambda i: (i, 0))
    return pl.pallas_call(
        _mix_body,
        grid=(L // tm,),
        in_specs=[row(D_MODEL), row(D_MLSTM), row(D_MLSTM), row(D_MLSTM), row(D_S5), whole, whole, whole],
        out_specs=row(D_MODEL),
        out_shape=jax.ShapeDtypeStruct((L, D_MODEL), F32),
        scratch_shapes=[pltpu.VMEM((tm, D_MLSTM), BF16), pltpu.VMEM((tm, D_S5), BF16), pltpu.VMEM((tm, D_S5), BF16)],
        compiler_params=pltpu.CompilerParams(
            dimension_semantics=("parallel",), vmem_limit_bytes=VMEM_LIMIT_BYTES),
        name="mix",
    )(x, hf, hb, o_in, y, norm_w, w_glu, w_out)


def _encode(x, p):
    x = _ffn(x, p["norm_ffn1"], p["ffn1_w_gate"], p["ffn1_w_up"], p["ffn1_w_down"], p["norm_final"],
             final_norm=False)
    q, kt, v, o_in, u, rb, col = _in_proj(x, p["norm_mix"], p["w_all"], p["w_u"], p["gate_bias_r"],
                                          p["conv_w"], p["conv_b"])
    hf, hb = _mlstm(q, kt, v, rb, col)
    y = _s5(u, p["s5_m"], p["s5_win"], p["s5_wout"], p["s5_a"])
    x = _mix(x, hf, hb, o_in, y, p["mlstm_norm_w"], p["s5_w_glu"], p["w_out"])
    return _ffn(x, p["norm_ffn2"], p["ffn2_w_gate"], p["ffn2_w_up"], p["ffn2_w_down"], p["norm_final"],
                final_norm=True)


def _prepare(norm_ffn1, ffn1_w_gate, ffn1_w_up, ffn1_w_down, norm_mix, w_in, conv_w, conv_b, b_igate, b_fgate,
             mlstm_norm_w, s5_a_re, s5_a_im, s5_log_dt, s5_b_re, s5_b_im, s5_c_re, s5_c_im, s5_d, s5_w_glu,
             w_out, norm_ffn2, ffn2_w_gate, ffn2_w_up, ffn2_w_down, norm_final):
    l = 0
    w = w_in[l]
    g0 = 4 * D_MLSTM
    gate_bias = jnp.concatenate([b_igate[l].reshape(-1), b_fgate[l].reshape(-1)])
    m2, win2, wout2, a2 = _s5_weights(s5_a_re[l], s5_a_im[l], s5_log_dt[l], s5_b_re[l], s5_b_im[l],
                                      s5_c_re[l], s5_c_im[l], s5_d[l])
    row = lambda a: a.reshape(1, -1).astype(F32)
    return {
        "norm_ffn1": row(norm_ffn1[l]), "norm_ffn2": row(norm_ffn2[l]), "norm_final": row(norm_final),
        "ffn1_w_gate": ffn1_w_gate[l].astype(BF16), "ffn1_w_up": ffn1_w_up[l].astype(BF16),
        "ffn1_w_down": ffn1_w_down[l].astype(BF16),
        "ffn2_w_gate": ffn2_w_gate[l].astype(BF16), "ffn2_w_up": ffn2_w_up[l].astype(BF16),
        "ffn2_w_down": ffn2_w_down[l].astype(BF16),
        "norm_mix": row(norm_mix[l]),
        "w_all": w.astype(BF16),
        "w_u": w[:, g0 + N_GATES:].astype(BF16),
        "conv_w": conv_w[l].astype(F32), "conv_b": row(conv_b[l]),
        "gate_bias_r": gate_bias.reshape(N_GATES, 1),
        "mlstm_norm_w": row(mlstm_norm_w[l]),
        "s5_m": m2, "s5_win": win2, "s5_wout": wout2, "s5_a": a2,
        "s5_w_glu": s5_w_glu[l].astype(BF16), "w_out": w_out[l].astype(BF16),
    }


def kernel(x_prompt, x_sample, norm_ffn1, ffn1_w_gate, ffn1_w_up, ffn1_w_down, norm_mix, w_in, conv_w, conv_b, b_igate, b_fgate, mlstm_norm_w, s5_a_re, s5_a_im, s5_log_dt, s5_b_re, s5_b_im, s5_c_re, s5_c_im, s5_d, s5_w_glu, w_out, norm_ffn2, ffn2_w_gate, ffn2_w_up, ffn2_w_down, norm_final):
    assert norm_ffn1.shape[0] == 1 and x_prompt.shape[0] == 1 and x_sample.shape[0] == 1
    p = _prepare(norm_ffn1, ffn1_w_gate, ffn1_w_up, ffn1_w_down, norm_mix, w_in, conv_w, conv_b, b_igate,
                 b_fgate, mlstm_norm_w, s5_a_re, s5_a_im, s5_log_dt, s5_b_re, s5_b_im, s5_c_re, s5_c_im, s5_d,
                 s5_w_glu, w_out, norm_ffn2, ffn2_w_gate, ffn2_w_up, ffn2_w_down, norm_final)
    y_prompt = _encode(x_prompt[0], p)[None]
    y_sample = _encode(x_sample[0], p)[None]
    return (y_prompt, y_sample)
```

```python
import functools
import math

import jax
import jax.numpy as jnp
from jax import lax
from jax.experimental import pallas as pl
from jax.experimental.pallas import tpu as pltpu

F32 = jnp.float32
BF16 = jnp.bfloat16

D_MODEL = 2048
D_MLSTM = 1024
D_S5 = 1024
N_HEADS = 8
HEAD_DIM = 128
CHUNK = 128
MLSTM_SUB = 4
N_GATES = 32
GATE_COL_LANES = 128
S5_GROUPS = 64
S5_GC = 16
S5_P = 64
S5_BLK = 16
S5_PAIRS = S5_GROUPS // 2
D_FF = 5632
EPS = 1e-6
M_INIT = -1e30

VMEM_LIMIT_BYTES = 56 * 1024 * 1024

FFN_TM = 1024
FFN_TF = 512
PROJ_TM = 256
PROJ_HALO = 8
MIX_TM = 512


def _sigmoid(x):
    return 1.0 / (1.0 + jnp.exp(-x))


def _rmsnorm(x, w):
    return x * lax.rsqrt(jnp.mean(x * x, axis=-1, keepdims=True) + EPS) * w


def _ffn_body(x_ref, nw_ref, wg_ref, wu_ref, wd_ref, nf_ref, o_ref, xn_ref, *, final_norm):
    j = pl.program_id(1)

    @pl.when(j == 0)
    def _():
        x = x_ref[...]
        xn_ref[...] = _rmsnorm(x, nw_ref[...]).astype(BF16)
        o_ref[...] = x

    xn = xn_ref[...]
    g = jnp.dot(xn, wg_ref[...], preferred_element_type=F32)
    u = jnp.dot(xn, wu_ref[...], preferred_element_type=F32)
    h = (0.5 * g * _sigmoid(g)) * u
    o_ref[...] += jnp.dot(h.astype(BF16), wd_ref[...], preferred_element_type=F32)

    if final_norm:
        @pl.when(j == pl.num_programs(1) - 1)
        def _():
            o_ref[...] = _rmsnorm(o_ref[...], nf_ref[...])


def _ffn(x, norm_w, w_gate, w_up, w_down, norm_final, *, final_norm):
    L = x.shape[0]
    assert L % FFN_TM == 0 and D_FF % FFN_TF == 0
    return pl.pallas_call(
        functools.partial(_ffn_body, final_norm=final_norm),
        grid=(L // FFN_TM, D_FF // FFN_TF),
        in_specs=[
            pl.BlockSpec((FFN_TM, D_MODEL), lambda i, j: (i, 0)),
            pl.BlockSpec((1, D_MODEL), lambda i, j: (0, 0)),
            pl.BlockSpec((D_MODEL, FFN_TF), lambda i, j: (0, j)),
            pl.BlockSpec((D_MODEL, FFN_TF), lambda i, j: (0, j)),
            pl.BlockSpec((FFN_TF, D_MODEL), lambda i, j: (j, 0)),
            pl.BlockSpec((1, D_MODEL), lambda i, j: (0, 0)),
        ],
        out_specs=pl.BlockSpec((FFN_TM, D_MODEL), lambda i, j: (i, 0)),
        out_shape=jax.ShapeDtypeStruct((L, D_MODEL), F32),
        scratch_shapes=[pltpu.VMEM((FFN_TM, D_MODEL), BF16)],
        compiler_params=pltpu.CompilerParams(
            dimension_semantics=("parallel", "arbitrary"), vmem_limit_bytes=VMEM_LIMIT_BYTES),
        name="ffn_final" if final_norm else "ffn",
    )(x, norm_w, w_gate, w_up, w_down, norm_final)


QK_COLS = 2 * D_MLSTM
QK_CB = 512


def _chunk_scan(x, op, identity, backward):
    n = x.shape[-1]
    pos = lax.broadcasted_iota(jnp.int32, x.shape, x.ndim - 1) % CHUNK
    sh = 1
    while sh < CHUNK:
        if backward:
            x = op(x, jnp.where(pos < CHUNK - sh, pltpu.roll(x, n - sh, x.ndim - 1), identity))
        else:
            x = op(x, jnp.where(pos >= sh, pltpu.roll(x, sh, x.ndim - 1), identity))
        sh *= 2
    return x


def _in_proj_body(xp_ref, x_ref, xnx_ref, nw_ref, w_ref, wu_ref, gb_ref,
                  cw_ref, cb_ref, q_ref, kt_ref, v_ref, o_ref, u_ref, rb_ref, col_ref):
    i = pl.program_id(0)
    tm = x_ref.shape[0]
    nw = nw_ref[...]
    xn = _rmsnorm(x_ref[...], nw)
    xn_prev = jnp.where(i == 0, 0.0, _rmsnorm(xp_ref[...], nw))
    xn_next = jnp.where(i == pl.num_programs(0) - 1, 0.0, _rmsnorm(xnx_ref[...], nw))
    xn_b = xn.astype(BF16)
    xe_b = jnp.concatenate([xn_prev, xn, xn_next], axis=0).astype(BF16)
    rows = tm + 2 * PROJ_HALO

    w_gates = w_ref[:, 2 * QK_COLS:2 * QK_COLS + N_GATES]
    g = lax.dot_general(w_gates, xn_b, (((0,), (1,)), ((), ())), preferred_element_type=F32) + gb_ref[...]
    H = N_HEADS
    f_pre = g[2 * H:]
    lf = jnp.minimum(f_pre, 0.0) - jnp.log1p(jnp.exp(-jnp.abs(f_pre)))
    b_f = _chunk_scan(lf[:H], jnp.add, 0.0, False)
    b_b = _chunk_scan(lf[H:], jnp.add, 0.0, True)
    r_f = g[:H] - b_f
    r_b = g[H:2 * H] - b_b
    cm_f = _chunk_scan(r_f, jnp.maximum, -jnp.inf, False)
    cm_b = _chunk_scan(r_b, jnp.maximum, -jnp.inf, True)

    for c in range(QK_COLS // QK_CB):
        cs = slice(c * QK_CB, (c + 1) * QK_CB)
        z = jnp.dot(xe_b, w_ref[:, cs], preferred_element_type=F32)
        z_m1 = pltpu.roll(z, 1, 0)[PROJ_HALO:PROJ_HALO + tm]
        z_0 = z[PROJ_HALO:PROJ_HALO + tm]
        z_p1 = pltpu.roll(z, rows - 1, 0)[PROJ_HALO:PROJ_HALO + tm]
        y = z_m1 * cw_ref[0:1, cs] + z_0 * cw_ref[1:2, cs] + z_p1 * cw_ref[2:3, cs] + cb_ref[:, cs]
        y = y * _sigmoid(y)
        if c * QK_CB < D_MLSTM:
            q_ref[:, cs] = (y * (HEAD_DIM ** -0.5)).astype(q_ref.dtype)
        else:
            kt_ref[c * QK_CB - D_MLSTM:(c + 1) * QK_CB - D_MLSTM, :] = y.T.astype(kt_ref.dtype)

    v_ref[...] = jnp.dot(xn_b, w_ref[:, QK_COLS:QK_COLS + D_MLSTM], preferred_element_type=F32).astype(v_ref.dtype)
    o_ref[...] = jnp.dot(xn_b, w_ref[:, QK_COLS + D_MLSTM:QK_COLS + 2 * D_MLSTM], preferred_element_type=F32)
    u_ref[...] = jnp.dot(xn_b, wu_ref[...], preferred_element_type=F32)
    rb_ref[...] = jnp.concatenate([r_f, r_b, b_f, b_b], axis=0)
    pad = jnp.zeros((GATE_COL_LANES - 4 * H, tm), F32)
    col_ref[...] = jnp.concatenate([b_f, b_b, cm_f, cm_b, pad], axis=0).T


def _in_proj(x, norm_w, w_all, w_u, gate_bias, conv_w, conv_b):
    L = x.shape[0]
    tm = PROJ_TM
    assert L % tm == 0 and tm % CHUNK == 0
    hb = tm // PROJ_HALO
    nblk8 = L // PROJ_HALO
    whole = pl.BlockSpec(memory_space=pltpu.VMEM)
    return pl.pallas_call(
        _in_proj_body,
        grid=(L // tm,),
        in_specs=[
            pl.BlockSpec((PROJ_HALO, D_MODEL), lambda i: (jnp.maximum(i * hb - 1, 0), 0)),
            pl.BlockSpec((tm, D_MODEL), lambda i: (i, 0)),
            pl.BlockSpec((PROJ_HALO, D_MODEL), lambda i: (jnp.minimum((i + 1) * hb, nblk8 - 1), 0)),
            whole, whole, whole, whole, whole, whole,
        ],
        out_specs=[
            pl.BlockSpec((tm, D_MLSTM), lambda i: (i, 0)),
            pl.BlockSpec((D_MLSTM, tm), lambda i: (0, i)),
            pl.BlockSpec((tm, D_MLSTM), lambda i: (i, 0)),
            pl.BlockSpec((tm, D_MLSTM), lambda i: (i, 0)),
            pl.BlockSpec((tm, D_S5), lambda i: (i, 0)),
            pl.BlockSpec((N_GATES, tm), lambda i: (0, i)),
            pl.BlockSpec((tm, GATE_COL_LANES), lambda i: (i, 0)),
        ],
        out_shape=[
            jax.ShapeDtypeStruct((L, D_MLSTM), BF16),
            jax.ShapeDtypeStruct((D_MLSTM, L), BF16),
            jax.ShapeDtypeStruct((L, D_MLSTM), BF16),
            jax.ShapeDtypeStruct((L, D_MLSTM), F32),
            jax.ShapeDtypeStruct((L, D_S5), F32),
            jax.ShapeDtypeStruct((N_GATES, L), F32),
            jax.ShapeDtypeStruct((L, GATE_COL_LANES), F32),
        ],
        compiler_params=pltpu.CompilerParams(
            dimension_semantics=("parallel",), vmem_limit_bytes=VMEM_LIMIT_BYTES),
        name="in_proj",
    )(x, x, x, norm_w, w_all, w_u, gate_bias, conv_w, conv_b)


def _bf16_split3(x):
    hi = x.astype(BF16)
    r1 = x - hi.astype(F32)
    mid = r1.astype(BF16)
    lo = (r1 - mid.astype(F32)).astype(BF16)
    return hi, mid, lo


def _mlstm_direction(q_ref, kt_ref, v_ref, rb_ref, col_ref, h_ref, ct_ref, m_ref, ml_ref, d, sub):
    T = CHUNK
    H = N_HEADS
    backward = d == 1
    tt = slice(sub * T, (sub + 1) * T)
    rr = lax.broadcasted_iota(jnp.int32, (T, T), 0)
    cc = lax.broadcasted_iota(jnp.int32, (T, T), 1)
    mask = (cc >= rr) if backward else (cc <= rr)
    last = 0 if backward else T - 1

    r_rows = rb_ref[d * H:(d + 1) * H, tt]
    b_rows = rb_ref[(2 + d) * H:(3 + d) * H, tt]
    b_cols = col_ref[tt, d * H:(d + 1) * H]
    cm_cols = col_ref[tt, (2 + d) * H:(3 + d) * H]
    m_prev = m_ref[d]
    m_prev_l = ml_ref[d]
    m_cols = jnp.maximum(cm_cols, m_prev_l)
    clamp_cols = -(b_cols + m_cols)
    m_last = jnp.maximum(jnp.max(r_rows, axis=1, keepdims=True), m_prev)
    b_tot = jnp.broadcast_to(b_rows[:, last:last + 1], (H, T))
    sc_rows = jnp.exp(m_prev - m_last)
    wkk_rows = jnp.exp(r_rows - m_last)
    ones = jnp.ones((T, HEAD_DIM), BF16)

    for h in range(H):
        ci = d * H + h
        hs = slice(h * HEAD_DIM, (h + 1) * HEAD_DIM)
        q = q_ref[tt, hs]
        kt = kt_ref[hs, tt]
        vaug = jnp.concatenate([v_ref[tt, hs], ones], axis=1)
        ct_prev = ct_ref[ci]
        m_col = jnp.broadcast_to(m_cols[:, h:h + 1], (T, T))
        w = jnp.exp(jnp.where(mask, r_rows[h:h + 1, :] - m_col, -jnp.inf))
        sq = jnp.dot(q, jnp.concatenate([kt, ct_prev.astype(BF16)], axis=1), preferred_element_type=F32)
        s = sq[:, :T] * w
        qc = sq[:, T:]
        kw = (kt.astype(F32) * wkk_rows[h:h + 1, :]).astype(BF16)
        both = jnp.dot(jnp.concatenate([s.astype(BF16), kw], axis=0), vaug, preferred_element_type=F32)
        sv = both[:T]
        upd = both[T:]
        s_inter = jnp.exp(m_prev[h:h + 1, :] - m_col)
        num = sv[:, :HEAD_DIM] + s_inter * qc[:, :HEAD_DIM]
        den = sv[:, HEAD_DIM:] + s_inter * qc[:, HEAD_DIM:]
        floor = jnp.exp(jnp.broadcast_to(clamp_cols[:, h:h + 1], (T, HEAD_DIM)))
        h_ref[tt, hs] = num / jnp.maximum(jnp.abs(den), floor)

        sc = sc_rows[h:h + 1, :]
        ct_ref[ci] = jnp.concatenate([sc, sc], axis=1) * ct_prev + upd

    m_ref[d] = b_tot + m_last
    ml_ref[d] = b_cols[last:last + 1, :] + m_cols[last:last + 1, :]


def _mlstm_body(qf_ref, ktf_ref, vf_ref, rbf_ref, colf_ref, qb_ref, ktb_ref, vb_ref, rbb_ref, colb_ref,
                hf_ref, hb_ref, ct_ref, m_ref, ml_ref):
    @pl.when(pl.program_id(0) == 0)
    def _():
        ct_ref[...] = jnp.zeros_like(ct_ref)
        m_ref[...] = jnp.full_like(m_ref, M_INIT)
        ml_ref[...] = jnp.full_like(ml_ref, M_INIT)

    for sub in range(MLSTM_SUB):
        _mlstm_direction(qf_ref, ktf_ref, vf_ref, rbf_ref, colf_ref, hf_ref, ct_ref, m_ref, ml_ref, 0, sub)
        _mlstm_direction(qb_ref, ktb_ref, vb_ref, rbb_ref, colb_ref, hb_ref, ct_ref, m_ref, ml_ref, 1,
                         MLSTM_SUB - 1 - sub)


def _mlstm(q, kt, v, rb, col):
    L = q.shape[0]
    T = MLSTM_SUB * CHUNK
    assert L % T == 0 and CHUNK == HEAD_DIM
    nc = L // T
    fwd = lambda c: (c, 0)
    bwd = lambda c: (nc - 1 - c, 0)
    fwd_r = lambda c: (0, c)
    bwd_r = lambda c: (0, nc - 1 - c)
    return pl.pallas_call(
        _mlstm_body,
        grid=(nc,),
        in_specs=[
            pl.BlockSpec((T, D_MLSTM), fwd), pl.BlockSpec((D_MLSTM, T), fwd_r), pl.BlockSpec((T, D_MLSTM), fwd),
            pl.BlockSpec((N_GATES, T), fwd_r), pl.BlockSpec((T, GATE_COL_LANES), fwd),
            pl.BlockSpec((T, D_MLSTM), bwd), pl.BlockSpec((D_MLSTM, T), bwd_r), pl.BlockSpec((T, D_MLSTM), bwd),
            pl.BlockSpec((N_GATES, T), bwd_r), pl.BlockSpec((T, GATE_COL_LANES), bwd),
        ],
        out_specs=[pl.BlockSpec((T, D_MLSTM), fwd), pl.BlockSpec((T, D_MLSTM), bwd)],
        out_shape=[jax.ShapeDtypeStruct((L, D_MLSTM), F32), jax.ShapeDtypeStruct((L, D_MLSTM), F32)],
        scratch_shapes=[
            pltpu.VMEM((2 * N_HEADS, HEAD_DIM, 2 * HEAD_DIM), F32),
            pltpu.VMEM((2, N_HEADS, CHUNK), F32),
            pltpu.VMEM((2, 1, N_HEADS), F32),
        ],
        compiler_params=pltpu.CompilerParams(
            dimension_semantics=("arbitrary",), vmem_limit_bytes=VMEM_LIMIT_BYTES),
        name="mlstm",
    )(q, kt, v, rb, col, q, kt, v, rb, col)


S5_PAIR_CH = 2 * S5_GC
S5_ROW = S5_BLK * S5_PAIR_CH
S5_ST = 2 * S5_P
S5_TILE = 8
S5_LANE_PAIRS = 4
S5_TT = 8192
S5_NBT = S5_TT // S5_BLK


def _s5_scan(a, sin_ref, nb):
    R = S5_TILE
    ntile = nb // R
    row = lax.broadcasted_iota(jnp.int32, (R, S5_ST), 0)
    zero = jnp.zeros((1, S5_ST), F32)

    def cmul(x, y):
        return x[0] * y[0] - x[1] * y[1], x[0] * y[1] + x[1] * y[0]

    def bcast(x):
        return tuple(jnp.broadcast_to(t, (R, S5_ST)) for t in x)

    def tables(ar, ai, backward):
        pw = {1: (ar, ai)}
        for e in range(2, R + 1):
            pw[e] = cmul(pw[e // 2], pw[e - e // 2])
        steps = []
        for sh in (1, 2, 4):
            keep = (row < R - sh) if backward else (row >= sh)
            steps.append(tuple(jnp.where(keep, t, 0.0) for t in bcast(pw[sh])))
        order = range(R, 0, -1) if backward else range(1, R + 1)
        carry_pw = tuple(jnp.concatenate([pw[e][j] for e in order], axis=0) for j in (0, 1))
        return steps, carry_pw

    def scan_tile(x, carry, steps, cpw, backward):
        for sh, am in zip((1, 2, 4), steps):
            rs = (R - sh) if backward else sh
            x = tuple(p + q for p, q in zip(x, cmul(am, (pltpu.roll(x[0], rs, 0), pltpu.roll(x[1], rs, 0)))))
        cb = bcast(carry)
        x = tuple(p + q for p, q in zip(x, cmul(cpw, cb)))
        edge, rs = (R - 1, R - 1) if backward else (0, 1)
        enter = tuple(jnp.where(row == edge, c, pltpu.roll(t, rs, 0)) for t, c in zip(x, cb))
        last = 0 if backward else R - 1
        return enter, (x[0][last:last + 1], x[1][last:last + 1])

    tabs = []
    for p in range(S5_LANE_PAIRS):
        ap = a[p]
        tabs.append((tables(ap[0:1], ap[1:2], False), tables(ap[2:3], ap[3:4], True)))

    def step(i, carry):
        rf = pl.multiple_of(i * R, R)
        rb = pl.multiple_of((ntile - 1 - i) * R, R)
        out = []
        for p in range(S5_LANE_PAIRS):
            (steps_f, cpw_f), (steps_b, cpw_b) = tabs[p]
            cf, cb = carry[p]
            xf = (sin_ref[p, pl.ds(rf, R), 0:S5_ST], sin_ref[p, pl.ds(rf, R), S5_ST:2 * S5_ST])
            xb = (sin_ref[p, pl.ds(rb, R), 2 * S5_ST:3 * S5_ST], sin_ref[p, pl.ds(rb, R), 3 * S5_ST:4 * S5_ST])
            ef, cf = scan_tile(xf, cf, steps_f, cpw_f, False)
            eb, cb = scan_tile(xb, cb, steps_b, cpw_b, True)
            sin_ref[p, pl.ds(rf, R), 0:S5_ST] = ef[0]
            sin_ref[p, pl.ds(rf, R), S5_ST:2 * S5_ST] = ef[1]
            sin_ref[p, pl.ds(rb, R), 2 * S5_ST:3 * S5_ST] = eb[0]
            sin_ref[p, pl.ds(rb, R), 3 * S5_ST:4 * S5_ST] = eb[1]
            out.append((cf, cb))
        return tuple(out)

    init = tuple(((zero, zero), (zero, zero)) for _ in range(S5_LANE_PAIRS))
    lax.fori_loop(0, ntile, step, init)


def _s5_body(u_ref, m_ref, win_ref, wout_ref, a_ref, y_ref, u2_ref, sin_ref, y2_ref):
    phase = pl.program_id(1)
    t = pl.program_id(2)
    nb = u2_ref.shape[1]
    r0 = pl.multiple_of(t * S5_NBT, S5_NBT)
    rows = pl.ds(r0, S5_NBT)

    n_slot = S5_LANE_PAIRS
    lt = S5_LANE_PAIRS * S5_PAIR_CH
    slot = lax.broadcasted_iota(jnp.int32, (S5_NBT, lt), 1) // S5_PAIR_CH

    def pick(parts, first):
        out = parts[n_slot - 1]
        for i in range(n_slot - 2, -1, -1):
            out = jnp.where(slot == (first + i) % n_slot, parts[i], out)
        return out

    @pl.when(phase == 0)
    def _():
        for q in range(S5_BLK // n_slot):
            rot = []
            for i in range(n_slot):
                tok = u_ref[pl.ds(n_slot * q + i, S5_NBT, stride=S5_BLK), :]
                rot.append(pltpu.roll(tok, i * S5_PAIR_CH, 1) if i else tok)
            for p in range(S5_LANE_PAIRS):
                u2_ref[p, rows, q * lt:(q + 1) * lt] = pick(rot, p).astype(BF16)
        for p in range(S5_LANE_PAIRS):
            sin_ref[p, rows, :] = jnp.dot(u2_ref[p, rows, :], win_ref[p], preferred_element_type=F32)

    @pl.when((phase == 0) & (t == pl.num_programs(2) - 1))
    def _():
        _s5_scan(a_ref[...], sin_ref, nb)

    @pl.when(phase == 1)
    def _():
        for p in range(S5_LANE_PAIRS):
            y2_ref[p] = (jnp.dot(u2_ref[p, rows, :], m_ref[p], preferred_element_type=F32)
                         + jnp.dot(sin_ref[p, rows, :].astype(BF16), wout_ref[p], preferred_element_type=F32))
        for tkn in range(S5_BLK):
            q, i = divmod(tkn, n_slot)
            merged = pick([y2_ref[p, :, q * lt:(q + 1) * lt] for p in range(S5_LANE_PAIRS)], i)
            out = pltpu.roll(merged, ((n_slot - i) % n_slot) * S5_PAIR_CH, 1) if i else merged
            y_ref[pl.ds(tkn, S5_NBT, stride=S5_BLK), :] = out


def _s5(u, m2, win2, wout2, a2):
    L = u.shape[0]
    assert L % S5_TT == 0 and D_S5 == S5_PAIRS * S5_PAIR_CH
    nb = L // S5_BLK
    nq = S5_PAIRS // S5_LANE_PAIRS
    lane_tile = S5_LANE_PAIRS * S5_PAIR_CH
    wspec = pl.BlockSpec((S5_LANE_PAIRS, S5_ROW, S5_ROW), lambda q, ph, t: (q, 0, 0))
    return pl.pallas_call(
        _s5_body,
        grid=(nq, 2, L // S5_TT),
        in_specs=[
            pl.BlockSpec((S5_TT, lane_tile), lambda q, ph, t: (t, q)),
            wspec, wspec, wspec,
            pl.BlockSpec((S5_LANE_PAIRS, 4, S5_ST), lambda q, ph, t: (q, 0, 0)),
        ],
        out_specs=pl.BlockSpec((S5_TT, lane_tile), lambda q, ph, t: (t * ph, q)),
        out_shape=jax.ShapeDtypeStruct((L, D_S5), F32),
        scratch_shapes=[
            pltpu.VMEM((S5_LANE_PAIRS, nb, S5_ROW), BF16),
            pltpu.VMEM((S5_LANE_PAIRS, nb, 4 * S5_ST), F32),
            pltpu.VMEM((S5_LANE_PAIRS, S5_NBT, S5_ROW), F32),
        ],
        compiler_params=pltpu.CompilerParams(
            dimension_semantics=("parallel", "arbitrary", "arbitrary"), vmem_limit_bytes=VMEM_LIMIT_BYTES),
        name="s5",
    )(u, m2, win2, wout2, a2)


S5_EXP = (S5_BLK + 1) * S5_PAIR_CH
S5_EXP_PAD = 640
S5_PWT_ROWS = 24


def _s5_prep_body(pwk_ref, pwt_ref, ct_ref, b2t_ref, d_ref, m_ref, win_ref, wout_ref):
    nt, w = S5_BLK, S5_PAIR_CH
    lane = lax.broadcasted_iota(jnp.int32, (S5_ST, S5_EXP_PAD), 1)
    row = lax.broadcasted_iota(jnp.int32, (S5_ST, S5_EXP_PAD), 0)
    sel_slot = (lane // w == row).astype(BF16)
    sel_chan = ((lane % S5_GC == row) & (row < S5_GC)).astype(BF16)
    same_group = (row // S5_P == (lane // S5_GC) % 2) & (lane < S5_EXP)

    def expand(x, sel):
        return sum(jnp.dot(p, sel, preferred_element_type=F32) for p in _bf16_split3(x))

    def split_dot(a, x):
        a_hi, a_lo, _ = _bf16_split3(a)
        x_hi, x_lo, _ = _bf16_split3(x)
        return (jnp.dot(a_hi, x_hi, preferred_element_type=F32) + jnp.dot(a_hi, x_lo, preferred_element_type=F32)
                + jnp.dot(a_lo, x_hi, preferred_element_type=F32))

    lane_m = lax.broadcasted_iota(jnp.int32, (w, S5_ROW), 1)
    row_m = lax.broadcasted_iota(jnp.int32, (w, S5_ROW), 0)
    krow, wout_rows, win_rows = [], [], []
    for z in range(2):
        pr, pi = expand(pwk_ref[z, 0, 0], sel_slot), expand(pwk_ref[z, 1, 0], sel_slot)
        cr, ci = expand(ct_ref[z, 0, 0], sel_chan), expand(ct_ref[z, 1, 0], sel_chan)
        xr = jnp.where(same_group, pr * cr - pi * ci, 0.0)
        xi = jnp.where(same_group, pr * ci + pi * cr, 0.0)
        lo = w if z == 0 else 0
        wout_rows += [xr[:, lo:lo + S5_ROW], -xi[:, lo:lo + S5_ROW]]
        k_all = split_dot(b2t_ref[z, 0, 0], xr) - split_dot(b2t_ref[z, 1, 0], xi)
        lo = 0 if z == 0 else w
        krow.append(k_all[:, lo:lo + S5_ROW])
        br, bi = b2t_ref[z, 0, 0], b2t_ref[z, 1, 0]
        blocks = []
        for s in range(nt):
            e = nt - 1 - s if z == 0 else s
            qr, qi = pwt_ref[z, 0, 0, e:e + 1, :], pwt_ref[z, 1, 0, e:e + 1, :]
            blocks.append((br * qr - bi * qi, br * qi + bi * qr))
        win_rows.append(blocks)

    kf, kb = krow
    d_diag = jnp.where((lane_m % w) == row_m, d_ref[0], 0.0)
    m_rows = []
    for s in range(nt):
        f = jnp.where(lane_m >= w * s, pltpu.roll(kf, w * s, 1), 0.0) if s else kf
        sh = w * (nt - 1 - s)
        b = jnp.where(lane_m < S5_ROW - sh, pltpu.roll(kb, S5_ROW - sh, 1), 0.0) if sh else kb
        dd = jnp.where(lane_m // w == s, d_diag, 0.0)
        m_rows.append(f + b + dd)

    n_slot = S5_LANE_PAIRS
    lt = n_slot * w

    pp = pl.program_id(1)

    def order_cols(x):
        return jnp.concatenate([pltpu.roll(x[:, q * lt:(q + 1) * lt], pp * w, 1) for q in range(S5_ROW // lt)], axis=1)

    for k, x in enumerate(wout_rows):
        wout_ref[0, k * S5_ST:(k + 1) * S5_ST, :] = order_cols(x).astype(BF16)
    for s in range(nt):
        r0 = pl.multiple_of((n_slot * (s // n_slot) + (s % n_slot + pp) % n_slot) * w, w)
        m_ref[0, pl.ds(r0, w), :] = order_cols(m_rows[s]).astype(BF16)
        for z in range(2):
            wr, wi = win_rows[z][s]
            win_ref[0, pl.ds(r0, w), (2 * z) * S5_ST:(2 * z + 1) * S5_ST] = wr.astype(BF16)
            win_ref[0, pl.ds(r0, w), (2 * z + 1) * S5_ST:(2 * z + 2) * S5_ST] = wi.astype(BF16)


def _s5_prep(pwk, pwt, ct, b2t, d2):
    npair = d2.shape[0]
    n = S5_LANE_PAIRS
    spec = lambda r, c: pl.BlockSpec((2, 2, 1, r, c), lambda g, p: (0, 0, g * n + p, 0, 0))
    out = pl.BlockSpec((1, S5_ROW, S5_ROW), lambda g, p: (g * n + p, 0, 0))
    shape = jax.ShapeDtypeStruct((npair, S5_ROW, S5_ROW), BF16)
    return pl.pallas_call(
        _s5_prep_body,
        grid=(npair // n, n),
        in_specs=[spec(S5_ST, S5_ST), spec(S5_PWT_ROWS, S5_ST), spec(S5_ST, S5_ST), spec(S5_PAIR_CH, S5_ST),
                  pl.BlockSpec((1, 1, S5_ROW), lambda g, p: (g * n + p, 0, 0))],
        out_specs=[out, out, out],
        out_shape=[shape, shape, shape],
        compiler_params=pltpu.CompilerParams(
            dimension_semantics=("parallel", "arbitrary"), vmem_limit_bytes=VMEM_LIMIT_BYTES),
        name="s5_prep",
    )(pwk, pwt, ct, b2t, d2)


def _s5_weights(a_re, a_im, log_dt, b_re, b_im, c_re, c_im, d_skip):
    nt, npair = S5_BLK, S5_PAIRS
    lam = lax.complex(a_re, a_im)
    dt = jnp.exp(log_dt)[..., None]
    lam_bar = jnp.exp(lam * dt)
    b_bar = ((lam_bar - 1.0) / lam)[..., None] * lax.complex(b_re, b_im)
    taus = jnp.arange(nt + 1, dtype=F32)
    pw = jnp.exp((lam * dt)[..., None] * taus).reshape(2, npair, S5_ST, nt + 1)
    ri = lambda x: jnp.stack([jnp.real(x), jnp.imag(x)], axis=1)
    pwk = jnp.stack([pw[0], pw[1, ..., ::-1]])
    pwk = jnp.pad(ri(pwk), ((0, 0), (0, 0), (0, 0), (0, 0), (0, S5_ST - (nt + 1))))
    pwt = jnp.pad(ri(pw).transpose(0, 1, 2, 4, 3), ((0, 0), (0, 0), (0, 0), (0, S5_PWT_ROWS - (nt + 1)), (0, 0)))
    ct = lax.complex(c_re, c_im).transpose(0, 1, 3, 2).reshape(2, npair, S5_ST, S5_GC)
    ct = jnp.pad(ri(ct), ((0, 0), (0, 0), (0, 0), (0, 0), (0, S5_ST - S5_GC)))
    bb = b_bar.reshape(2, npair, 2, S5_P, S5_GC)
    eye2 = jnp.eye(2, dtype=F32)
    b2t = (bb.transpose(0, 1, 2, 4, 3)[:, :, :, :, None, :] * eye2[None, None, :, None, :, None])
    b2t = ri(b2t.reshape(2, npair, S5_PAIR_CH, S5_ST))
    d2 = jnp.tile(d_skip.reshape(npair, 1, S5_PAIR_CH), (1, 1, nt))
    m2, win2, wout2 = _s5_prep(pwk, pwt, ct, b2t, d2)
    a_blk = pw[..., nt]
    a2 = jnp.stack([jnp.real(a_blk[0]), jnp.imag(a_blk[0]), jnp.real(a_blk[1]), jnp.imag(a_blk[1])], axis=1)
    return m2, win2, wout2, a2


def _mix_body(x_ref, hf_ref, hb_ref, o_ref, y_ref, nw_ref, wglu_ref, wout_ref, out_ref):
    h = hf_ref[...] + hb_ref[...]
    parts = []
    for hd in range(N_HEADS):
        hh = h[:, hd * HEAD_DIM:(hd + 1) * HEAD_DIM]
        mu = jnp.mean(hh, axis=-1, keepdims=True)
        var = jnp.mean(jnp.square(hh - mu), axis=-1, keepdims=True)
        parts.append((hh - mu) * lax.rsqrt(var + EPS))
    hn = jnp.concatenate(parts, axis=1)
    h_m = hn * nw_ref[...] * _sigmoid(o_ref[...])
    y = y_ref[...]
    gelu = 0.5 * y * (1.0 + jnp.tanh(math.sqrt(2.0 / math.pi) * (y + 0.044715 * (y * y * y))))
    ab = jnp.dot(gelu.astype(BF16), wglu_ref[...], preferred_element_type=F32)
    h_s = ab[:, :D_S5] * _sigmoid(ab[:, D_S5:])
    mix = jnp.dot(h_m.astype(BF16), wout_ref[0:D_MLSTM, :], preferred_element_type=F32)
    mix += jnp.dot(h_s.astype(BF16), wout_ref[D_MLSTM:, :], preferred_element_type=F32)
    out_ref[...] = x_ref[...] + mix


def _mix(x, hf, hb, o_in, y, norm_w, w_glu, w_out):
    L = x.shape[0]
    tm = MIX_TM
    assert L % tm == 0
    whole = pl.BlockSpec(memory_space=pltpu.VMEM)
    row = lambda n: pl.BlockSpec((tm, n), lambda i: (i, 0))
    return pl.pallas_call(
        _mix_body,
        grid=(L // tm,),
        in_specs=[row(D_MODEL), row(D_MLSTM), row(D_MLSTM), row(D_MLSTM), row(D_S5), whole, whole, whole],
        out_specs=row(D_MODEL),
        out_shape=jax.ShapeDtypeStruct((L, D_MODEL), F32),
        compiler_params=pltpu.CompilerParams(
            dimension_semantics=("parallel",), vmem_limit_bytes=VMEM_LIMIT_BYTES),
        name="mix",
    )(x, hf, hb, o_in, y, norm_w, w_glu, w_out)


def _encode(x, p):
    x = _ffn(x, p["norm_ffn1"], p["ffn1_w_gate"], p["ffn1_w_up"], p["ffn1_w_down"], p["norm_final"],
             final_norm=False)
    q, kt, v, o_in, u, rb, col = _in_proj(x, p["norm_mix"], p["w_all"], p["w_u"], p["gate_bias_r"],
                                          p["conv_w"], p["conv_b"])
    hf, hb = _mlstm(q, kt, v, rb, col)
    y = _s5(u, p["s5_m"], p["s5_win"], p["s5_wout"], p["s5_a"])
    x = _mix(x, hf, hb, o_in, y, p["mlstm_norm_w"], p["s5_w_glu"], p["w_out"])
    return _ffn(x, p["norm_ffn2"], p["ffn2_w_gate"], p["ffn2_w_up"], p["ffn2_w_down"], p["norm_final"],
                final_norm=True)


def _prepare(norm_ffn1, ffn1_w_gate, ffn1_w_up, ffn1_w_down, norm_mix, w_in, conv_w, conv_b, b_igate, b_fgate,
             mlstm_norm_w, s5_a_re, s5_a_im, s5_log_dt, s5_b_re, s5_b_im, s5_c_re, s5_c_im, s5_d, s5_w_glu,
             w_out, norm_ffn2, ffn2_w_gate, ffn2_w_up, ffn2_w_down, norm_final):
    l = 0
    w = w_in[l]
    g0 = 4 * D_MLSTM
    gate_bias = jnp.concatenate([b_igate[l].reshape(-1), b_fgate[l].reshape(-1)])
    m2, win2, wout2, a2 = _s5_weights(s5_a_re[l], s5_a_im[l], s5_log_dt[l], s5_b_re[l], s5_b_im[l],
                                      s5_c_re[l], s5_c_im[l], s5_d[l])
    row = lambda a: a.reshape(1, -1).astype(F32)
    return {
        "norm_ffn1": row(norm_ffn1[l]), "norm_ffn2": row(norm_ffn2[l]), "norm_final": row(norm_final),
        "ffn1_w_gate": ffn1_w_gate[l].astype(BF16), "ffn1_w_up": ffn1_w_up[l].astype(BF16),
        "ffn1_w_down": ffn1_w_down[l].astype(BF16),
        "ffn2_w_gate": ffn2_w_gate[l].astype(BF16), "ffn2_w_up": ffn2_w_up[l].astype(BF16),
        "ffn2_w_down": ffn2_w_down[l].astype(BF16),
        "norm_mix": row(norm_mix[l]),
        "w_all": w.astype(BF16),
        "w_u": w[:, g0 + N_GATES:].astype(BF16),
        "conv_w": conv_w[l].astype(F32), "conv_b": row(conv_b[l]),
        "gate_bias_r": gate_bias.reshape(N_GATES, 1),
        "mlstm_norm_w": row(mlstm_norm_w[l]),
        "s5_m": m2, "s5_win": win2, "s5_wout": wout2, "s5_a": a2,
        "s5_w_glu": s5_w_glu[l].astype(BF16), "w_out": w_out[l].astype(BF16),
    }


def kernel(x_prompt, x_sample, norm_ffn1, ffn1_w_gate, ffn1_w_up, ffn1_w_down, norm_mix, w_in, conv_w, conv_b, b_igate, b_fgate, mlstm_norm_w, s5_a_re, s5_a_im, s5_log_dt, s5_b_re, s5_b_im, s5_c_re, s5_c_im, s5_d, s5_w_glu, w_out, norm_ffn2, ffn2_w_gate, ffn2_w_up, ffn2_w_down, norm_final):
    assert norm_ffn1.shape[0] == 1 and x_prompt.shape[0] == 1 and x_sample.shape[0] == 1
    p = _prepare(norm_ffn1, ffn1_w_gate, ffn1_w_up, ffn1_w_down, norm_mix, w_in, conv_w, conv_b, b_igate,
                 b_fgate, mlstm_norm_w, s5_a_re, s5_a_im, s5_log_dt, s5_b_re, s5_b_im, s5_c_re, s5_c_im, s5_d,
                 s5_w_glu, w_out, norm_ffn2, ffn2_w_gate, ffn2_w_up, ffn2_w_down, norm_final)
    y_prompt = _encode(x_prompt[0], p)[None]
    y_sample = _encode(x_sample[0], p)[None]
    return (y_prompt, y_sample)
```

```python
import functools
import math

import jax
import jax.numpy as jnp
from jax import lax
from jax.experimental import pallas as pl
from jax.experimental.pallas import tpu as pltpu

F32 = jnp.float32
BF16 = jnp.bfloat16

D_MODEL = 2048
D_MLSTM = 1024
D_S5 = 1024
N_HEADS = 8
HEAD_DIM = 128
CHUNK = 128
MLSTM_SUB = 4
N_GATES = 32
GATE_COL_LANES = 128
S5_GROUPS = 64
S5_GC = 16
S5_P = 64
S5_BLK = 16
S5_PAIRS = S5_GROUPS // 2
D_FF = 5632
EPS = 1e-6
M_INIT = -1e30

VMEM_LIMIT_BYTES = 56 * 1024 * 1024

FFN_TM = 1024
FFN_TF = 512
PROJ_TM = 256
PROJ_HALO = 8
MIX_TM = 512


def _sigmoid(x):
    return 1.0 / (1.0 + jnp.exp(-x))


def _rmsnorm(x, w):
    return x * lax.rsqrt(jnp.mean(x * x, axis=-1, keepdims=True) + EPS) * w


def _ffn_body(x_ref, nw_ref, wg_ref, wu_ref, wd_ref, nf_ref, o_ref, xn_ref, *, final_norm):
    j = pl.program_id(1)

    @pl.when(j == 0)
    def _():
        x = x_ref[...]
        xn_ref[...] = _rmsnorm(x, nw_ref[...]).astype(BF16)
        o_ref[...] = x

    xn = xn_ref[...]
    g = jnp.dot(xn, wg_ref[...], preferred_element_type=F32)
    u = jnp.dot(xn, wu_ref[...], preferred_element_type=F32)
    h = (0.5 * g * _sigmoid(g)) * u
    o_ref[...] += jnp.dot(h.astype(BF16), wd_ref[...], preferred_element_type=F32)

    if final_norm:
        @pl.when(j == pl.num_programs(1) - 1)
        def _():
            o_ref[...] = _rmsnorm(o_ref[...], nf_ref[...])


def _ffn(x, norm_w, w_gate, w_up, w_down, norm_final, *, final_norm):
    L = x.shape[0]
    assert L % FFN_TM == 0 and D_FF % FFN_TF == 0
    return pl.pallas_call(
        functools.partial(_ffn_body, final_norm=final_norm),
        grid=(L // FFN_TM, D_FF // FFN_TF),
        in_specs=[
            pl.BlockSpec((FFN_TM, D_MODEL), lambda i, j: (i, 0)),
            pl.BlockSpec((1, D_MODEL), lambda i, j: (0, 0)),
            pl.BlockSpec((D_MODEL, FFN_TF), lambda i, j: (0, j)),
            pl.BlockSpec((D_MODEL, FFN_TF), lambda i, j: (0, j)),
            pl.BlockSpec((FFN_TF, D_MODEL), lambda i, j: (j, 0)),
            pl.BlockSpec((1, D_MODEL), lambda i, j: (0, 0)),
        ],
        out_specs=pl.BlockSpec((FFN_TM, D_MODEL), lambda i, j: (i, 0)),
        out_shape=jax.ShapeDtypeStruct((L, D_MODEL), F32),
        scratch_shapes=[pltpu.VMEM((FFN_TM, D_MODEL), BF16)],
        compiler_params=pltpu.CompilerParams(
            dimension_semantics=("parallel", "arbitrary"), vmem_limit_bytes=VMEM_LIMIT_BYTES),
        name="ffn_final" if final_norm else "ffn",
    )(x, norm_w, w_gate, w_up, w_down, norm_final)


QK_COLS = 2 * D_MLSTM
QK_CB = 512


def _chunk_scan(x, op, identity, backward):
    n = x.shape[-1]
    pos = lax.broadcasted_iota(jnp.int32, x.shape, x.ndim - 1) % CHUNK
    sh = 1
    while sh < CHUNK:
        if backward:
            x = op(x, jnp.where(pos < CHUNK - sh, pltpu.roll(x, n - sh, x.ndim - 1), identity))
        else:
            x = op(x, jnp.where(pos >= sh, pltpu.roll(x, sh, x.ndim - 1), identity))
        sh *= 2
    return x


def _in_proj_body(xp_ref, x_ref, xnx_ref, nw_ref, w_ref, wu_ref, gb_ref,
                  cw_ref, cb_ref, q_ref, kt_ref, v_ref, o_ref, u_ref, rb_ref, col_ref):
    i = pl.program_id(0)
    tm = x_ref.shape[0]
    nw = nw_ref[...]
    xn = _rmsnorm(x_ref[...], nw)
    xn_prev = jnp.where(i == 0, 0.0, _rmsnorm(xp_ref[...], nw))
    xn_next = jnp.where(i == pl.num_programs(0) - 1, 0.0, _rmsnorm(xnx_ref[...], nw))
    xn_b = xn.astype(BF16)
    xe_b = jnp.concatenate([xn_prev, xn, xn_next], axis=0).astype(BF16)
    rows = tm + 2 * PROJ_HALO

    w_gates = w_ref[:, 2 * QK_COLS:2 * QK_COLS + N_GATES]
    g = lax.dot_general(w_gates, xn_b, (((0,), (1,)), ((), ())), preferred_element_type=F32) + gb_ref[...]
    H = N_HEADS
    f_pre = g[2 * H:]
    lf = jnp.minimum(f_pre, 0.0) - jnp.log1p(jnp.exp(-jnp.abs(f_pre)))
    b_f = _chunk_scan(lf[:H], jnp.add, 0.0, False)
    b_b = _chunk_scan(lf[H:], jnp.add, 0.0, True)
    r_f = g[:H] - b_f
    r_b = g[H:2 * H] - b_b
    cm_f = _chunk_scan(r_f, jnp.maximum, -jnp.inf, False)
    cm_b = _chunk_scan(r_b, jnp.maximum, -jnp.inf, True)

    for c in range(QK_COLS // QK_CB):
        cs = slice(c * QK_CB, (c + 1) * QK_CB)
        z = jnp.dot(xe_b, w_ref[:, cs], preferred_element_type=F32)
        z_m1 = pltpu.roll(z, 1, 0)[PROJ_HALO:PROJ_HALO + tm]
        z_0 = z[PROJ_HALO:PROJ_HALO + tm]
        z_p1 = pltpu.roll(z, rows - 1, 0)[PROJ_HALO:PROJ_HALO + tm]
        y = z_m1 * cw_ref[0:1, cs] + z_0 * cw_ref[1:2, cs] + z_p1 * cw_ref[2:3, cs] + cb_ref[:, cs]
        y = y * _sigmoid(y)
        if c * QK_CB < D_MLSTM:
            q_ref[:, cs] = (y * (HEAD_DIM ** -0.5)).astype(q_ref.dtype)
        else:
            kt_ref[c * QK_CB - D_MLSTM:(c + 1) * QK_CB - D_MLSTM, :] = y.T.astype(kt_ref.dtype)

    v_ref[...] = jnp.dot(xn_b, w_ref[:, QK_COLS:QK_COLS + D_MLSTM], preferred_element_type=F32).astype(v_ref.dtype)
    o_ref[...] = jnp.dot(xn_b, w_ref[:, QK_COLS + D_MLSTM:QK_COLS + 2 * D_MLSTM], preferred_element_type=F32)
    u_ref[...] = jnp.dot(xn_b, wu_ref[...], preferred_element_type=F32)
    rb_ref[...] = jnp.concatenate([r_f, r_b, b_f, b_b], axis=0)
    pad = jnp.zeros((GATE_COL_LANES - 4 * H, tm), F32)
    col_ref[...] = jnp.concatenate([b_f, b_b, cm_f, cm_b, pad], axis=0).T


def _in_proj(x, norm_w, w_all, w_u, gate_bias, conv_w, conv_b):
    L = x.shape[0]
    tm = PROJ_TM
    assert L % tm == 0 and tm % CHUNK == 0
    hb = tm // PROJ_HALO
    nblk8 = L // PROJ_HALO
    whole = pl.BlockSpec(memory_space=pltpu.VMEM)
    return pl.pallas_call(
        _in_proj_body,
        grid=(L // tm,),
        in_specs=[
            pl.BlockSpec((PROJ_HALO, D_MODEL), lambda i: (jnp.maximum(i * hb - 1, 0), 0)),
            pl.BlockSpec((tm, D_MODEL), lambda i: (i, 0)),
            pl.BlockSpec((PROJ_HALO, D_MODEL), lambda i: (jnp.minimum((i + 1) * hb, nblk8 - 1), 0)),
            whole, whole, whole, whole, whole, whole,
        ],
        out_specs=[
            pl.BlockSpec((tm, D_MLSTM), lambda i: (i, 0)),
            pl.BlockSpec((D_MLSTM, tm), lambda i: (0, i)),
            pl.BlockSpec((tm, D_MLSTM), lambda i: (i, 0)),
            pl.BlockSpec((tm, D_MLSTM), lambda i: (i, 0)),
            pl.BlockSpec((tm, D_S5), lambda i: (i, 0)),
            pl.BlockSpec((N_GATES, tm), lambda i: (0, i)),
            pl.BlockSpec((tm, GATE_COL_LANES), lambda i: (i, 0)),
        ],
        out_shape=[
            jax.ShapeDtypeStruct((L, D_MLSTM), BF16),
            jax.ShapeDtypeStruct((D_MLSTM, L), BF16),
            jax.ShapeDtypeStruct((L, D_MLSTM), BF16),
            jax.ShapeDtypeStruct((L, D_MLSTM), F32),
            jax.ShapeDtypeStruct((L, D_S5), F32),
            jax.ShapeDtypeStruct((N_GATES, L), F32),
            jax.ShapeDtypeStruct((L, GATE_COL_LANES), F32),
        ],
        compiler_params=pltpu.CompilerParams(
            dimension_semantics=("parallel",), vmem_limit_bytes=VMEM_LIMIT_BYTES),
        name="in_proj",
    )(x, x, x, norm_w, w_all, w_u, gate_bias, conv_w, conv_b)


def _bf16_split3(x):
    hi = x.astype(BF16)
    r1 = x - hi.astype(F32)
    mid = r1.astype(BF16)
    lo = (r1 - mid.astype(F32)).astype(BF16)
    return hi, mid, lo


def _mlstm_direction(q_ref, kt_ref, v_ref, rb_ref, col_ref, h_ref, ct_ref, m_ref, ml_ref, d, sub):
    T = CHUNK
    H = N_HEADS
    backward = d == 1
    tt = slice(sub * T, (sub + 1) * T)
    rr = lax.broadcasted_iota(jnp.int32, (T, T), 0)
    cc = lax.broadcasted_iota(jnp.int32, (T, T), 1)
    mask = (cc >= rr) if backward else (cc <= rr)
    last = 0 if backward else T - 1

    r_rows = rb_ref[d * H:(d + 1) * H, tt]
    b_rows = rb_ref[(2 + d) * H:(3 + d) * H, tt]
    b_cols = col_ref[tt, d * H:(d + 1) * H]
    cm_cols = col_ref[tt, (2 + d) * H:(3 + d) * H]
    m_prev = m_ref[d]
    m_prev_l = ml_ref[d]
    m_cols = jnp.maximum(cm_cols, m_prev_l)
    clamp_cols = -(b_cols + m_cols)
    m_last = jnp.maximum(jnp.max(r_rows, axis=1, keepdims=True), m_prev)
    b_tot = jnp.broadcast_to(b_rows[:, last:last + 1], (H, T))
    sc_rows = jnp.exp(m_prev - m_last)
    wkk_rows = jnp.exp(r_rows - m_last)
    ones = jnp.ones((T, HEAD_DIM), BF16)

    for h in range(H):
        ci = d * H + h
        hs = slice(h * HEAD_DIM, (h + 1) * HEAD_DIM)
        q = q_ref[tt, hs]
        kt = kt_ref[hs, tt]
        vaug = jnp.concatenate([v_ref[tt, hs], ones], axis=1)
        ct_prev = ct_ref[ci]
        m_col = jnp.broadcast_to(m_cols[:, h:h + 1], (T, T))
        w = jnp.exp(jnp.where(mask, r_rows[h:h + 1, :] - m_col, -jnp.inf))
        sq = jnp.dot(q, jnp.concatenate([kt, ct_prev.astype(BF16)], axis=1), preferred_element_type=F32)
        s = sq[:, :T] * w
        qc = sq[:, T:]
        kw = (kt.astype(F32) * wkk_rows[h:h + 1, :]).astype(BF16)
        both = jnp.dot(jnp.concatenate([s.astype(BF16), kw], axis=0), vaug, preferred_element_type=F32)
        sv = both[:T]
        upd = both[T:]
        s_inter = jnp.exp(m_prev[h:h + 1, :] - m_col)
        num = sv[:, :HEAD_DIM] + s_inter * qc[:, :HEAD_DIM]
        den = sv[:, HEAD_DIM:] + s_inter * qc[:, HEAD_DIM:]
        floor = jnp.exp(jnp.broadcast_to(clamp_cols[:, h:h + 1], (T, HEAD_DIM)))
        h_ref[tt, hs] = num / jnp.maximum(jnp.abs(den), floor)

        sc = sc_rows[h:h + 1, :]
        ct_ref[ci] = jnp.concatenate([sc, sc], axis=1) * ct_prev + upd

    m_ref[d] = b_tot + m_last
    ml_ref[d] = b_cols[last:last + 1, :] + m_cols[last:last + 1, :]


def _mlstm_body(qf_ref, ktf_ref, vf_ref, rbf_ref, colf_ref, qb_ref, ktb_ref, vb_ref, rbb_ref, colb_ref,
                hf_ref, hb_ref, ct_ref, m_ref, ml_ref):
    @pl.when(pl.program_id(0) == 0)
    def _():
        ct_ref[...] = jnp.zeros_like(ct_ref)
        m_ref[...] = jnp.full_like(m_ref, M_INIT)
        ml_ref[...] = jnp.full_like(ml_ref, M_INIT)

    for sub in range(MLSTM_SUB):
        _mlstm_direction(qf_ref, ktf_ref, vf_ref, rbf_ref, colf_ref, hf_ref, ct_ref, m_ref, ml_ref, 0, sub)
        _mlstm_direction(qb_ref, ktb_ref, vb_ref, rbb_ref, colb_ref, hb_ref, ct_ref, m_ref, ml_ref, 1,
                         MLSTM_SUB - 1 - sub)


def _mlstm(q, kt, v, rb, col):
    L = q.shape[0]
    T = MLSTM_SUB * CHUNK
    assert L % T == 0 and CHUNK == HEAD_DIM
    nc = L // T
    fwd = lambda c: (c, 0)
    bwd = lambda c: (nc - 1 - c, 0)
    fwd_r = lambda c: (0, c)
    bwd_r = lambda c: (0, nc - 1 - c)
    return pl.pallas_call(
        _mlstm_body,
        grid=(nc,),
        in_specs=[
            pl.BlockSpec((T, D_MLSTM), fwd), pl.BlockSpec((D_MLSTM, T), fwd_r), pl.BlockSpec((T, D_MLSTM), fwd),
            pl.BlockSpec((N_GATES, T), fwd_r), pl.BlockSpec((T, GATE_COL_LANES), fwd),
            pl.BlockSpec((T, D_MLSTM), bwd), pl.BlockSpec((D_MLSTM, T), bwd_r), pl.BlockSpec((T, D_MLSTM), bwd),
            pl.BlockSpec((N_GATES, T), bwd_r), pl.BlockSpec((T, GATE_COL_LANES), bwd),
        ],
        out_specs=[pl.BlockSpec((T, D_MLSTM), fwd), pl.BlockSpec((T, D_MLSTM), bwd)],
        out_shape=[jax.ShapeDtypeStruct((L, D_MLSTM), F32), jax.ShapeDtypeStruct((L, D_MLSTM), F32)],
        scratch_shapes=[
            pltpu.VMEM((2 * N_HEADS, HEAD_DIM, 2 * HEAD_DIM), F32),
            pltpu.VMEM((2, N_HEADS, CHUNK), F32),
            pltpu.VMEM((2, 1, N_HEADS), F32),
        ],
        compiler_params=pltpu.CompilerParams(
            dimension_semantics=("arbitrary",), vmem_limit_bytes=VMEM_LIMIT_BYTES),
        name="mlstm",
    )(q, kt, v, rb, col, q, kt, v, rb, col)


S5_PAIR_CH = 2 * S5_GC
S5_ROW = S5_BLK * S5_PAIR_CH
S5_ST = 2 * S5_P
S5_TILE = 8
S5_LANE_PAIRS = 4
S5_TT = 8192
S5_NBT = S5_TT // S5_BLK


def _s5_scan(a, sin_ref, nb):
    R = S5_TILE
    ntile = nb // R
    row = lax.broadcasted_iota(jnp.int32, (R, S5_ST), 0)
    zero = jnp.zeros((1, S5_ST), F32)

    def cmul(x, y):
        return x[0] * y[0] - x[1] * y[1], x[0] * y[1] + x[1] * y[0]

    def bcast(x):
        return tuple(jnp.broadcast_to(t, (R, S5_ST)) for t in x)

    def tables(ar, ai, backward):
        pw = {1: (ar, ai)}
        for e in range(2, R + 1):
            pw[e] = cmul(pw[e // 2], pw[e - e // 2])
        steps = []
        for sh in (1, 2, 4):
            keep = (row < R - sh) if backward else (row >= sh)
            steps.append(tuple(jnp.where(keep, t, 0.0) for t in bcast(pw[sh])))
        order = range(R, 0, -1) if backward else range(1, R + 1)
        carry_pw = tuple(jnp.concatenate([pw[e][j] for e in order], axis=0) for j in (0, 1))
        return steps, carry_pw

    def scan_tile(x, carry, steps, cpw, backward):
        for sh, am in zip((1, 2, 4), steps):
            rs = (R - sh) if backward else sh
            x = tuple(p + q for p, q in zip(x, cmul(am, (pltpu.roll(x[0], rs, 0), pltpu.roll(x[1], rs, 0)))))
        cb = bcast(carry)
        x = tuple(p + q for p, q in zip(x, cmul(cpw, cb)))
        edge, rs = (R - 1, R - 1) if backward else (0, 1)
        enter = tuple(jnp.where(row == edge, c, pltpu.roll(t, rs, 0)) for t, c in zip(x, cb))
        last = 0 if backward else R - 1
        return enter, (x[0][last:last + 1], x[1][last:last + 1])

    for p in range(S5_LANE_PAIRS):
        ap = a[p]
        (steps_f, cpw_f), (steps_b, cpw_b) = tables(ap[0:1], ap[1:2], False), tables(ap[2:3], ap[3:4], True)

        def step(i, carry, p=p, steps_f=steps_f, cpw_f=cpw_f, steps_b=steps_b, cpw_b=cpw_b):
            cf, cb = carry
            rf = pl.multiple_of(i * R, R)
            rb = pl.multiple_of((ntile - 1 - i) * R, R)
            xf = (sin_ref[p, pl.ds(rf, R), 0:S5_ST], sin_ref[p, pl.ds(rf, R), S5_ST:2 * S5_ST])
            xb = (sin_ref[p, pl.ds(rb, R), 2 * S5_ST:3 * S5_ST], sin_ref[p, pl.ds(rb, R), 3 * S5_ST:4 * S5_ST])
            ef, cf = scan_tile(xf, cf, steps_f, cpw_f, False)
            eb, cb = scan_tile(xb, cb, steps_b, cpw_b, True)
            sin_ref[p, pl.ds(rf, R), 0:S5_ST] = ef[0]
            sin_ref[p, pl.ds(rf, R), S5_ST:2 * S5_ST] = ef[1]
            sin_ref[p, pl.ds(rb, R), 2 * S5_ST:3 * S5_ST] = eb[0]
            sin_ref[p, pl.ds(rb, R), 3 * S5_ST:4 * S5_ST] = eb[1]
            return cf, cb

        lax.fori_loop(0, ntile, step, ((zero, zero), (zero, zero)))


def _s5_body(u_ref, m_ref, win_ref, wout_ref, a_ref, y_ref, u2_ref, sin_ref, y2_ref):
    phase = pl.program_id(1)
    t = pl.program_id(2)
    nb = u2_ref.shape[1]
    r0 = pl.multiple_of(t * S5_NBT, S5_NBT)
    rows = pl.ds(r0, S5_NBT)

    n_slot = S5_LANE_PAIRS
    lt = S5_LANE_PAIRS * S5_PAIR_CH
    slot = lax.broadcasted_iota(jnp.int32, (S5_NBT, lt), 1) // S5_PAIR_CH

    def pick(parts, first):
        out = parts[n_slot - 1]
        for i in range(n_slot - 2, -1, -1):
            out = jnp.where(slot == (first + i) % n_slot, parts[i], out)
        return out

    @pl.when(phase == 0)
    def _():
        for q in range(S5_BLK // n_slot):
            rot = []
            for i in range(n_slot):
                tok = u_ref[pl.ds(n_slot * q + i, S5_NBT, stride=S5_BLK), :]
                rot.append(pltpu.roll(tok, i * S5_PAIR_CH, 1) if i else tok)
            for p in range(S5_LANE_PAIRS):
                u2_ref[p, rows, q * lt:(q + 1) * lt] = pick(rot, p).astype(BF16)
        for p in range(S5_LANE_PAIRS):
            sin_ref[p, rows, :] = jnp.dot(u2_ref[p, rows, :], win_ref[p], preferred_element_type=F32)

    @pl.when((phase == 0) & (t == pl.num_programs(2) - 1))
    def _():
        _s5_scan(a_ref[...], sin_ref, nb)

    @pl.when(phase == 1)
    def _():
        for p in range(S5_LANE_PAIRS):
            y2_ref[p] = (jnp.dot(u2_ref[p, rows, :], m_ref[p], preferred_element_type=F32)
                         + jnp.dot(sin_ref[p, rows, :].astype(BF16), wout_ref[p], preferred_element_type=F32))
        for tkn in range(S5_BLK):
            q, i = divmod(tkn, n_slot)
            merged = pick([y2_ref[p, :, q * lt:(q + 1) * lt] for p in range(S5_LANE_PAIRS)], i)
            out = pltpu.roll(merged, ((n_slot - i) % n_slot) * S5_PAIR_CH, 1) if i else merged
            y_ref[pl.ds(tkn, S5_NBT, stride=S5_BLK), :] = out


def _s5(u, m2, win2, wout2, a2):
    L = u.shape[0]
    assert L % S5_TT == 0 and D_S5 == S5_PAIRS * S5_PAIR_CH
    nb = L // S5_BLK
    nq = S5_PAIRS // S5_LANE_PAIRS
    lane_tile = S5_LANE_PAIRS * S5_PAIR_CH
    wspec = pl.BlockSpec((S5_LANE_PAIRS, S5_ROW, S5_ROW), lambda q, ph, t: (q, 0, 0))
    return pl.pallas_call(
        _s5_body,
        grid=(nq, 2, L // S5_TT),
        in_specs=[
            pl.BlockSpec((S5_TT, lane_tile), lambda q, ph, t: (t, q)),
            wspec, wspec, wspec,
            pl.BlockSpec((S5_LANE_PAIRS, 4, S5_ST), lambda q, ph, t: (q, 0, 0)),
        ],
        out_specs=pl.BlockSpec((S5_TT, lane_tile), lambda q, ph, t: (t * ph, q)),
        out_shape=jax.ShapeDtypeStruct((L, D_S5), F32),
        scratch_shapes=[
            pltpu.VMEM((S5_LANE_PAIRS, nb, S5_ROW), BF16),
            pltpu.VMEM((S5_LANE_PAIRS, nb, 4 * S5_ST), F32),
            pltpu.VMEM((S5_LANE_PAIRS, S5_NBT, S5_ROW), F32),
        ],
        compiler_params=pltpu.CompilerParams(
            dimension_semantics=("parallel", "arbitrary", "arbitrary"), vmem_limit_bytes=VMEM_LIMIT_BYTES),
        name="s5",
    )(u, m2, win2, wout2, a2)


S5_EXP = (S5_BLK + 1) * S5_PAIR_CH
S5_EXP_PAD = 640
S5_PWT_ROWS = 24


def _s5_prep_body(pwk_ref, pwt_ref, ct_ref, b2t_ref, d_ref, m_ref, win_ref, wout_ref):
    nt, w = S5_BLK, S5_PAIR_CH
    lane = lax.broadcasted_iota(jnp.int32, (S5_ST, S5_EXP_PAD), 1)
    row = lax.broadcasted_iota(jnp.int32, (S5_ST, S5_EXP_PAD), 0)
    sel_slot = (lane // w == row).astype(BF16)
    sel_chan = ((lane % S5_GC == row) & (row < S5_GC)).astype(BF16)
    same_group = (row // S5_P == (lane // S5_GC) % 2) & (lane < S5_EXP)

    def expand(x, sel):
        return sum(jnp.dot(p, sel, preferred_element_type=F32) for p in _bf16_split3(x))

    def split_dot(a, x):
        a_hi, a_lo, _ = _bf16_split3(a)
        x_hi, x_lo, _ = _bf16_split3(x)
        return (jnp.dot(a_hi, x_hi, preferred_element_type=F32) + jnp.dot(a_hi, x_lo, preferred_element_type=F32)
                + jnp.dot(a_lo, x_hi, preferred_element_type=F32))

    lane_m = lax.broadcasted_iota(jnp.int32, (w, S5_ROW), 1)
    row_m = lax.broadcasted_iota(jnp.int32, (w, S5_ROW), 0)
    krow, wout_rows, win_rows = [], [], []
    for z in range(2):
        pr, pi = expand(pwk_ref[z, 0, 0], sel_slot), expand(pwk_ref[z, 1, 0], sel_slot)
        cr, ci = expand(ct_ref[z, 0, 0], sel_chan), expand(ct_ref[z, 1, 0], sel_chan)
        xr = jnp.where(same_group, pr * cr - pi * ci, 0.0)
        xi = jnp.where(same_group, pr * ci + pi * cr, 0.0)
        lo = w if z == 0 else 0
        wout_rows += [xr[:, lo:lo + S5_ROW], -xi[:, lo:lo + S5_ROW]]
        k_all = split_dot(b2t_ref[z, 0, 0], xr) - split_dot(b2t_ref[z, 1, 0], xi)
        lo = 0 if z == 0 else w
        krow.append(k_all[:, lo:lo + S5_ROW])
        br, bi = b2t_ref[z, 0, 0], b2t_ref[z, 1, 0]
        blocks = []
        for s in range(nt):
            e = nt - 1 - s if z == 0 else s
            qr, qi = pwt_ref[z, 0, 0, e:e + 1, :], pwt_ref[z, 1, 0, e:e + 1, :]
            blocks.append((br * qr - bi * qi, br * qi + bi * qr))
        win_rows.append(blocks)

    kf, kb = krow
    d_diag = jnp.where((lane_m % w) == row_m, d_ref[0], 0.0)
    m_rows = []
    for s in range(nt):
        f = jnp.where(lane_m >= w * s, pltpu.roll(kf, w * s, 1), 0.0) if s else kf
        sh = w * (nt - 1 - s)
        b = jnp.where(lane_m < S5_ROW - sh, pltpu.roll(kb, S5_ROW - sh, 1), 0.0) if sh else kb
        dd = jnp.where(lane_m // w == s, d_diag, 0.0)
        m_rows.append(f + b + dd)

    n_slot = S5_LANE_PAIRS
    lt = n_slot * w

    pp = pl.program_id(1)

    def order_cols(x):
        return jnp.concatenate([pltpu.roll(x[:, q * lt:(q + 1) * lt], pp * w, 1) for q in range(S5_ROW // lt)], axis=1)

    for k, x in enumerate(wout_rows):
        wout_ref[0, k * S5_ST:(k + 1) * S5_ST, :] = order_cols(x).astype(BF16)
    for s in range(nt):
        r0 = pl.multiple_of((n_slot * (s // n_slot) + (s % n_slot + pp) % n_slot) * w, w)
        m_ref[0, pl.ds(r0, w), :] = order_cols(m_rows[s]).astype(BF16)
        for z in range(2):
            wr, wi = win_rows[z][s]
            win_ref[0, pl.ds(r0, w), (2 * z) * S5_ST:(2 * z + 1) * S5_ST] = wr.astype(BF16)
            win_ref[0, pl.ds(r0, w), (2 * z + 1) * S5_ST:(2 * z + 2) * S5_ST] = wi.astype(BF16)


def _s5_prep(pwk, pwt, ct, b2t, d2):
    npair = d2.shape[0]
    n = S5_LANE_PAIRS
    spec = lambda r, c: pl.BlockSpec((2, 2, 1, r, c), lambda g, p: (0, 0, g * n + p, 0, 0))
    out = pl.BlockSpec((1, S5_ROW, S5_ROW), lambda g, p: (g * n + p, 0, 0))
    shape = jax.ShapeDtypeStruct((npair, S5_ROW, S5_ROW), BF16)
    return pl.pallas_call(
        _s5_prep_body,
        grid=(npair // n, n),
        in_specs=[spec(S5_ST, S5_ST), spec(S5_PWT_ROWS, S5_ST), spec(S5_ST, S5_ST), spec(S5_PAIR_CH, S5_ST),
                  pl.BlockSpec((1, 1, S5_ROW), lambda g, p: (g * n + p, 0, 0))],
        out_specs=[out, out, out],
        out_shape=[shape, shape, shape],
        compiler_params=pltpu.CompilerParams(
            dimension_semantics=("parallel", "arbitrary"), vmem_limit_bytes=VMEM_LIMIT_BYTES),
        name="s5_prep",
    )(pwk, pwt, ct, b2t, d2)


def _s5_weights(a_re, a_im, log_dt, b_re, b_im, c_re, c_im, d_skip):
    nt, npair = S5_BLK, S5_PAIRS
    lam = lax.complex(a_re, a_im)
    dt = jnp.exp(log_dt)[..., None]
    lam_bar = jnp.exp(lam * dt)
    b_bar = ((lam_bar - 1.0) / lam)[..., None] * lax.complex(b_re, b_im)
    taus = jnp.arange(nt + 1, dtype=F32)
    pw = jnp.exp((lam * dt)[..., None] * taus).reshape(2, npair, S5_ST, nt + 1)
    ri = lambda x: jnp.stack([jnp.real(x), jnp.imag(x)], axis=1)
    pwk = jnp.stack([pw[0], pw[1, ..., ::-1]])
    pwk = jnp.pad(ri(pwk), ((0, 0), (0, 0), (0, 0), (0, 0), (0, S5_ST - (nt + 1))))
    pwt = jnp.pad(ri(pw).transpose(0, 1, 2, 4, 3), ((0, 0), (0, 0), (0, 0), (0, S5_PWT_ROWS - (nt + 1)), (0, 0)))
    ct = lax.complex(c_re, c_im).transpose(0, 1, 3, 2).reshape(2, npair, S5_ST, S5_GC)
    ct = jnp.pad(ri(ct), ((0, 0), (0, 0), (0, 0), (0, 0), (0, S5_ST - S5_GC)))
    bb = b_bar.reshape(2, npair, 2, S5_P, S5_GC)
    eye2 = jnp.eye(2, dtype=F32)
    b2t = (bb.transpose(0, 1, 2, 4, 3)[:, :, :, :, None, :] * eye2[None, None, :, None, :, None])
    b2t = ri(b2t.reshape(2, npair, S5_PAIR_CH, S5_ST))
    d2 = jnp.tile(d_skip.reshape(npair, 1, S5_PAIR_CH), (1, 1, nt))
    m2, win2, wout2 = _s5_prep(pwk, pwt, ct, b2t, d2)
    a_blk = pw[..., nt]
    a2 = jnp.stack([jnp.real(a_blk[0]), jnp.imag(a_blk[0]), jnp.real(a_blk[1]), jnp.imag(a_blk[1])], axis=1)
    return m2, win2, wout2, a2


def _mix_body(x_ref, hf_ref, hb_ref, o_ref, y_ref, nw_ref, wglu_ref, wout_ref, out_ref):
    h = hf_ref[...] + hb_ref[...]
    parts = []
    for hd in range(N_HEADS):
        hh = h[:, hd * HEAD_DIM:(hd + 1) * HEAD_DIM]
        mu = jnp.mean(hh, axis=-1, keepdims=True)
        var = jnp.mean(jnp.square(hh - mu), axis=-1, keepdims=True)
        parts.append((hh - mu) * lax.rsqrt(var + EPS))
    hn = jnp.concatenate(parts, axis=1)
    h_m = hn * nw_ref[...] * _sigmoid(o_ref[...])
    y = y_ref[...]
    gelu = 0.5 * y * (1.0 + jnp.tanh(math.sqrt(2.0 / math.pi) * (y + 0.044715 * (y * y * y))))
    ab = jnp.dot(gelu.astype(BF16), wglu_ref[...], preferred_element_type=F32)
    h_s = ab[:, :D_S5] * _sigmoid(ab[:, D_S5:])
    mix = jnp.dot(h_m.astype(BF16), wout_ref[0:D_MLSTM, :], preferred_element_type=F32)
    mix += jnp.dot(h_s.astype(BF16), wout_ref[D_MLSTM:, :], preferred_element_type=F32)
    out_ref[...] = x_ref[...] + mix


def _mix(x, hf, hb, o_in, y, norm_w, w_glu, w_out):
    L = x.shape[0]
    tm = MIX_TM
    assert L % tm == 0
    whole = pl.BlockSpec(memory_space=pltpu.VMEM)
    row = lambda n: pl.BlockSpec((tm, n), lambda i: (i, 0))
    return pl.pallas_call(
        _mix_body,
        grid=(L // tm,),
        in_specs=[row(D_MODEL), row(D_MLSTM), row(D_MLSTM), row(D_MLSTM), row(D_S5), whole, whole, whole],
        out_specs=row(D_MODEL),
        out_shape=jax.ShapeDtypeStruct((L, D_MODEL), F32),
        compiler_params=pltpu.CompilerParams(
            dimension_semantics=("parallel",), vmem_limit_bytes=VMEM_LIMIT_BYTES),
        name="mix",
    )(x, hf, hb, o_in, y, norm_w, w_glu, w_out)


def _encode(x, p):
    x = _ffn(x, p["norm_ffn1"], p["ffn1_w_gate"], p["ffn1_w_up"], p["ffn1_w_down"], p["norm_final"],
             final_norm=False)
    q, kt, v, o_in, u, rb, col = _in_proj(x, p["norm_mix"], p["w_all"], p["w_u"], p["gate_bias_r"],
                                          p["conv_w"], p["conv_b"])
    hf, hb = _mlstm(q, kt, v, rb, col)
    y = _s5(u, p["s5_m"], p["s5_win"], p["s5_wout"], p["s5_a"])
    x = _mix(x, hf, hb, o_in, y, p["mlstm_norm_w"], p["s5_w_glu"], p["w_out"])
    return _ffn(x, p["norm_ffn2"], p["ffn2_w_gate"], p["ffn2_w_up"], p["ffn2_w_down"], p["norm_final"],
                final_norm=True)


def _prepare(norm_ffn1, ffn1_w_gate, ffn1_w_up, ffn1_w_down, norm_mix, w_in, conv_w, conv_b, b_igate, b_fgate,
             mlstm_norm_w, s5_a_re, s5_a_im, s5_log_dt, s5_b_re, s5_b_im, s5_c_re, s5_c_im, s5_d, s5_w_glu,
             w_out, norm_ffn2, ffn2_w_gate, ffn2_w_up, ffn2_w_down, norm_final):
    l = 0
    w = w_in[l]
    g0 = 4 * D_MLSTM
    gate_bias = jnp.concatenate([b_igate[l].reshape(-1), b_fgate[l].reshape(-1)])
    m2, win2, wout2, a2 = _s5_weights(s5_a_re[l], s5_a_im[l], s5_log_dt[l], s5_b_re[l], s5_b_im[l],
                                      s5_c_re[l], s5_c_im[l], s5_d[l])
    row = lambda a: a.reshape(1, -1).astype(F32)
    return {
        "norm_ffn1": row(norm_ffn1[l]), "norm_ffn2": row(norm_ffn2[l]), "norm_final": row(norm_final),
        "ffn1_w_gate": ffn1_w_gate[l].astype(BF16), "ffn1_w_up": ffn1_w_up[l].astype(BF16),
        "ffn1_w_down": ffn1_w_down[l].astype(BF16),
        "ffn2_w_gate": ffn2_w_gate[l].astype(BF16), "ffn2_w_up": ffn2_w_up[l].astype(BF16),
        "ffn2_w_down": ffn2_w_down[l].astype(BF16),
        "norm_mix": row(norm_mix[l]),
        "w_all": w.astype(BF16),
        "w_u": w[:, g0 + N_GATES:].astype(BF16),
        "conv_w": conv_w[l].astype(F32), "conv_b": row(conv_b[l]),
        "gate_bias_r": gate_bias.reshape(N_GATES, 1),
        "mlstm_norm_w": row(mlstm_norm_w[l]),
        "s5_m": m2, "s5_win": win2, "s5_wout": wout2, "s5_a": a2,
        "s5_w_glu": s5_w_glu[l].astype(BF16), "w_out": w_out[l].astype(BF16),
    }


def kernel(x_prompt, x_sample, norm_ffn1, ffn1_w_gate, ffn1_w_up, ffn1_w_down, norm_mix, w_in, conv_w, conv_b, b_igate, b_fgate, mlstm_norm_w, s5_a_re, s5_a_im, s5_log_dt, s5_b_re, s5_b_im, s5_c_re, s5_c_im, s5_d, s5_w_glu, w_out, norm_ffn2, ffn2_w_gate, ffn2_w_up, ffn2_w_down, norm_final):
    assert norm_ffn1.shape[0] == 1 and x_prompt.shape[0] == 1 and x_sample.shape[0] == 1
    p = _prepare(norm_ffn1, ffn1_w_gate, ffn1_w_up, ffn1_w_down, norm_mix, w_in, conv_w, conv_b, b_igate,
                 b_fgate, mlstm_norm_w, s5_a_re, s5_a_im, s5_log_dt, s5_b_re, s5_b_im, s5_c_re, s5_c_im, s5_d,
                 s5_w_glu, w_out, norm_ffn2, ffn2_w_gate, ffn2_w_up, ffn2_w_down, norm_final)
    y_prompt = _encode(x_prompt[0], p)[None]
    y_sample = _encode(x_sample[0], p)[None]
    return (y_prompt, y_sample)
```
